```python
import jax, jax.numpy as jnp
from jax import lax
import numpy as np

D_MODEL = 1024
BATCH = 16
SEQ = 4096
DEPTH = 1

GRID_W = 64
CTX_LEN = 256
HEAD_DIM = 64
KV_HEADS = 2
A_HEADS = 8
B_HEADS = 8
GROUP = A_HEADS // KV_HEADS
Q_BLOCK = 128
WINDOW = 128
BAND = Q_BLOCK + 2 * WINDOW
ROPE_HALF = HEAD_DIM // 2
ROPE_THETA = 10000.0
N_EXPERTS = 32
TOP_K = 4
D_EXPERT = D_MODEL
SWIGLU_ALPHA = 1.702
SWIGLU_LIMIT = 7.0
RMS_EPS = 1e-6
ATTN_SCALE = HEAD_DIM ** -0.5
NEG_INF = -1e30
KV_W = KV_HEADS * HEAD_DIM
Q_A = A_HEADS * HEAD_DIM
Q_B = B_HEADS * HEAD_DIM
KV_COLS = 4 * KV_W
IN_COLS = KV_COLS + Q_A + Q_B + 2 * D_MODEL

kernel_name = 'hybrid_gated_gqa_window_moe_dit_block'


def rms_norm(x, g):
    xf = x.astype(jnp.float32)
    y = xf * lax.rsqrt(jnp.mean(xf * xf, axis=-1, keepdims=True) + RMS_EPS)
    return (y * g.astype(jnp.float32)).astype(x.dtype)


def adaln(cond, w, b):
    return jnp.split(jax.nn.silu(cond) @ w + b, 6, axis=-1)


def modulate(h, shift, scale):
    return h * (1 + scale) + shift


def to_heads(t):
    return t.reshape(t.shape[:-1] + (t.shape[-1] // HEAD_DIM, HEAD_DIM))


def split_kv(p):
    k_a, v_a, k_b, v_b = jnp.split(p, [KV_W, 2 * KV_W, 3 * KV_W], axis=-1)
    return to_heads(k_a), to_heads(v_a), to_heads(k_b), to_heads(v_b)


def split_qg(p):
    q_a, q_b, g_a, g_b = jnp.split(p, [Q_A, Q_A + Q_B, Q_A + Q_B + D_MODEL], axis=-1)
    return to_heads(q_a), to_heads(q_b), g_a, g_b


def axis_table(pos):
    inv = ROPE_THETA ** (-(jnp.arange(ROPE_HALF // 2, dtype=jnp.float32) * 2.0 / ROPE_HALF))
    ang = pos.astype(jnp.float32)[:, None] * inv[None, :]
    return jnp.cos(ang), jnp.sin(ang)


def rope_axis(x, cos, sin):
    x1, x2 = x[..., :ROPE_HALF // 2], x[..., ROPE_HALF // 2:]
    c, s = cos[:, None, :], sin[:, None, :]
    return jnp.concatenate([x1 * c - x2 * s, x2 * c + x1 * s], axis=-1)


def rope_2d(x, tabs):
    (cr, sr), (cc, sc) = tabs
    y = jnp.concatenate([rope_axis(x[..., :ROPE_HALF], cr, sr),
                         rope_axis(x[..., ROPE_HALF:], cc, sc)], axis=-1)
    return y.astype(x.dtype)


def group_q(q):
    return q.reshape(q.shape[:-2] + (KV_HEADS, q.shape[-2] // KV_HEADS, HEAD_DIM))


def attend(q, k, v, mask=None, sink=None):
    s = jnp.einsum('bqkgd,bskd->bkgqs', q, k).astype(jnp.float32) * ATTN_SCALE
    if mask is not None:
        s = jnp.where(mask, s, NEG_INF)
    if sink is not None:
        sk = jnp.broadcast_to(sink.astype(jnp.float32).reshape(KV_HEADS, -1, 1, 1), s.shape[:-1] + (1,))
        p = jax.nn.softmax(jnp.concatenate([s, sk], axis=-1), axis=-1)[..., :-1]
    else:
        p = jax.nn.softmax(s, axis=-1)
    return jnp.einsum('bkgqs,bskd->bqkgd', p.astype(v.dtype), v)


def global_attn(q, k, v):
    b, s = q.shape[:2]
    nb = s // Q_BLOCK
    qb = group_q(q).reshape(b, nb, Q_BLOCK, KV_HEADS, GROUP, HEAD_DIM).swapaxes(0, 1)
    out = lax.map(lambda qblk: attend(qblk, k, v), qb)
    return out.swapaxes(0, 1).reshape(b, s, A_HEADS * HEAD_DIM)


def band_blocks(t, nb):
    b = t.shape[0]
    tp = jnp.pad(t, ((0, 0), (WINDOW, WINDOW), (0, 0), (0, 0)))
    tb = tp.reshape(b, nb + 2, Q_BLOCK, KV_HEADS, HEAD_DIM)
    band = jnp.concatenate([tb[:, :-2], tb[:, 1:-1], tb[:, 2:]], axis=2)
    return band.swapaxes(0, 1)


def window_attn(q, k, v, k_ctx, v_ctx, sink):
    b, s = q.shape[:2]
    nb = s // Q_BLOCK
    n_ctx = k_ctx.shape[1]
    qb = group_q(q).reshape(b, nb, Q_BLOCK, KV_HEADS, GROUP, HEAD_DIM).swapaxes(0, 1)
    offs = jnp.arange(BAND) - WINDOW
    rel_ok = jnp.abs(offs[None, :] - jnp.arange(Q_BLOCK)[:, None]) <= WINDOW
    ctx_ok = jnp.ones((Q_BLOCK, n_ctx), dtype=bool)

    def one(args):
        qblk, kb, vb, n = args
        kpos = n * Q_BLOCK + offs
        band_ok = rel_ok & ((kpos >= 0) & (kpos < s))[None, :]
        mask = jnp.concatenate([band_ok, ctx_ok], axis=-1)
        return attend(qblk, jnp.concatenate([kb, k_ctx], axis=1),
                      jnp.concatenate([vb, v_ctx], axis=1), mask, sink)

    out = lax.map(one, (qb, band_blocks(k, nb), band_blocks(v, nb), jnp.arange(nb)))
    return out.swapaxes(0, 1).reshape(b, s, B_HEADS * HEAD_DIM)


def dense_attn(q, k, v, sink=None):
    return attend(group_q(q), k, v, None, sink).reshape(q.shape[:2] + (-1,))


def merge_branches(y_a, y_b, g_a, g_b, w_br_a, w_br_b, w_o):
    return (jax.nn.sigmoid(g_a) * (y_a @ w_br_a) + jax.nn.sigmoid(g_b) * (y_b @ w_br_b)) @ w_o


def moe(h, w_router, b_router, w_e1, b_e1, w_e2, b_e2):
    shape = h.shape
    t = h.reshape(-1, shape[-1])
    logits = (t @ w_router + b_router).astype(jnp.float32)
    top_logit, top_idx = lax.top_k(logits, TOP_K)
    top_w = jax.nn.softmax(top_logit, axis=-1)
    combine = jnp.einsum('tk,tke->te', top_w,
                         jax.nn.one_hot(top_idx, N_EXPERTS, dtype=jnp.float32)).astype(t.dtype)
    out = jnp.zeros_like(t)
    for e in range(N_EXPERTS):
        u = t @ w_e1[e] + b_e1[e]
        u_glu = jnp.minimum(u[:, 0::2], SWIGLU_LIMIT)
        u_lin = jnp.clip(u[:, 1::2], -SWIGLU_LIMIT, SWIGLU_LIMIT)
        act = u_glu * jax.nn.sigmoid(SWIGLU_ALPHA * u_glu) * (u_lin + 1)
        out = out + combine[:, e:e + 1] * (act @ w_e2[e] + b_e2[e])
    return out.reshape(shape)


def setup_inputs(seed: int = 0) -> dict:
    key = jax.random.key(seed)
    ks = jax.random.split(key, 22)
    L, D = DEPTH, D_MODEL

    def nrm(k, shape, scale):
        return jax.random.normal(k, shape, jnp.float32) * scale

    return {
        'x': nrm(ks[0], (BATCH, SEQ, D), 1.0),
        'c': nrm(ks[1], (BATCH, D), 1.0),
        'ctx': nrm(ks[2], (BATCH, CTX_LEN, D), 1.0),
        'c_ctx': nrm(ks[3], (D,), 1.0),
        'w_mod': nrm(ks[4], (L, D, 6 * D), 0.5 * D ** -0.5),
        'b_mod': nrm(ks[5], (L, 6 * D), 0.01),
        'norm1_g': 1.0 + nrm(ks[6], (L, D), 0.02),
        'norm2_g': 1.0 + nrm(ks[7], (L, D), 0.02),
        'w_in': nrm(ks[8], (L, D, IN_COLS), D ** -0.5),
        'q_norm_g': 1.0 + nrm(ks[9], (L, HEAD_DIM), 0.02),
        'k_norm_g': 1.0 + nrm(ks[10], (L, HEAD_DIM), 0.02),
        'sink': nrm(ks[11], (L, B_HEADS), 0.5),
        'w_br_a': nrm(ks[12], (L, Q_A, D), Q_A ** -0.5),
        'w_br_b': nrm(ks[13], (L, Q_B, D), Q_B ** -0.5),
        'w_o': nrm(ks[14], (L, D, D), D ** -0.5),
        'w_router': nrm(ks[15], (L, D, N_EXPERTS), D ** -0.5),
        'b_router': nrm(ks[16], (L, N_EXPERTS), 0.01),
        'w_e1': nrm(ks[17], (L, N_EXPERTS, D, 2 * D_EXPERT), D ** -0.5),
        'b_e1': nrm(ks[18], (L, N_EXPERTS, 2 * D_EXPERT), 0.01),
        'w_e2': nrm(ks[19], (L, N_EXPERTS, D_EXPERT, D), D_EXPERT ** -0.5),
        'b_e2': nrm(ks[20], (L, N_EXPERTS, D), 0.01),
        'final_g': 1.0 + nrm(ks[21], (D,), 0.02),
    }


def reference(x, c, ctx, c_ctx, w_mod, b_mod, norm1_g, norm2_g, w_in, q_norm_g, k_norm_g,
              sink, w_br_a, w_br_b, w_o, w_router, b_router, w_e1, b_e1, w_e2, b_e2, final_g):
    s = x.shape[1]
    rows = s // GRID_W
    row = jnp.repeat(jnp.arange(rows), GRID_W)
    col = jnp.tile(jnp.arange(GRID_W), rows)
    tabs = (axis_table(row), axis_table(col))
    for l in range(DEPTH):
        sh1, sc1, g1, sh2, sc2, g2 = [m[:, None, :] for m in adaln(c, w_mod[l], b_mod[l])]
        csh1, csc1, cg1, csh2, csc2, cg2 = adaln(c_ctx, w_mod[l], b_mod[l])

        h = modulate(rms_norm(x, norm1_g[l]), sh1, sc1)
        hc = modulate(rms_norm(ctx, norm1_g[l]), csh1, csc1)
        p = h @ w_in[l]
        k_a, v_a, k_b, v_b = split_kv(p[..., :KV_COLS])
        q_a, q_b, g_a, g_b = split_qg(p[..., KV_COLS:])
        k_a_c, v_a_c, k_b_c, v_b_c = split_kv(hc @ w_in[l, :, :KV_COLS])
        k_a_c = rms_norm(k_a_c, k_norm_g[l])

        q_a = rope_2d(rms_norm(q_a, q_norm_g[l]), tabs)
        k_a = rope_2d(rms_norm(k_a, k_norm_g[l]), tabs)
        q_b = rope_2d(q_b, tabs)
        k_b = rope_2d(k_b, tabs)

        y_a = global_attn(q_a, jnp.concatenate([k_a, k_a_c], axis=1),
                          jnp.concatenate([v_a, v_a_c], axis=1))
        y_b = window_attn(q_b, k_b, v_b, k_b_c, v_b_c, sink[l])
        x = x + g1 * merge_branches(y_a, y_b, g_a, g_b, w_br_a[l], w_br_b[l], w_o[l])

        h2 = modulate(rms_norm(x, norm2_g[l]), sh2, sc2)
        x = x + g2 * moe(h2, w_router[l], b_router[l], w_e1[l], b_e1[l], w_e2[l], b_e2[l])

        if l < DEPTH - 1:
            q_a_c, q_b_c, g_a_c, g_b_c = split_qg(hc @ w_in[l, :, KV_COLS:])
            y_a_c = dense_attn(rms_norm(q_a_c, q_norm_g[l]), k_a_c, v_a_c)
            y_b_c = dense_attn(q_b_c, k_b_c, v_b_c, sink[l])
            ctx = ctx + cg1 * merge_branches(y_a_c, y_b_c, g_a_c, g_b_c, w_br_a[l], w_br_b[l], w_o[l])
            hc2 = modulate(rms_norm(ctx, norm2_g[l]), csh2, csc2)
            ctx = ctx + cg2 * moe(hc2, w_router[l], b_router[l], w_e1[l], b_e1[l], w_e2[l], b_e2[l])
    return rms_norm(x, final_g)
```

```python
import functools

import jax
import jax.numpy as jnp
from jax import lax
from jax.experimental import pallas as pl
from jax.experimental.pallas import tpu as pltpu

HEAD_DIM = 64
KV_HEADS = 2
A_HEADS = 8
B_HEADS = 8
GROUP = A_HEADS // KV_HEADS
GRID_W = 64
WINDOW = 128
ROPE_HALF = HEAD_DIM // 2
ROPE_QUARTER = ROPE_HALF // 2
ROPE_THETA = 10000.0
N_EXPERTS = 32
TOP_K = 4
SWIGLU_ALPHA = 1.702
SWIGLU_LIMIT = 7.0
RMS_EPS = 1e-6
ATTN_SCALE = HEAD_DIM ** -0.5
NEG_INF = -1e30
KV_W = KV_HEADS * HEAD_DIM
Q_A = A_HEADS * HEAD_DIM
Q_B = B_HEADS * HEAD_DIM

TOKEN_TILE = 256
MOE_CHUNK = 256
VMEM_LIMIT = 56 * 1024 * 1024

_NT = (((1,), (1,)), ((), ()))


def _dot_nt(a, b, precision=None):
    return lax.dot_general(a, b, _NT, preferred_element_type=jnp.float32, precision=precision)


def _dot(a, b, precision=None):
    return jnp.dot(a, b, preferred_element_type=jnp.float32, precision=precision)


def _params(sem):
    return pltpu.CompilerParams(dimension_semantics=sem, vmem_limit_bytes=VMEM_LIMIT)


def _adaln_kernel(cond_ref, w_ref, b_ref, o_ref):
    cond = cond_ref[...]
    act = cond * jax.nn.sigmoid(cond)
    o_ref[...] = _dot(act, w_ref[...], precision=lax.Precision.HIGHEST) + b_ref[...]


def _adaln(cond, w, b):
    rows, d = cond.shape
    n = w.shape[1]
    tn = 1024
    return pl.pallas_call(
        _adaln_kernel,
        grid=(n // tn,),
        in_specs=[pl.BlockSpec((rows, d), lambda j: (0, 0)),
                  pl.BlockSpec((d, tn), lambda j: (0, j)),
                  pl.BlockSpec((1, tn), lambda j: (0, j))],
        out_specs=pl.BlockSpec((rows, tn), lambda j: (0, j)),
        out_shape=jax.ShapeDtypeStruct((rows, n), jnp.float32),
        compiler_params=_params(("arbitrary",)),
        name="adaln",
    )(cond, w, b.reshape(1, n))


def _rope_t(xh, cos, sin):
    q = ROPE_QUARTER
    a0, a1, b0, b1 = xh[0:q], xh[q:2 * q], xh[2 * q:3 * q], xh[3 * q:4 * q]
    cr, cc = cos[0:q], cos[q:2 * q]
    sr, sc = sin[0:q], sin[q:2 * q]
    return jnp.concatenate([a0 * cr - a1 * sr, a1 * cr + a0 * sr,
                            b0 * cc - b1 * sc, b1 * cc + b0 * sc], axis=0)


def _head_norm_t(xh, g):
    ms = jnp.mean(xh * xh, axis=0, keepdims=True)
    return xh * lax.rsqrt(ms + RMS_EPS) * g


def _mod_norm(x, g, shift, scale):
    ms = jnp.mean(x * x, axis=-1, keepdims=True)
    return x * lax.rsqrt(ms + RMS_EPS) * g * (1.0 + scale) + shift


def _in_proj_kernel(n_lat, x_ref, ctx_ref, mod_ref, g_ref, wq_ref, wk_ref, wv_ref, wg_ref,
                    qg_ref, kg_ref, cos_ref, sin_ref,
                    qa_ref, qb_ref, ka_ref, kb_ref, va_ref, vb_ref, gate_ref, h_ref):
    i = pl.program_id(1)
    shift = mod_ref[0, 0, 0:1, :]
    scale = mod_ref[0, 0, 1:2, :]

    @pl.when(i < n_lat)
    def _():
        h_ref[...] = _mod_norm(x_ref[0], g_ref[...], shift, scale).astype(jnp.bfloat16)

    @pl.when(i >= n_lat)
    def _():
        h_ref[...] = _mod_norm(ctx_ref[0], g_ref[...], shift, scale).astype(jnp.bfloat16)

    h = h_ref[...]
    cos = cos_ref[...]
    sin = sin_ref[...]

    kt = _dot_nt(wk_ref[...], h)
    kg = kg_ref[...]
    ka = [_rope_t(_head_norm_t(kt[j * HEAD_DIM:(j + 1) * HEAD_DIM], kg), cos, sin) for j in range(KV_HEADS)]
    kb = [_rope_t(kt[KV_W + j * HEAD_DIM:KV_W + (j + 1) * HEAD_DIM], cos, sin) for j in range(KV_HEADS)]
    ka_ref[0, 0] = jnp.concatenate(ka, axis=0).T.astype(jnp.bfloat16)
    kb_ref[0, 0] = jnp.concatenate(kb, axis=0).T.astype(jnp.bfloat16)
    vt = _dot_nt(wv_ref[...], h)
    va_ref[0, 0] = vt[0:KV_W].astype(jnp.bfloat16)
    vb_ref[0, 0] = vt[KV_W:2 * KV_W].astype(jnp.bfloat16)

    @pl.when(i < n_lat)
    def _():
        qg = qg_ref[...]
        for half in range(2):
            qt = _dot_nt(wq_ref[half * Q_A:(half + 1) * Q_A, :], h)
            for hd in range(A_HEADS):
                xh = qt[hd * HEAD_DIM:(hd + 1) * HEAD_DIM]
                if half == 0:
                    out = _rope_t(_head_norm_t(xh, qg), cos, sin)
                    qa_ref[0, hd * HEAD_DIM:(hd + 1) * HEAD_DIM, :] = out.astype(jnp.bfloat16)
                else:
                    out = _rope_t(xh * ATTN_SCALE, cos, sin)
                    qb_ref[0, hd * HEAD_DIM:(hd + 1) * HEAD_DIM, :] = out.astype(jnp.bfloat16)
        rows = 512
        for c in range(wg_ref.shape[0] // rows):
            gt = _dot_nt(wg_ref[c * rows:(c + 1) * rows, :], h)
            gate_ref[0, c * rows:(c + 1) * rows, :] = jax.nn.sigmoid(gt).astype(jnp.bfloat16)


def _in_proj(x, ctx, mod, norm_g, wq_t, wk_t, wv_t, wg_t, qg, kg, cos, sin):
    b, s, d = x.shape
    n_ctx = ctx.shape[1]
    t = TOKEN_TILE
    n_lat = s // t
    n_tiles = n_lat + n_ctx // t
    last = n_lat - 1
    full = lambda shape: pl.BlockSpec(shape, lambda bi, i: (0,) * len(shape))
    q_spec = pl.BlockSpec((1, Q_A, t), lambda bi, i: (bi, 0, jnp.minimum(i, last)))
    k_spec = pl.BlockSpec((1, 1, t, KV_W), lambda bi, i: (bi, i, 0, 0))
    v_spec = pl.BlockSpec((1, 1, KV_W, t), lambda bi, i: (bi, i, 0, 0))
    bf = jnp.bfloat16
    return pl.pallas_call(
        functools.partial(_in_proj_kernel, n_lat),
        grid=(b, n_tiles),
        in_specs=[pl.BlockSpec((1, t, d), lambda bi, i: (bi, jnp.minimum(i, last), 0)),
                  pl.BlockSpec((1, t, d), lambda bi, i: (bi, jnp.maximum(i - n_lat, 0), 0)),
                  pl.BlockSpec((1, 1, 2, d), lambda bi, i: (bi, i // n_lat, 0, 0)),
                  full((1, d)), full(wq_t.shape), full(wk_t.shape), full(wv_t.shape), full(wg_t.shape),
                  full((HEAD_DIM, t)), full((HEAD_DIM, t)),
                  pl.BlockSpec((ROPE_HALF, t), lambda bi, i: (0, i)),
                  pl.BlockSpec((ROPE_HALF, t), lambda bi, i: (0, i))],
        out_specs=[q_spec, q_spec, k_spec, k_spec, v_spec, v_spec,
                   pl.BlockSpec((1, 2 * d, t), lambda bi, i: (bi, 0, jnp.minimum(i, last)))],
        out_shape=[jax.ShapeDtypeStruct((b, Q_A, s), bf), jax.ShapeDtypeStruct((b, Q_B, s), bf),
                   jax.ShapeDtypeStruct((b, n_tiles, t, KV_W), bf), jax.ShapeDtypeStruct((b, n_tiles, t, KV_W), bf),
                   jax.ShapeDtypeStruct((b, n_tiles, KV_W, t), bf), jax.ShapeDtypeStruct((b, n_tiles, KV_W, t), bf),
                   jax.ShapeDtypeStruct((b, 2 * d, s), bf)],
        scratch_shapes=[pltpu.VMEM((t, d), bf)],
        compiler_params=_params(("arbitrary", "arbitrary")),
        name="in_proj",
    )(x, ctx, mod, norm_g, wq_t, wk_t, wv_t, wg_t, qg, kg, cos, sin)


def _attn_kernel(windowed, tq, n_lat, n_tiles, sink_ref, q_ref, k_ref, v_ref, y_ref):
    qi = pl.program_id(1)
    t = TOKEN_TILE
    n = GROUP * tq
    for kvh in range(KV_HEADS):
        heads = [kvh * GROUP + g for g in range(GROUP)]
        qcat = jnp.concatenate([q_ref[0, hd * HEAD_DIM:(hd + 1) * HEAD_DIM, :] for hd in heads], axis=1)
        zeros = jnp.zeros_like(qcat)
        qpad = jnp.concatenate([qcat, zeros] if kvh == 0 else [zeros, qcat], axis=0)

        def block(kb, carry, mask=None):
            m, l, acc = carry
            s = _dot(k_ref[0, kb], qpad)
            if mask is not None:
                s = jnp.where(mask, s, NEG_INF)
            m_new = jnp.maximum(m, jnp.max(s, axis=0, keepdims=True))
            alpha = jnp.exp(m - m_new)
            p = jnp.exp(s - m_new)
            l = l * alpha + jnp.sum(p, axis=0, keepdims=True)
            vblk = v_ref[0, kb, kvh * HEAD_DIM:(kvh + 1) * HEAD_DIM, :]
            acc = acc * alpha + _dot(vblk, p.astype(jnp.bfloat16))
            return m_new, l, acc

        acc0 = jnp.zeros((HEAD_DIM, n), jnp.float32)
        if windowed:
            m0 = jnp.concatenate([jnp.full((1, tq), sink_ref[hd], jnp.float32) for hd in heads], axis=1)
            carry = (m0, jnp.ones((1, n), jnp.float32), acc0)
            qpos = qi * tq + lax.rem(lax.broadcasted_iota(jnp.int32, (t, n), 1), tq)
            krow = lax.broadcasted_iota(jnp.int32, (t, n), 0)
            first = (qi * tq) // t - 1
            for j in range(tq // t + 2):
                blk = first + j
                kpos = blk * t + krow
                ok = (jnp.abs(kpos - qpos) <= WINDOW) & (kpos >= 0) & (kpos < n_lat * t)
                carry = block(jnp.clip(blk, 0, n_lat - 1), carry, ok)
            carry = lax.fori_loop(n_lat, n_tiles, block, carry)
        else:
            carry = (jnp.full((1, n), -jnp.inf, jnp.float32), jnp.zeros((1, n), jnp.float32), acc0)
            carry = lax.fori_loop(0, n_tiles, block, carry)
        _, l, acc = carry
        out = acc / l
        for g, hd in enumerate(heads):
            y_ref[0, hd * HEAD_DIM:(hd + 1) * HEAD_DIM, :] = out[:, g * tq:(g + 1) * tq].astype(jnp.bfloat16)


def _attention(windowed, tq, sink, q_t, k, v_t):
    b, hq, s = q_t.shape
    n_tiles = k.shape[1]
    n_lat = s // TOKEN_TILE
    grid_spec = pltpu.PrefetchScalarGridSpec(
        num_scalar_prefetch=1,
        grid=(b, s // tq),
        in_specs=[pl.BlockSpec((1, hq, tq), lambda bi, qi, sk: (bi, 0, qi)),
                  pl.BlockSpec((1,) + k.shape[1:], lambda bi, qi, sk: (bi, 0, 0, 0)),
                  pl.BlockSpec((1,) + v_t.shape[1:], lambda bi, qi, sk: (bi, 0, 0, 0))],
        out_specs=pl.BlockSpec((1, hq, tq), lambda bi, qi, sk: (bi, 0, qi)),
    )
    return pl.pallas_call(
        functools.partial(_attn_kernel, windowed, tq, n_lat, n_tiles),
        grid_spec=grid_spec,
        out_shape=jax.ShapeDtypeStruct((b, hq, s), jnp.bfloat16),
        compiler_params=_params(("arbitrary", "arbitrary")),
        name="attn_window" if windowed else "attn_global",
    )(sink, q_t, k, v_t)


def _merge_kernel(x_ref, ya_ref, yb_ref, gate_ref, wa_ref, wb_ref, wo_ref, g1_ref, mod_ref, g2_ref,
                  wr_ref, br_ref, x1_ref, h2_ref, idx_ref, wt_ref):
    d = x_ref.shape[-1]
    za = _dot(wa_ref[...], ya_ref[0])
    zb = _dot(wb_ref[...], yb_ref[0])
    mt = gate_ref[0, 0:d, :].astype(jnp.float32) * za + gate_ref[0, d:2 * d, :].astype(jnp.float32) * zb
    o = _dot(mt.T.astype(jnp.bfloat16), wo_ref[...])
    x1 = x_ref[0] + g1_ref[0] * o
    x1_ref[0] = x1
    h2 = _mod_norm(x1, g2_ref[...], mod_ref[0, 0:1, :], mod_ref[0, 1:2, :])
    h2_ref[0] = h2
    logits = _dot_nt(wr_ref[...], h2, precision=lax.Precision.HIGHEST) + br_ref[...]
    n_e = logits.shape[0]
    row = lax.broadcasted_iota(jnp.int32, logits.shape, 0)
    work = logits
    top_v, top_i = [], []
    for _ in range(TOP_K):
        mk = jnp.max(work, axis=0, keepdims=True)
        ik = jnp.min(jnp.where(work == mk, row, n_e), axis=0, keepdims=True)
        top_v.append(mk)
        top_i.append(ik)
        work = jnp.where(row == ik, -jnp.inf, work)
    ex = [jnp.exp(v - top_v[0]) for v in top_v]
    den = ex[0] + ex[1] + ex[2] + ex[3]
    idx_ref[0] = jnp.concatenate(top_i, axis=0)
    wt_ref[0] = jnp.concatenate([e / den for e in ex], axis=0)


def _merge(x, ya_t, yb_t, gate_t, wa_t, wb_t, wo, g1, mod2, norm2_g, wr_t, br):
    b, s, d = x.shape
    t = TOKEN_TILE
    full = lambda shape: pl.BlockSpec(shape, lambda bi, i: (0,) * len(shape))
    tok = pl.BlockSpec((1, t, d), lambda bi, i: (bi, i, 0))
    col = lambda rows: pl.BlockSpec((1, rows, t), lambda bi, i: (bi, 0, i))
    return pl.pallas_call(
        _merge_kernel,
        grid=(b, s // t),
        in_specs=[tok, col(Q_A), col(Q_B), col(2 * d),
                  full(wa_t.shape), full(wb_t.shape), full(wo.shape),
                  pl.BlockSpec((1, 1, d), lambda bi, i: (bi, 0, 0)),
                  pl.BlockSpec((1, 2, d), lambda bi, i: (bi, 0, 0)),
                  full((1, d)), full(wr_t.shape), full(br.shape)],
        out_specs=[tok, tok, col(TOP_K), col(TOP_K)],
        out_shape=[jax.ShapeDtypeStruct((b, s, d), jnp.float32), jax.ShapeDtypeStruct((b, s, d), jnp.float32),
                   jax.ShapeDtypeStruct((b, TOP_K, s), jnp.int32), jax.ShapeDtypeStruct((b, TOP_K, s), jnp.float32)],
        compiler_params=_params(("arbitrary", "arbitrary")),
        name="merge_router",
    )(x, ya_t, yb_t, gate_t, wa_t, wb_t, wo, g1, mod2, norm2_g, wr_t, br)


def _moe_kernel(offs_ref, tok_ref, wt_ref, h_ref, w1g_ref, w1l_ref, b1g_ref, b1l_ref, w2_ref, b2_ref,
                o_ref, xg_ref, y_ref, acc_ref):
    ti = pl.program_id(0)
    e = pl.program_id(1)
    n_e = pl.num_programs(1)
    ch = xg_ref.shape[0]

    @pl.when((ti == 0) & (e == 0))
    def _():
        xg_ref[...] = jnp.zeros_like(xg_ref)

    @pl.when(e == 0)
    def _():
        acc_ref[...] = jnp.zeros_like(acc_ref)

    start = offs_ref[ti * (n_e + 1) + e]
    end = offs_ref[ti * (n_e + 1) + e + 1]

    def chunk(c, _):
        base = start + c * ch
        n = jnp.minimum(ch, end - base)

        def gather(r, _):
            tok = tok_ref[0, 0, base + r]
            xg_ref[pl.ds(r, 1), :] = h_ref[pl.ds(tok, 1), :]
            return 0

        lax.fori_loop(0, n, gather, 0)
        xg = xg_ref[...].astype(jnp.bfloat16)
        glu = jnp.minimum(_dot(xg, w1g_ref[0]) + b1g_ref[0], SWIGLU_LIMIT)
        lin = jnp.clip(_dot(xg, w1l_ref[0]) + b1l_ref[0], -SWIGLU_LIMIT, SWIGLU_LIMIT)
        act = glu * jax.nn.sigmoid(SWIGLU_ALPHA * glu) * (lin + 1.0)
        y_ref[...] = _dot(act.astype(jnp.bfloat16), w2_ref[0]) + b2_ref[0]

        def scatter(r, _):
            tok = tok_ref[0, 0, base + r]
            w = wt_ref[0, 0, base + r]
            acc_ref[pl.ds(tok, 1), :] = acc_ref[pl.ds(tok, 1), :] + w * y_ref[pl.ds(r, 1), :]
            return 0

        lax.fori_loop(0, n, scatter, 0)
        return 0

    lax.fori_loop(0, (end - start + ch - 1) // ch, chunk, 0)

    @pl.when(e == n_e - 1)
    def _():
        o_ref[...] = acc_ref[...]


def _moe(offs, tok_sorted, wt_sorted, h2, w1g, w1l, b1g, b1l, w2, b2, tile):
    n_tok, d = h2.shape
    n_e, _, d_e = w1g.shape
    n_t = n_tok // tile
    a = tok_sorted.shape[-1]
    smem = lambda: pl.BlockSpec((1, 1, a), lambda ti, e, o: (ti, 0, 0), memory_space=pltpu.SMEM)
    ex = lambda shape: pl.BlockSpec((1,) + shape, lambda ti, e, o: (e, 0, 0))
    grid_spec = pltpu.PrefetchScalarGridSpec(
        num_scalar_prefetch=1,
        grid=(n_t, n_e),
        in_specs=[smem(), smem(),
                  pl.BlockSpec((tile, d), lambda ti, e, o: (ti, 0)),
                  ex((d, d_e)), ex((d, d_e)), ex((1, d_e)), ex((1, d_e)), ex((d_e, d)), ex((1, d))],
        out_specs=pl.BlockSpec((tile, d), lambda ti, e, o: (ti, 0)),
        scratch_shapes=[pltpu.VMEM((MOE_CHUNK, d), jnp.float32), pltpu.VMEM((MOE_CHUNK, d), jnp.float32),
                        pltpu.VMEM((tile, d), jnp.float32)],
    )
    return pl.pallas_call(
        _moe_kernel,
        grid_spec=grid_spec,
        out_shape=jax.ShapeDtypeStruct((n_tok, d), jnp.float32),
        compiler_params=_params(("arbitrary", "arbitrary")),
        name="moe",
    )(offs, tok_sorted, wt_sorted, h2, w1g, w1l, b1g, b1l, w2, b2)


def _final_kernel(x_ref, m_ref, g2_ref, g_ref, o_ref):
    x = x_ref[0] + g2_ref[0] * m_ref[0]
    ms = jnp.mean(x * x, axis=-1, keepdims=True)
    o_ref[0] = x * lax.rsqrt(ms + RMS_EPS) * g_ref[...]


def _final(x1, moe_out, g2, final_g):
    b, s, d = x1.shape
    t = 512
    tok = pl.BlockSpec((1, t, d), lambda bi, i: (bi, i, 0))
    return pl.pallas_call(
        _final_kernel,
        grid=(b, s // t),
        in_specs=[tok, tok, pl.BlockSpec((1, 1, d), lambda bi, i: (bi, 0, 0)),
                  pl.BlockSpec((1, d), lambda bi, i: (0, 0))],
        out_specs=tok,
        out_shape=jax.ShapeDtypeStruct((b, s, d), jnp.float32),
        compiler_params=_params(("arbitrary", "arbitrary")),
        name="final_norm",
    )(x1, moe_out, g2, final_g)


def _rope_tables(s, n_ctx):
    inv = ROPE_THETA ** (-(jnp.arange(ROPE_QUARTER, dtype=jnp.float32) * 2.0 / ROPE_HALF))
    pos = jnp.arange(s)
    ang_r = (pos // GRID_W).astype(jnp.float32)[None, :] * inv[:, None]
    ang_c = (pos % GRID_W).astype(jnp.float32)[None, :] * inv[:, None]
    ang = jnp.concatenate([ang_r, ang_c], axis=0)
    cos = jnp.concatenate([jnp.cos(ang), jnp.ones((ROPE_HALF, n_ctx), jnp.float32)], axis=1)
    sin = jnp.concatenate([jnp.sin(ang), jnp.zeros((ROPE_HALF, n_ctx), jnp.float32)], axis=1)
    return cos, sin


def _route_lists(top_idx, top_w, tile):
    b, k, s = top_idx.shape
    n_t = b * s // tile
    per = s // tile
    e_flat = top_idx.reshape(b, k, per, tile).transpose(0, 2, 1, 3).reshape(n_t, k * tile)
    w_flat = top_w.reshape(b, k, per, tile).transpose(0, 2, 1, 3).reshape(n_t, k * tile)
    tok = jnp.tile(jnp.arange(tile, dtype=jnp.int32), k)[None, :].repeat(n_t, axis=0)
    e_sorted, tok_sorted, w_sorted = lax.sort((e_flat, tok, w_flat), dimension=1, num_keys=2)
    counts = jnp.sum(e_flat[:, :, None] == jnp.arange(N_EXPERTS, dtype=jnp.int32)[None, None, :], axis=1,
                     dtype=jnp.int32)
    offs = jnp.concatenate([jnp.zeros((n_t, 1), jnp.int32), jnp.cumsum(counts, axis=1, dtype=jnp.int32)], axis=1)
    return offs.reshape(-1), tok_sorted[:, None, :], w_sorted[:, None, :]


def kernel(x, c, ctx, c_ctx, w_mod, b_mod, norm1_g, norm2_g, w_in, q_norm_g, k_norm_g, sink, w_br_a, w_br_b,
           w_o, w_router, b_router, w_e1, b_e1, w_e2, b_e2, final_g):
    b, s, d = x.shape
    n_ctx = ctx.shape[1]
    assert w_mod.shape[0] == 1
    bf = jnp.bfloat16
    cos, sin = _rope_tables(s, n_ctx)
    t = TOKEN_TILE
    moe_tile = min(2048, s)
    for l in range(1):
        rows = ((b + 1 + 7) // 8) * 8
        cond = jnp.zeros((rows, d), jnp.float32).at[:b].set(c).at[b].set(c_ctx)
        mods = _adaln(cond, w_mod[l], b_mod[l])
        sh1, sc1, g1, sh2, sc2, g2 = jnp.split(mods, 6, axis=-1)
        mod1 = jnp.stack([jnp.stack([sh1[:b], sc1[:b]], axis=1),
                          jnp.broadcast_to(jnp.stack([sh1[b], sc1[b]], axis=0)[None], (b, 2, d))], axis=1)
        mod2 = jnp.stack([sh2[:b], sc2[:b]], axis=1)

        w = w_in[l]
        kv = w[:, :4 * KV_W]
        wk_t = jnp.concatenate([kv[:, 0:KV_W], kv[:, 2 * KV_W:3 * KV_W]], axis=1).T.astype(bf)
        wv_t = jnp.concatenate([kv[:, KV_W:2 * KV_W], kv[:, 3 * KV_W:4 * KV_W]], axis=1).T.astype(bf)
        wq_t = w[:, 4 * KV_W:4 * KV_W + Q_A + Q_B].T.astype(bf)
        wg_t = w[:, 4 * KV_W + Q_A + Q_B:].T.astype(bf)
        qg = jnp.broadcast_to((q_norm_g[l] * ATTN_SCALE)[:, None], (HEAD_DIM, t))
        kg = jnp.broadcast_to(k_norm_g[l][:, None], (HEAD_DIM, t))

        qa_t, qb_t, ka, kb, va_t, vb_t, gate_t = _in_proj(
            x, ctx, mod1, norm1_g[l][None], wq_t, wk_t, wv_t, wg_t, qg, kg, cos, sin)

        ya_t = _attention(False, 128, sink[l], qa_t, ka, va_t)
        yb_t = _attention(True, 256, sink[l], qb_t, kb, vb_t)

        x1, h2, top_idx, top_w = _merge(
            x, ya_t, yb_t, gate_t, w_br_a[l].T.astype(bf), w_br_b[l].T.astype(bf), w_o[l].astype(bf),
            g1[:b, None, :], mod2, norm2_g[l][None], w_router[l].T, b_router[l][:, None])

        offs, tok_sorted, w_sorted = _route_lists(top_idx, top_w, moe_tile)
        w1 = w_e1[l]
        moe_out = _moe(offs, tok_sorted, w_sorted, h2.reshape(b * s, d),
                       w1[:, :, 0::2].astype(bf), w1[:, :, 1::2].astype(bf),
                       b_e1[l][:, None, 0::2], b_e1[l][:, None, 1::2],
                       w_e2[l].astype(bf), b_e2[l][:, None, :], moe_tile)
        out = _final(x1, moe_out.reshape(b, s, d), g2[:b, None, :], final_g[None])
    return out
```

```python
import functools

import numpy as np
import jax
import jax.numpy as jnp
from jax import lax
from jax.experimental import pallas as pl
from jax.experimental.pallas import tpu as pltpu

HEAD_DIM = 64
KV_HEADS = 2
A_HEADS = 8
B_HEADS = 8
GROUP = A_HEADS // KV_HEADS
GRID_W = 64
WINDOW = 128
ROPE_HALF = HEAD_DIM // 2
ROPE_QUARTER = ROPE_HALF // 2
ROPE_THETA = 10000.0
N_EXPERTS = 32
TOP_K = 4
SWIGLU_ALPHA = 1.702
SWIGLU_LIMIT = 7.0
RMS_EPS = 1e-6
ATTN_SCALE = HEAD_DIM ** -0.5
LOG2E = 1.4426950408889634
NEG_INF = -1e30
KV_W = KV_HEADS * HEAD_DIM
Q_A = A_HEADS * HEAD_DIM
Q_B = B_HEADS * HEAD_DIM
V_ROWS = HEAD_DIM + 16

SUBLANES = 8
LANES = 128
TOKEN_TILE = 256
MOE_TILE = 2048
MOE_CHUNK = 288
ROW_UNROLL = 8
SPLIT_BLOCK = 256
VMEM_LIMIT = 56 * 1024 * 1024

_NT = (((1,), (1,)), ((), ()))


def _dot_nt(a, b, precision=None):
    return lax.dot_general(a, b, _NT, preferred_element_type=jnp.float32, precision=precision)


def _dot(a, b, precision=None):
    return jnp.dot(a, b, preferred_element_type=jnp.float32, precision=precision)


def _params(sem):
    return pltpu.CompilerParams(dimension_semantics=sem, vmem_limit_bytes=VMEM_LIMIT)


def _adaln_kernel(cond_ref, w_ref, b_ref, o_ref):
    cond = cond_ref[...]
    act = cond * jax.nn.sigmoid(cond)
    o_ref[...] = _dot(act, w_ref[...], precision=lax.Precision.HIGHEST) + b_ref[...]


def _adaln(cond, w, b):
    rows, d = cond.shape
    n = w.shape[1]
    tn = 1024
    return pl.pallas_call(
        _adaln_kernel,
        grid=(n // tn,),
        in_specs=[pl.BlockSpec((rows, d), lambda j: (0, 0)),
                  pl.BlockSpec((d, tn), lambda j: (0, j)),
                  pl.BlockSpec((1, tn), lambda j: (0, j))],
        out_specs=pl.BlockSpec((rows, tn), lambda j: (0, j)),
        out_shape=jax.ShapeDtypeStruct((rows, n), jnp.float32),
        compiler_params=_params(("arbitrary",)),
        name="adaln",
    )(cond, w, b.reshape(1, n))


def _rope_t(xh, cos, sin):
    q = ROPE_QUARTER
    a0, a1, b0, b1 = xh[0:q], xh[q:2 * q], xh[2 * q:3 * q], xh[3 * q:4 * q]
    cr, cc = cos[0:q], cos[q:2 * q]
    sr, sc = sin[0:q], sin[q:2 * q]
    return jnp.concatenate([a0 * cr - a1 * sr, a1 * cr + a0 * sr,
                            b0 * cc - b1 * sc, b1 * cc + b0 * sc], axis=0)


def _head_norm_t(xh, g):
    ms = jnp.mean(xh * xh, axis=0, keepdims=True)
    return xh * lax.rsqrt(ms + RMS_EPS) * g


def _mod_norm(x, g, shift, scale):
    ms = jnp.mean(x * x, axis=-1, keepdims=True)
    return x * lax.rsqrt(ms + RMS_EPS) * g * (1.0 + scale) + shift


def _in_proj_kernel(n_lat, x_ref, ctx_ref, mod_ref, g_ref, wq_ref, wk_ref, wv_ref, wg_ref,
                    qg_ref, kg_ref, cos_ref, sin_ref,
                    qa_ref, qb_ref, ka_ref, kb_ref, va_ref, vb_ref, gate_ref, h_ref):
    i = pl.program_id(1)
    shift = mod_ref[0, 0, 0:1, :]
    scale = mod_ref[0, 0, 1:2, :]

    @pl.when(i < n_lat)
    def _():
        h_ref[...] = _mod_norm(x_ref[0], g_ref[...], shift, scale).astype(jnp.bfloat16)

    @pl.when(i >= n_lat)
    def _():
        h_ref[...] = _mod_norm(ctx_ref[0], g_ref[...], shift, scale).astype(jnp.bfloat16)

    h = h_ref[...]
    cos = cos_ref[...]
    sin = sin_ref[...]
    t = h.shape[0]

    kt = _dot_nt(wk_ref[...], h)
    kg = kg_ref[...]
    ka = [_rope_t(_head_norm_t(kt[j * HEAD_DIM:(j + 1) * HEAD_DIM], kg), cos, sin) for j in range(KV_HEADS)]
    kb = [_rope_t(kt[KV_W + j * HEAD_DIM:KV_W + (j + 1) * HEAD_DIM], cos, sin) for j in range(KV_HEADS)]
    ka_ref[0, 0] = jnp.concatenate(ka, axis=0).T.astype(jnp.bfloat16)
    kb_ref[0, 0] = jnp.concatenate(kb, axis=0).T.astype(jnp.bfloat16)
    vt = _dot_nt(wv_ref[...], h)
    ones = jnp.ones((V_ROWS - HEAD_DIM, t), jnp.float32)
    for br, v_ref in enumerate((va_ref, vb_ref)):
        rows = []
        for j in range(KV_HEADS):
            rows += [vt[br * KV_W + j * HEAD_DIM:br * KV_W + (j + 1) * HEAD_DIM], ones]
        v_ref[0, 0] = jnp.concatenate(rows, axis=0).astype(jnp.bfloat16)

    @pl.when(i < n_lat)
    def _():
        qg = qg_ref[...]
        for half in range(2):
            qt = _dot_nt(wq_ref[half * Q_A:(half + 1) * Q_A, :], h)
            for hd in range(A_HEADS):
                xh = qt[hd * HEAD_DIM:(hd + 1) * HEAD_DIM]
                if half == 0:
                    out = _rope_t(_head_norm_t(xh, qg), cos, sin)
                    qa_ref[0, hd * HEAD_DIM:(hd + 1) * HEAD_DIM, :] = out.astype(jnp.bfloat16)
                else:
                    out = _rope_t(xh * (ATTN_SCALE * LOG2E), cos, sin)
                    qb_ref[0, hd * HEAD_DIM:(hd + 1) * HEAD_DIM, :] = out.astype(jnp.bfloat16)
        rows = 512
        for c in range(wg_ref.shape[0] // rows):
            gt = _dot_nt(wg_ref[c * rows:(c + 1) * rows, :], h)
            gate_ref[0, c * rows:(c + 1) * rows, :] = jax.nn.sigmoid(gt).astype(jnp.bfloat16)


def _in_proj(x, ctx, mod, norm_g, wq_t, wk_t, wv_t, wg_t, qg, kg, cos, sin):
    b, s, d = x.shape
    n_ctx = ctx.shape[1]
    t = TOKEN_TILE
    n_lat = s // t
    n_tiles = n_lat + n_ctx // t
    last = n_lat - 1
    full = lambda shape: pl.BlockSpec(shape, lambda bi, i: (0,) * len(shape))
    q_spec = pl.BlockSpec((1, Q_A, t), lambda bi, i: (bi, 0, jnp.minimum(i, last)))
    k_spec = pl.BlockSpec((1, 1, t, KV_W), lambda bi, i: (bi, i, 0, 0))
    v_spec = pl.BlockSpec((1, 1, KV_HEADS * V_ROWS, t), lambda bi, i: (bi, i, 0, 0))
    bf = jnp.bfloat16
    return pl.pallas_call(
        functools.partial(_in_proj_kernel, n_lat),
        grid=(b, n_tiles),
        in_specs=[pl.BlockSpec((1, t, d), lambda bi, i: (bi, jnp.minimum(i, last), 0)),
                  pl.BlockSpec((1, t, d), lambda bi, i: (bi, jnp.maximum(i - n_lat, 0), 0)),
                  pl.BlockSpec((1, 1, 2, d), lambda bi, i: (bi, i // n_lat, 0, 0)),
                  full((1, d)), full(wq_t.shape), full(wk_t.shape), full(wv_t.shape), full(wg_t.shape),
                  full((HEAD_DIM, t)), full((HEAD_DIM, t)),
                  pl.BlockSpec((ROPE_HALF, t), lambda bi, i: (0, i)),
                  pl.BlockSpec((ROPE_HALF, t), lambda bi, i: (0, i))],
        out_specs=[q_spec, q_spec, k_spec, k_spec, v_spec, v_spec,
                   pl.BlockSpec((1, 2 * d, t), lambda bi, i: (bi, 0, jnp.minimum(i, last)))],
        out_shape=[jax.ShapeDtypeStruct((b, Q_A, s), bf), jax.ShapeDtypeStruct((b, Q_B, s), bf),
                   jax.ShapeDtypeStruct((b, n_tiles, t, KV_W), bf), jax.ShapeDtypeStruct((b, n_tiles, t, KV_W), bf),
                   jax.ShapeDtypeStruct((b, n_tiles, KV_HEADS * V_ROWS, t), bf),
                   jax.ShapeDtypeStruct((b, n_tiles, KV_HEADS * V_ROWS, t), bf),
                   jax.ShapeDtypeStruct((b, 2 * d, s), bf)],
        scratch_shapes=[pltpu.VMEM((t, d), bf)],
        compiler_params=_params(("arbitrary", "arbitrary")),
        name="in_proj",
    )(x, ctx, mod, norm_g, wq_t, wk_t, wv_t, wg_t, qg, kg, cos, sin)


def _attn_kernel(windowed, tq, n_lat, n_tiles, unroll, sink_ref, q_ref, k_ref, v_ref, y_ref, s_ref):
    qi = pl.program_id(1)
    t = TOKEN_TILE
    n = GROUP * tq
    heads = [[kvh * GROUP + g for g in range(GROUP)] for kvh in range(KV_HEADS)]
    qpad = []
    for kvh in range(KV_HEADS):
        qcat = jnp.concatenate([q_ref[0, hd * HEAD_DIM:(hd + 1) * HEAD_DIM, :] for hd in heads[kvh]], axis=1)
        zeros = jnp.zeros_like(qcat)
        qpad.append(jnp.concatenate([qcat, zeros] if kvh == 0 else [zeros, qcat], axis=0))

    def scores(slot, kb, m8, lo=0, rows=t, mask=None):
        kblk = k_ref[0, kb, lo:lo + rows, :]
        out = []
        for kvh in range(KV_HEADS):
            s = _dot(kblk, qpad[kvh])
            if mask is not None:
                s = jnp.where(mask, s, NEG_INF)
            s_ref[kvh, slot, 0:rows, :] = s
            out.append(jnp.maximum(m8[kvh], jnp.max(s.reshape(rows // 8, 8, n), axis=0)))
        return tuple(out)

    def weigh(slot, kb, m, accs, lo=0, rows=t):
        out = []
        for kvh in range(KV_HEADS):
            p = jnp.exp2(s_ref[kvh, slot, 0:rows, :] - m[kvh]).astype(jnp.bfloat16)
            vblk = v_ref[0, kb, kvh * V_ROWS:(kvh + 1) * V_ROWS, lo:lo + rows]
            out.append(accs[kvh] + _dot(vblk, p))
        return tuple(out)

    acc0 = jnp.zeros((V_ROWS, n), jnp.float32)
    if windowed:
        sink = [jnp.concatenate([jnp.full((1, tq), sink_ref[hd] * LOG2E, jnp.float32) for hd in heads[kvh]], axis=1)
                for kvh in range(KV_HEADS)]
        assert tq == t and WINDOW * 2 == t
        c_half = lax.rem(lax.broadcasted_iota(jnp.int32, (WINDOW, n), 1), tq)
        r_half = lax.broadcasted_iota(jnp.int32, (WINDOW, n), 0)
        c_full = lax.rem(lax.broadcasted_iota(jnp.int32, (t, n), 1), tq)
        r_full = lax.broadcasted_iota(jnp.int32, (t, n), 0)
        off_prev = jnp.where(qi >= 1, 0, 2 * t)
        off_next = jnp.where(qi + 1 < n_lat, 0, 2 * t)
        pieces = [
            (jnp.maximum(qi - 1, 0), WINDOW, WINDOW, r_half >= c_half + off_prev),
            (qi, 0, t, jnp.abs(r_full - c_full) <= WINDOW),
            (jnp.minimum(qi + 1, n_lat - 1), 0, WINDOW, c_half - r_half >= WINDOW + off_next),
        ] + [(kb, 0, t, None) for kb in range(n_lat, n_tiles)]
        m8 = tuple(jnp.broadcast_to(sink[kvh], (8, n)) for kvh in range(KV_HEADS))
        for slot, (kb, lo, rows, ok) in enumerate(pieces):
            m8 = scores(slot, kb, m8, lo, rows, ok)
        m = [jnp.max(m8[kvh], axis=0, keepdims=True) for kvh in range(KV_HEADS)]
        row = lax.broadcasted_iota(jnp.int32, (V_ROWS, n), 0)
        accs = tuple(jnp.where(row == HEAD_DIM, jnp.exp2(sink[kvh] - m[kvh]), 0.0) for kvh in range(KV_HEADS))
        for slot, (kb, lo, rows, _) in enumerate(pieces):
            accs = weigh(slot, kb, m, accs, lo, rows)
    else:
        m8 = tuple(jnp.full((8, n), -jnp.inf, jnp.float32) for _ in range(KV_HEADS))
        m8 = lax.fori_loop(0, n_tiles, lambda kb, c: scores(kb, kb, c), m8, unroll=unroll)
        m = [jnp.max(m8[kvh], axis=0, keepdims=True) for kvh in range(KV_HEADS)]
        accs = lax.fori_loop(0, n_tiles, lambda kb, c: weigh(kb, kb, m, c), (acc0, acc0), unroll=unroll)
    for kvh in range(KV_HEADS):
        acc = accs[kvh]
        out = acc[0:HEAD_DIM] / acc[HEAD_DIM:HEAD_DIM + 1]
        for g, hd in enumerate(heads[kvh]):
            y_ref[0, hd * HEAD_DIM:(hd + 1) * HEAD_DIM, :] = out[:, g * tq:(g + 1) * tq].astype(jnp.bfloat16)


def _attention(windowed, tq, unroll, sink, q_t, k, v_t):
    b, hq, s = q_t.shape
    n_tiles = k.shape[1]
    n_lat = s // TOKEN_TILE
    slots = tq // TOKEN_TILE + 2 + (n_tiles - n_lat) if windowed else n_tiles
    grid_spec = pltpu.PrefetchScalarGridSpec(
        num_scalar_prefetch=1,
        grid=(b, s // tq),
        in_specs=[pl.BlockSpec((1, hq, tq), lambda bi, qi, sk: (bi, 0, qi)),
                  pl.BlockSpec((1,) + k.shape[1:], lambda bi, qi, sk: (bi, 0, 0, 0)),
                  pl.BlockSpec((1,) + v_t.shape[1:], lambda bi, qi, sk: (bi, 0, 0, 0))],
        out_specs=pl.BlockSpec((1, hq, tq), lambda bi, qi, sk: (bi, 0, qi)),
        scratch_shapes=[pltpu.VMEM((KV_HEADS, slots, TOKEN_TILE, GROUP * tq), jnp.float32)],
    )
    return pl.pallas_call(
        functools.partial(_attn_kernel, windowed, tq, n_lat, n_tiles, unroll),
        grid_spec=grid_spec,
        out_shape=jax.ShapeDtypeStruct((b, hq, s), jnp.bfloat16),
        compiler_params=_params(("arbitrary", "arbitrary")),
        name="attn_window" if windowed else "attn_global",
    )(sink, q_t, k, v_t)


def _merge_kernel(x_ref, ya_ref, yb_ref, gate_ref, wa_ref, wb_ref, wo_ref, g1_ref, mod_ref, g2_ref,
                  wr_ref, br_ref, x1_ref, h2_ref, idx_ref, wt_ref):
    d = x_ref.shape[-1]
    za = _dot(wa_ref[...], ya_ref[0])
    zb = _dot(wb_ref[...], yb_ref[0])
    mt = gate_ref[0, 0:d, :].astype(jnp.float32) * za + gate_ref[0, d:2 * d, :].astype(jnp.float32) * zb
    o = _dot(mt.T.astype(jnp.bfloat16), wo_ref[...])
    x1 = x_ref[0] + g1_ref[0] * o
    x1_ref[0] = x1
    h2 = _mod_norm(x1, g2_ref[...], mod_ref[0, 0:1, :], mod_ref[0, 1:2, :])
    h2_ref[0] = h2
    logits = _dot_nt(wr_ref[...], h2, precision=lax.Precision.HIGHEST) + br_ref[...]
    n_e = logits.shape[0]
    row = lax.broadcasted_iota(jnp.int32, logits.shape, 0)
    work = logits
    top_v, top_i = [], []
    for _ in range(TOP_K):
        mk = jnp.max(work, axis=0, keepdims=True)
        ik = jnp.min(jnp.where(work == mk, row, n_e), axis=0, keepdims=True)
        top_v.append(mk)
        top_i.append(ik)
        work = jnp.where(row == ik, -jnp.inf, work)
    ex = [jnp.exp(v - top_v[0]) for v in top_v]
    den = ex[0] + ex[1] + ex[2] + ex[3]
    idx_ref[0] = jnp.concatenate(top_i, axis=0)
    wt_ref[0] = jnp.concatenate([e / den for e in ex], axis=0)


def _merge(x, ya_t, yb_t, gate_t, wa_t, wb_t, wo, g1, mod2, norm2_g, wr_t, br):
    b, s, d = x.shape
    t = TOKEN_TILE
    full = lambda shape: pl.BlockSpec(shape, lambda bi, i: (0,) * len(shape))
    tok = pl.BlockSpec((1, t, d), lambda bi, i: (bi, i, 0))
    col = lambda rows: pl.BlockSpec((1, rows, t), lambda bi, i: (bi, 0, i))
    return pl.pallas_call(
        _merge_kernel,
        grid=(b, s // t),
        in_specs=[tok, col(Q_A), col(Q_B), col(2 * d),
                  full(wa_t.shape), full(wb_t.shape), full(wo.shape),
                  pl.BlockSpec((1, 1, d), lambda bi, i: (bi, 0, 0)),
                  pl.BlockSpec((1, 2, d), lambda bi, i: (bi, 0, 0)),
                  full((1, d)), full(wr_t.shape), full(br.shape)],
        out_specs=[tok, tok, col(TOP_K), col(TOP_K)],
        out_shape=[jax.ShapeDtypeStruct((b, s, d), jnp.float32), jax.ShapeDtypeStruct((b, s, d), jnp.float32),
                   jax.ShapeDtypeStruct((b, TOP_K, s), jnp.int32), jax.ShapeDtypeStruct((b, TOP_K, s), jnp.float32)],
        compiler_params=_params(("arbitrary", "arbitrary")),
        name="merge_router",
    )(x, ya_t, yb_t, gate_t, wa_t, wb_t, wo, g1, mod2, norm2_g, wr_t, br)


def _w1_split_kernel(w_ref, p_ref, g_ref, l_ref):
    half = SPLIT_BLOCK // 2
    for jb in range(w_ref.shape[-1] // SPLIT_BLOCK):
        blk = w_ref[0, :, jb * SPLIT_BLOCK:(jb + 1) * SPLIT_BLOCK].astype(jnp.bfloat16)
        r = _dot(blk, p_ref[...])
        g_ref[0, :, jb * half:(jb + 1) * half] = r[:, :half].astype(jnp.bfloat16)
        l_ref[0, :, jb * half:(jb + 1) * half] = r[:, half:].astype(jnp.bfloat16)


def _w1_split(w1):
    n_e, d, d2 = w1.shape
    rows = 512
    half = SPLIT_BLOCK // 2
    perm = np.zeros((SPLIT_BLOCK, SPLIT_BLOCK), np.float32)
    for j in range(half):
        perm[2 * j, j] = 1.0
        perm[2 * j + 1, half + j] = 1.0
    out = pl.BlockSpec((1, rows, d2 // 2), lambda e, r: (e, r, 0))
    return pl.pallas_call(
        _w1_split_kernel,
        grid=(n_e, d // rows),
        in_specs=[pl.BlockSpec((1, rows, d2), lambda e, r: (e, r, 0)),
                  pl.BlockSpec((SPLIT_BLOCK, SPLIT_BLOCK), lambda e, r: (0, 0))],
        out_specs=[out, out],
        out_shape=[jax.ShapeDtypeStruct((n_e, d, d2 // 2), jnp.bfloat16)] * 2,
        compiler_params=_params(("arbitrary", "arbitrary")),
        name="w1_split",
    )(w1, jnp.asarray(perm, jnp.bfloat16))


def _moe_kernel(offs_ref, tok_ref, wt_ref, h_ref, w1g_ref, w1l_ref, b1g_ref, b1l_ref, w2_ref, b2_ref,
                acc_ref, xg_ref, y_ref):
    ti = pl.program_id(0)
    e = pl.program_id(1)
    n_e = pl.num_programs(1)
    ch = xg_ref.shape[0] // SUBLANES
    lanes = xg_ref.shape[1]

    def tile_rows(r):
        return pl.ds(pl.multiple_of(r * SUBLANES, SUBLANES), SUBLANES)

    @pl.when((ti == 0) & (e == 0))
    def _():
        xg_ref[...] = jnp.zeros_like(xg_ref)

    @pl.when(e == 0)
    def _():
        acc_ref[...] = jnp.zeros_like(acc_ref)

    start = offs_ref[ti * (n_e + 1) + e]
    end = offs_ref[ti * (n_e + 1) + e + 1]

    def chunk(c, _):
        base = start + c * ch
        n = jnp.minimum(ch, end - base)
        groups = (n + ROW_UNROLL - 1) // ROW_UNROLL

        def gather(i, _):
            r0 = pl.multiple_of(i * ROW_UNROLL, ROW_UNROLL)
            for u in range(ROW_UNROLL):
                tok = tok_ref[0, 0, base + r0 + u]
                xg_ref[tile_rows(r0 + u), :] = h_ref[tile_rows(tok), :]
            return 0

        lax.fori_loop(0, groups, gather, 0)
        xg = jnp.concatenate([xg_ref[pl.ds(j, ch, stride=SUBLANES), :] for j in range(SUBLANES)],
                             axis=1).astype(jnp.bfloat16)
        glu = jnp.minimum(_dot(xg, w1g_ref[0]) + b1g_ref[0], SWIGLU_LIMIT)
        lin = jnp.clip(_dot(xg, w1l_ref[0]) + b1l_ref[0], -SWIGLU_LIMIT, SWIGLU_LIMIT)
        act = glu * jax.nn.sigmoid(SWIGLU_ALPHA * glu) * (lin + 1.0)
        y = _dot(act.astype(jnp.bfloat16), w2_ref[0]) + b2_ref[0]
        for j in range(SUBLANES):
            y_ref[pl.ds(j, ch, stride=SUBLANES), :] = y[:, j * lanes:(j + 1) * lanes]

        def scatter(i, _):
            r0 = pl.multiple_of(i * ROW_UNROLL, ROW_UNROLL)
            toks = [tok_ref[0, 0, base + r0 + u] for u in range(ROW_UNROLL)]
            new = [acc_ref[tile_rows(toks[u]), :] + wt_ref[0, 0, base + r0 + u] * y_ref[tile_rows(r0 + u), :]
                   for u in range(ROW_UNROLL)]
            for u in range(ROW_UNROLL):
                acc_ref[tile_rows(toks[u]), :] = new[u]
            return 0

        full_groups = n // ROW_UNROLL
        lax.fori_loop(0, full_groups, scatter, 0)

        def scatter_row(r, _):
            tok = tok_ref[0, 0, base + r]
            acc_ref[tile_rows(tok), :] = acc_ref[tile_rows(tok), :] + wt_ref[0, 0, base + r] * y_ref[tile_rows(r), :]
            return 0

        lax.fori_loop(full_groups * ROW_UNROLL, n, scatter_row, 0)
        return 0

    lax.fori_loop(0, (end - start + ch - 1) // ch, chunk, 0)


def _moe(offs, tok_sorted, wt_sorted, h2, w1g, w1l, b1g, b1l, w2, b2, tile):
    n_tok, d = h2.shape
    n_e, _, d_e = w1g.shape
    n_t = n_tok // tile
    a = tok_sorted.shape[-1]
    assert d == SUBLANES * LANES
    smem = lambda: pl.BlockSpec((1, 1, a), lambda ti, e, o: (ti, 0, 0), memory_space=pltpu.SMEM)
    ex = lambda shape: pl.BlockSpec((1,) + shape, lambda ti, e, o: (e, 0, 0))
    tiles = pl.BlockSpec((tile * SUBLANES, LANES), lambda ti, e, o: (ti, 0))
    grid_spec = pltpu.PrefetchScalarGridSpec(
        num_scalar_prefetch=1,
        grid=(n_t, n_e),
        in_specs=[smem(), smem(), tiles,
                  ex((d, d_e)), ex((d, d_e)), ex((1, d_e)), ex((1, d_e)), ex((d_e, d)), ex((1, d))],
        out_specs=tiles,
        scratch_shapes=[pltpu.VMEM((MOE_CHUNK * SUBLANES, LANES), jnp.float32),
                        pltpu.VMEM((MOE_CHUNK * SUBLANES, LANES), jnp.float32)],
    )
    out = pl.pallas_call(
        _moe_kernel,
        grid_spec=grid_spec,
        out_shape=jax.ShapeDtypeStruct((n_tok * SUBLANES, LANES), jnp.float32),
        compiler_params=_params(("arbitrary", "arbitrary")),
        name="moe",
    )(offs, tok_sorted, wt_sorted, h2.reshape(n_tok * SUBLANES, LANES), w1g, w1l, b1g, b1l, w2, b2)
    return out.reshape(n_tok, d)


def _final_kernel(x_ref, m_ref, g2_ref, g_ref, o_ref):
    x = x_ref[0] + g2_ref[0] * m_ref[0]
    ms = jnp.mean(x * x, axis=-1, keepdims=True)
    o_ref[0] = x * lax.rsqrt(ms + RMS_EPS) * g_ref[...]


def _final(x1, moe_out, g2, final_g):
    b, s, d = x1.shape
    t = 512
    tok = pl.BlockSpec((1, t, d), lambda bi, i: (bi, i, 0))
    return pl.pallas_call(
        _final_kernel,
        grid=(b, s // t),
        in_specs=[tok, tok, pl.BlockSpec((1, 1, d), lambda bi, i: (bi, 0, 0)),
                  pl.BlockSpec((1, d), lambda bi, i: (0, 0))],
        out_specs=tok,
        out_shape=jax.ShapeDtypeStruct((b, s, d), jnp.float32),
        compiler_params=_params(("arbitrary", "arbitrary")),
        name="final_norm",
    )(x1, moe_out, g2, final_g)


def _rope_tables(s, n_ctx):
    inv = ROPE_THETA ** (-(jnp.arange(ROPE_QUARTER, dtype=jnp.float32) * 2.0 / ROPE_HALF))
    pos = jnp.arange(s)
    ang_r = (pos // GRID_W).astype(jnp.float32)[None, :] * inv[:, None]
    ang_c = (pos % GRID_W).astype(jnp.float32)[None, :] * inv[:, None]
    ang = jnp.concatenate([ang_r, ang_c], axis=0)
    cos = jnp.concatenate([jnp.cos(ang), jnp.ones((ROPE_HALF, n_ctx), jnp.float32)], axis=1)
    sin = jnp.concatenate([jnp.sin(ang), jnp.zeros((ROPE_HALF, n_ctx), jnp.float32)], axis=1)
    return cos, sin


def _route_lists(top_idx, top_w, tile):
    b, k, s = top_idx.shape
    n_t = b * s // tile
    per = s // tile
    e_flat = top_idx.reshape(b, k, per, tile).transpose(0, 2, 1, 3).reshape(n_t, k * tile)
    w_flat = top_w.reshape(b, k, per, tile).transpose(0, 2, 1, 3).reshape(n_t, k * tile)
    tok = jnp.tile(jnp.arange(tile, dtype=jnp.int32), k)[None, :].repeat(n_t, axis=0)
    e_sorted, tok_sorted, w_sorted = lax.sort((e_flat, tok, w_flat), dimension=1, num_keys=2)
    counts = jnp.sum(e_flat[:, :, None] == jnp.arange(N_EXPERTS, dtype=jnp.int32)[None, None, :], axis=1,
                     dtype=jnp.int32)
    offs = jnp.concatenate([jnp.zeros((n_t, 1), jnp.int32), jnp.cumsum(counts, axis=1, dtype=jnp.int32)], axis=1)
    pad = ((0, 0), (0, ROW_UNROLL))
    return offs.reshape(-1), jnp.pad(tok_sorted, pad)[:, None, :], jnp.pad(w_sorted, pad)[:, None, :]


def kernel(x, c, ctx, c_ctx, w_mod, b_mod, norm1_g, norm2_g, w_in, q_norm_g, k_norm_g, sink, w_br_a, w_br_b,
           w_o, w_router, b_router, w_e1, b_e1, w_e2, b_e2, final_g):
    b, s, d = x.shape
    n_ctx = ctx.shape[1]
    assert w_mod.shape[0] == 1
    bf = jnp.bfloat16
    cos, sin = _rope_tables(s, n_ctx)
    t = TOKEN_TILE
    moe_tile = min(MOE_TILE, s)
    for l in range(1):
        rows = ((b + 1 + 7) // 8) * 8
        cond = jnp.zeros((rows, d), jnp.float32).at[:b].set(c).at[b].set(c_ctx)
        mods = _adaln(cond, w_mod[l], b_mod[l])
        sh1, sc1, g1, sh2, sc2, g2 = jnp.split(mods, 6, axis=-1)
        mod1 = jnp.stack([jnp.stack([sh1[:b], sc1[:b]], axis=1),
                          jnp.broadcast_to(jnp.stack([sh1[b], sc1[b]], axis=0)[None], (b, 2, d))], axis=1)
        mod2 = jnp.stack([sh2[:b], sc2[:b]], axis=1)

        w = w_in[l]
        kv = w[:, :4 * KV_W]
        wk_t = jnp.concatenate([kv[:, 0:KV_W], kv[:, 2 * KV_W:3 * KV_W]], axis=1).T.astype(bf)
        wv_t = jnp.concatenate([kv[:, KV_W:2 * KV_W], kv[:, 3 * KV_W:4 * KV_W]], axis=1).T.astype(bf)
        wq_t = w[:, 4 * KV_W:4 * KV_W + Q_A + Q_B].T.astype(bf)
        wg_t = w[:, 4 * KV_W + Q_A + Q_B:].T.astype(bf)
        qg = jnp.broadcast_to((q_norm_g[l] * (ATTN_SCALE * LOG2E))[:, None], (HEAD_DIM, t))
        kg = jnp.broadcast_to(k_norm_g[l][:, None], (HEAD_DIM, t))

        qa_t, qb_t, ka, kb, va_t, vb_t, gate_t = _in_proj(
            x, ctx, mod1, norm1_g[l][None], wq_t, wk_t, wv_t, wg_t, qg, kg, cos, sin)

        ya_t = _attention(False, 128, 2, sink[l], qa_t, ka, va_t)
        yb_t = _attention(True, 256, 1, sink[l], qb_t, kb, vb_t)

        x1, h2, top_idx, top_w = _merge(
            x, ya_t, yb_t, gate_t, w_br_a[l].T.astype(bf), w_br_b[l].T.astype(bf), w_o[l].astype(bf),
            g1[:b, None, :], mod2, norm2_g[l][None], w_router[l].T, b_router[l][:, None])

        offs, tok_sorted, w_sorted = _route_lists(top_idx, top_w, moe_tile)
        w1g, w1l = _w1_split(w_e1[l])
        moe_out = _moe(offs, tok_sorted, w_sorted, h2.reshape(b * s, d), w1g, w1l,
                       b_e1[l][:, None, 0::2], b_e1[l][:, None, 1::2],
                       w_e2[l].astype(bf), b_e2[l][:, None, :], moe_tile)
        out = _final(x1, moe_out.reshape(b, s, d), g2[:b, None, :], final_g[None])
    return out
```

```python
import functools

import numpy as np
import jax
import jax.numpy as jnp
from jax import lax
from jax.experimental import pallas as pl
from jax.experimental.pallas import tpu as pltpu

HEAD_DIM = 64
KV_HEADS = 2
A_HEADS = 8
B_HEADS = 8
GROUP = A_HEADS // KV_HEADS
GRID_W = 64
WINDOW = 128
ROPE_HALF = HEAD_DIM // 2
ROPE_QUARTER = ROPE_HALF // 2
ROPE_THETA = 10000.0
N_EXPERTS = 32
TOP_K = 4
SWIGLU_ALPHA = 1.702
SWIGLU_LIMIT = 7.0
RMS_EPS = 1e-6
ATTN_SCALE = HEAD_DIM ** -0.5
LOG2E = 1.4426950408889634
NEG_INF = -1e30
KV_W = KV_HEADS * HEAD_DIM
Q_A = A_HEADS * HEAD_DIM
Q_B = B_HEADS * HEAD_DIM
V_ROWS = HEAD_DIM + 16

SUBLANES = 8
LANES = 128
TOKEN_TILE = 256
MERGE_TILE = 512
MOE_TILE = 2048
MOE_CHUNK = 288
ROW_UNROLL = 8
SPLIT_BLOCK = 256
VMEM_LIMIT = 56 * 1024 * 1024

_NT = (((1,), (1,)), ((), ()))


def _dot_nt(a, b, precision=None):
    return lax.dot_general(a, b, _NT, preferred_element_type=jnp.float32, precision=precision)


def _dot(a, b, precision=None):
    return jnp.dot(a, b, preferred_element_type=jnp.float32, precision=precision)


def _params(sem):
    return pltpu.CompilerParams(dimension_semantics=sem, vmem_limit_bytes=VMEM_LIMIT)


def _adaln_kernel(cond_ref, w_ref, b_ref, o_ref):
    cond = cond_ref[...]
    act = cond * jax.nn.sigmoid(cond)
    o_ref[...] = _dot(act, w_ref[...], precision=lax.Precision.HIGHEST) + b_ref[...]


def _adaln(cond, w, b):
    rows, d = cond.shape
    n = w.shape[1]
    tn = 1024
    return pl.pallas_call(
        _adaln_kernel,
        grid=(n // tn,),
        in_specs=[pl.BlockSpec((rows, d), lambda j: (0, 0)),
                  pl.BlockSpec((d, tn), lambda j: (0, j)),
                  pl.BlockSpec((1, tn), lambda j: (0, j))],
        out_specs=pl.BlockSpec((rows, tn), lambda j: (0, j)),
        out_shape=jax.ShapeDtypeStruct((rows, n), jnp.float32),
        compiler_params=_params(("arbitrary",)),
        name="adaln",
    )(cond, w, b.reshape(1, n))


def _rope_t(xh, cos, sin):
    q = ROPE_QUARTER
    a0, a1, b0, b1 = xh[0:q], xh[q:2 * q], xh[2 * q:3 * q], xh[3 * q:4 * q]
    cr, cc = cos[0:q], cos[q:2 * q]
    sr, sc = sin[0:q], sin[q:2 * q]
    return jnp.concatenate([a0 * cr - a1 * sr, a1 * cr + a0 * sr,
                            b0 * cc - b1 * sc, b1 * cc + b0 * sc], axis=0)


def _head_norm_t(xh, g):
    ms = jnp.mean(xh * xh, axis=0, keepdims=True)
    return xh * lax.rsqrt(ms + RMS_EPS) * g


def _mod_norm(x, g, shift, scale):
    ms = jnp.mean(x * x, axis=-1, keepdims=True)
    return x * lax.rsqrt(ms + RMS_EPS) * g * (1.0 + scale) + shift


def _in_proj_kernel(n_lat, x_ref, ctx_ref, mod_ref, g_ref, wq_ref, wk_ref, wv_ref, wg_ref,
                    qg_ref, kg_ref, cos_ref, sin_ref,
                    qa_ref, qb_ref, ka_ref, kb_ref, va_ref, vb_ref, gate_ref, h_ref):
    i = pl.program_id(1)
    shift = mod_ref[0, 0, 0:1, :]
    scale = mod_ref[0, 0, 1:2, :]

    @pl.when(i < n_lat)
    def _():
        h_ref[...] = _mod_norm(x_ref[0], g_ref[...], shift, scale).astype(jnp.bfloat16)

    @pl.when(i >= n_lat)
    def _():
        h_ref[...] = _mod_norm(ctx_ref[0], g_ref[...], shift, scale).astype(jnp.bfloat16)

    h = h_ref[...]
    cos = cos_ref[...]
    sin = sin_ref[...]
    t = h.shape[0]

    kt = _dot_nt(wk_ref[...], h)
    kg = kg_ref[...]
    ka = [_rope_t(_head_norm_t(kt[j * HEAD_DIM:(j + 1) * HEAD_DIM], kg), cos, sin) for j in range(KV_HEADS)]
    kb = [_rope_t(kt[KV_W + j * HEAD_DIM:KV_W + (j + 1) * HEAD_DIM], cos, sin) for j in range(KV_HEADS)]
    ka_ref[0, 0] = jnp.concatenate(ka, axis=0).T.astype(jnp.bfloat16)
    kb_ref[0, 0] = jnp.concatenate(kb, axis=0).T.astype(jnp.bfloat16)
    vt = _dot_nt(wv_ref[...], h)
    ones = jnp.ones((V_ROWS - HEAD_DIM, t), jnp.float32)
    for br, v_ref in enumerate((va_ref, vb_ref)):
        rows = []
        for j in range(KV_HEADS):
            rows += [vt[br * KV_W + j * HEAD_DIM:br * KV_W + (j + 1) * HEAD_DIM], ones]
        v_ref[0, 0] = jnp.concatenate(rows, axis=0).astype(jnp.bfloat16)

    @pl.when(i < n_lat)
    def _():
        qg = qg_ref[...]
        for half in range(2):
            qt = _dot_nt(wq_ref[half * Q_A:(half + 1) * Q_A, :], h)
            for hd in range(A_HEADS):
                xh = qt[hd * HEAD_DIM:(hd + 1) * HEAD_DIM]
                if half == 0:
                    out = _rope_t(_head_norm_t(xh, qg), cos, sin)
                    qa_ref[0, hd * HEAD_DIM:(hd + 1) * HEAD_DIM, :] = out.astype(jnp.bfloat16)
                else:
                    out = _rope_t(xh * (ATTN_SCALE * LOG2E), cos, sin)
                    qb_ref[0, hd * HEAD_DIM:(hd + 1) * HEAD_DIM, :] = out.astype(jnp.bfloat16)
        rows = 512
        for c in range(wg_ref.shape[0] // rows):
            gt = _dot_nt(wg_ref[c * rows:(c + 1) * rows, :], h)
            gate_ref[0, c * rows:(c + 1) * rows, :] = jax.nn.sigmoid(gt).astype(jnp.bfloat16)


def _in_proj(x, ctx, mod, norm_g, wq_t, wk_t, wv_t, wg_t, qg, kg, cos, sin):
    b, s, d = x.shape
    n_ctx = ctx.shape[1]
    t = TOKEN_TILE
    n_lat = s // t
    n_tiles = n_lat + n_ctx // t
    last = n_lat - 1
    full = lambda shape: pl.BlockSpec(shape, lambda bi, i: (0,) * len(shape))
    q_spec = pl.BlockSpec((1, Q_A, t), lambda bi, i: (bi, 0, jnp.minimum(i, last)))
    k_spec = pl.BlockSpec((1, 1, t, KV_W), lambda bi, i: (bi, i, 0, 0))
    v_spec = pl.BlockSpec((1, 1, KV_HEADS * V_ROWS, t), lambda bi, i: (bi, i, 0, 0))
    bf = jnp.bfloat16
    return pl.pallas_call(
        functools.partial(_in_proj_kernel, n_lat),
        grid=(b, n_tiles),
        in_specs=[pl.BlockSpec((1, t, d), lambda bi, i: (bi, jnp.minimum(i, last), 0)),
                  pl.BlockSpec((1, t, d), lambda bi, i: (bi, jnp.maximum(i - n_lat, 0), 0)),
                  pl.BlockSpec((1, 1, 2, d), lambda bi, i: (bi, i // n_lat, 0, 0)),
                  full((1, d)), full(wq_t.shape), full(wk_t.shape), full(wv_t.shape), full(wg_t.shape),
                  full((HEAD_DIM, t)), full((HEAD_DIM, t)),
                  pl.BlockSpec((ROPE_HALF, t), lambda bi, i: (0, i)),
                  pl.BlockSpec((ROPE_HALF, t), lambda bi, i: (0, i))],
        out_specs=[q_spec, q_spec, k_spec, k_spec, v_spec, v_spec,
                   pl.BlockSpec((1, 2 * d, t), lambda bi, i: (bi, 0, jnp.minimum(i, last)))],
        out_shape=[jax.ShapeDtypeStruct((b, Q_A, s), bf), jax.ShapeDtypeStruct((b, Q_B, s), bf),
                   jax.ShapeDtypeStruct((b, n_tiles, t, KV_W), bf), jax.ShapeDtypeStruct((b, n_tiles, t, KV_W), bf),
                   jax.ShapeDtypeStruct((b, n_tiles, KV_HEADS * V_ROWS, t), bf),
                   jax.ShapeDtypeStruct((b, n_tiles, KV_HEADS * V_ROWS, t), bf),
                   jax.ShapeDtypeStruct((b, 2 * d, s), bf)],
        scratch_shapes=[pltpu.VMEM((t, d), bf)],
        compiler_params=_params(("arbitrary", "arbitrary")),
        name="in_proj",
    )(x, ctx, mod, norm_g, wq_t, wk_t, wv_t, wg_t, qg, kg, cos, sin)


def _attn_kernel(windowed, tq, n_lat, n_tiles, unroll, sink_ref, q_ref, k_ref, v_ref, y_ref, s_ref):
    qi = pl.program_id(1)
    t = TOKEN_TILE
    n = GROUP * tq
    heads = [[kvh * GROUP + g for g in range(GROUP)] for kvh in range(KV_HEADS)]
    qpad = []
    for kvh in range(KV_HEADS):
        qcat = jnp.concatenate([q_ref[0, hd * HEAD_DIM:(hd + 1) * HEAD_DIM, :] for hd in heads[kvh]], axis=1)
        zeros = jnp.zeros_like(qcat)
        qpad.append(jnp.concatenate([qcat, zeros] if kvh == 0 else [zeros, qcat], axis=0))

    def scores(slot, kb, m8, lo=0, rows=t, mask=None):
        kblk = k_ref[0, kb, lo:lo + rows, :]
        out = []
        for kvh in range(KV_HEADS):
            s = _dot(kblk, qpad[kvh])
            if mask is not None:
                s = jnp.where(mask, s, NEG_INF)
            s_ref[kvh, slot, 0:rows, :] = s
            out.append(jnp.maximum(m8[kvh], jnp.max(s.reshape(rows // 8, 8, n), axis=0)))
        return tuple(out)

    def weigh(slot, kb, m, accs, lo=0, rows=t):
        out = []
        for kvh in range(KV_HEADS):
            p = jnp.exp2(s_ref[kvh, slot, 0:rows, :] - m[kvh]).astype(jnp.bfloat16)
            vblk = v_ref[0, kb, kvh * V_ROWS:(kvh + 1) * V_ROWS, lo:lo + rows]
            out.append(accs[kvh] + _dot(vblk, p))
        return tuple(out)

    acc0 = jnp.zeros((V_ROWS, n), jnp.float32)
    if windowed:
        sink = [jnp.concatenate([jnp.full((1, tq), sink_ref[hd] * LOG2E, jnp.float32) for hd in heads[kvh]], axis=1)
                for kvh in range(KV_HEADS)]
        assert tq == t and WINDOW * 2 == t
        c_half = lax.rem(lax.broadcasted_iota(jnp.int32, (WINDOW, n), 1), tq)
        r_half = lax.broadcasted_iota(jnp.int32, (WINDOW, n), 0)
        c_full = lax.rem(lax.broadcasted_iota(jnp.int32, (t, n), 1), tq)
        r_full = lax.broadcasted_iota(jnp.int32, (t, n), 0)
        off_prev = jnp.where(qi >= 1, 0, 2 * t)
        off_next = jnp.where(qi + 1 < n_lat, 0, 2 * t)
        pieces = [
            (jnp.maximum(qi - 1, 0), WINDOW, WINDOW, r_half >= c_half + off_prev),
            (qi, 0, t, jnp.abs(r_full - c_full) <= WINDOW),
            (jnp.minimum(qi + 1, n_lat - 1), 0, WINDOW, c_half - r_half >= WINDOW + off_next),
        ] + [(kb, 0, t, None) for kb in range(n_lat, n_tiles)]
        m8 = tuple(jnp.broadcast_to(sink[kvh], (8, n)) for kvh in range(KV_HEADS))
        for slot, (kb, lo, rows, ok) in enumerate(pieces):
            m8 = scores(slot, kb, m8, lo, rows, ok)
        m = [jnp.max(m8[kvh], axis=0, keepdims=True) for kvh in range(KV_HEADS)]
        row = lax.broadcasted_iota(jnp.int32, (V_ROWS, n), 0)
        accs = tuple(jnp.where(row == HEAD_DIM, jnp.exp2(sink[kvh] - m[kvh]), 0.0) for kvh in range(KV_HEADS))
        for slot, (kb, lo, rows, _) in enumerate(pieces):
            accs = weigh(slot, kb, m, accs, lo, rows)
    else:
        m8 = tuple(jnp.full((8, n), -jnp.inf, jnp.float32) for _ in range(KV_HEADS))
        m8 = lax.fori_loop(0, n_tiles, lambda kb, c: scores(kb, kb, c), m8, unroll=unroll)
        m = [jnp.max(m8[kvh], axis=0, keepdims=True) for kvh in range(KV_HEADS)]
        accs = lax.fori_loop(0, n_tiles, lambda kb, c: weigh(kb, kb, m, c), (acc0, acc0), unroll=unroll)
    for kvh in range(KV_HEADS):
        acc = accs[kvh]
        out = acc[0:HEAD_DIM] / acc[HEAD_DIM:HEAD_DIM + 1]
        for g, hd in enumerate(heads[kvh]):
            y_ref[0, hd * HEAD_DIM:(hd + 1) * HEAD_DIM, :] = out[:, g * tq:(g + 1) * tq].astype(jnp.bfloat16)


def _attention(windowed, tq, unroll, sink, q_t, k, v_t):
    b, hq, s = q_t.shape
    n_tiles = k.shape[1]
    n_lat = s // TOKEN_TILE
    slots = tq // TOKEN_TILE + 2 + (n_tiles - n_lat) if windowed else n_tiles
    grid_spec = pltpu.PrefetchScalarGridSpec(
        num_scalar_prefetch=1,
        grid=(b, s // tq),
        in_specs=[pl.BlockSpec((1, hq, tq), lambda bi, qi, sk: (bi, 0, qi)),
                  pl.BlockSpec((1,) + k.shape[1:], lambda bi, qi, sk: (bi, 0, 0, 0)),
                  pl.BlockSpec((1,) + v_t.shape[1:], lambda bi, qi, sk: (bi, 0, 0, 0))],
        out_specs=pl.BlockSpec((1, hq, tq), lambda bi, qi, sk: (bi, 0, qi)),
        scratch_shapes=[pltpu.VMEM((KV_HEADS, slots, TOKEN_TILE, GROUP * tq), jnp.float32)],
    )
    return pl.pallas_call(
        functools.partial(_attn_kernel, windowed, tq, n_lat, n_tiles, unroll),
        grid_spec=grid_spec,
        out_shape=jax.ShapeDtypeStruct((b, hq, s), jnp.bfloat16),
        compiler_params=_params(("arbitrary", "arbitrary")),
        name="attn_window" if windowed else "attn_global",
    )(sink, q_t, k, v_t)


def _to_token_tiles(ref, row0, rows, val):
    for j in range(SUBLANES):
        ref[pl.ds(row0 * SUBLANES + j, rows, stride=SUBLANES), :] = val[:, j * LANES:(j + 1) * LANES]


def _from_token_tiles(ref, row0, rows):
    return jnp.concatenate([ref[pl.ds(row0 * SUBLANES + j, rows, stride=SUBLANES), :] for j in range(SUBLANES)],
                           axis=1)


def _merge_kernel(x_ref, ya_ref, yb_ref, gate_ref, wa_ref, wb_ref, wo_ref, g1_ref, mod_ref, g2_ref,
                  wr_ref, br_ref, x1_ref, h2_ref, idx_ref, wt_ref):
    d = x_ref.shape[-1]
    t = TOKEN_TILE
    for sub in range(x_ref.shape[1] // t):
        tok = slice(sub * t, (sub + 1) * t)
        za = _dot(wa_ref[...], ya_ref[0, :, tok])
        zb = _dot(wb_ref[...], yb_ref[0, :, tok])
        mt = (gate_ref[0, 0:d, tok].astype(jnp.float32) * za
              + gate_ref[0, d:2 * d, tok].astype(jnp.float32) * zb)
        o = _dot(mt.T.astype(jnp.bfloat16), wo_ref[...])
        x1 = x_ref[0, tok, :] + g1_ref[0] * o
        x1_ref[0, tok, :] = x1
        h2 = _mod_norm(x1, g2_ref[...], mod_ref[0, 0:1, :], mod_ref[0, 1:2, :])
        _to_token_tiles(h2_ref, sub * t, t, h2)
        logits = _dot_nt(wr_ref[...], h2, precision=lax.Precision.HIGHEST) + br_ref[...]
        n_e = logits.shape[0]
        row = lax.broadcasted_iota(jnp.int32, logits.shape, 0)
        work = logits
        top_v, top_i = [], []
        for _ in range(TOP_K):
            mk = jnp.max(work, axis=0, keepdims=True)
            ik = jnp.min(jnp.where(work == mk, row, n_e), axis=0, keepdims=True)
            top_v.append(mk)
            top_i.append(ik)
            work = jnp.where(row == ik, -jnp.inf, work)
        ex = [jnp.exp(v - top_v[0]) for v in top_v]
        den = ex[0] + ex[1] + ex[2] + ex[3]
        idx_ref[0, :, tok] = jnp.concatenate(top_i, axis=0)
        wt_ref[0, :, tok] = jnp.concatenate([e / den for e in ex], axis=0)


def _merge(x, ya_t, yb_t, gate_t, wa_t, wb_t, wo, g1, mod2, norm2_g, wr_t, br):
    b, s, d = x.shape
    t = MERGE_TILE
    per = s // t
    assert d == SUBLANES * LANES
    full = lambda shape: pl.BlockSpec(shape, lambda bi, i: (0,) * len(shape))
    tok = pl.BlockSpec((1, t, d), lambda bi, i: (bi, i, 0))
    col = lambda rows: pl.BlockSpec((1, rows, t), lambda bi, i: (bi, 0, i))
    return pl.pallas_call(
        _merge_kernel,
        grid=(b, per),
        in_specs=[tok, col(Q_A), col(Q_B), col(2 * d),
                  full(wa_t.shape), full(wb_t.shape), full(wo.shape),
                  pl.BlockSpec((1, 1, d), lambda bi, i: (bi, 0, 0)),
                  pl.BlockSpec((1, 2, d), lambda bi, i: (bi, 0, 0)),
                  full((1, d)), full(wr_t.shape), full(br.shape)],
        out_specs=[tok, pl.BlockSpec((t * SUBLANES, LANES), lambda bi, i: (bi * per + i, 0)),
                   col(TOP_K), col(TOP_K)],
        out_shape=[jax.ShapeDtypeStruct((b, s, d), jnp.float32),
                   jax.ShapeDtypeStruct((b * s * SUBLANES, LANES), jnp.float32),
                   jax.ShapeDtypeStruct((b, TOP_K, s), jnp.int32), jax.ShapeDtypeStruct((b, TOP_K, s), jnp.float32)],
        compiler_params=_params(("arbitrary", "arbitrary")),
        name="merge_router",
    )(x, ya_t, yb_t, gate_t, wa_t, wb_t, wo, g1, mod2, norm2_g, wr_t, br)


def _w1_split_kernel(w_ref, p_ref, g_ref, l_ref):
    half = SPLIT_BLOCK // 2
    for jb in range(w_ref.shape[-1] // SPLIT_BLOCK):
        blk = w_ref[0, :, jb * SPLIT_BLOCK:(jb + 1) * SPLIT_BLOCK].astype(jnp.bfloat16)
        r = _dot(blk, p_ref[...])
        g_ref[0, :, jb * half:(jb + 1) * half] = r[:, :half].astype(jnp.bfloat16)
        l_ref[0, :, jb * half:(jb + 1) * half] = r[:, half:].astype(jnp.bfloat16)


def _w1_split(w1):
    n_e, d, d2 = w1.shape
    rows = 512
    half = SPLIT_BLOCK // 2
    perm = np.zeros((SPLIT_BLOCK, SPLIT_BLOCK), np.float32)
    for j in range(half):
        perm[2 * j, j] = 1.0
        perm[2 * j + 1, half + j] = 1.0
    out = pl.BlockSpec((1, rows, d2 // 2), lambda e, r: (e, r, 0))
    return pl.pallas_call(
        _w1_split_kernel,
        grid=(n_e, d // rows),
        in_specs=[pl.BlockSpec((1, rows, d2), lambda e, r: (e, r, 0)),
                  pl.BlockSpec((SPLIT_BLOCK, SPLIT_BLOCK), lambda e, r: (0, 0))],
        out_specs=[out, out],
        out_shape=[jax.ShapeDtypeStruct((n_e, d, d2 // 2), jnp.bfloat16)] * 2,
        compiler_params=_params(("arbitrary", "arbitrary")),
        name="w1_split",
    )(w1, jnp.asarray(perm, jnp.bfloat16))


def _moe_kernel(offs_ref, tok_ref, wt_ref, h_ref, w1g_ref, w1l_ref, b1g_ref, b1l_ref, w2_ref, b2_ref,
                acc_ref, xg_ref, y_ref):
    ti = pl.program_id(0)
    e = pl.program_id(1)
    n_e = pl.num_programs(1)
    ch = xg_ref.shape[0] // SUBLANES

    def tile_rows(r):
        return pl.ds(pl.multiple_of(r * SUBLANES, SUBLANES), SUBLANES)

    @pl.when((ti == 0) & (e == 0))
    def _():
        xg_ref[...] = jnp.zeros_like(xg_ref)

    @pl.when(e == 0)
    def _():
        acc_ref[...] = jnp.zeros_like(acc_ref)

    start = offs_ref[ti * (n_e + 1) + e]
    end = offs_ref[ti * (n_e + 1) + e + 1]

    def chunk(c, _):
        base = start + c * ch
        n = jnp.minimum(ch, end - base)
        groups = (n + ROW_UNROLL - 1) // ROW_UNROLL

        def gather(i, _):
            r0 = pl.multiple_of(i * ROW_UNROLL, ROW_UNROLL)
            for u in range(ROW_UNROLL):
                tok = tok_ref[0, 0, base + r0 + u]
                xg_ref[tile_rows(r0 + u), :] = h_ref[tile_rows(tok), :]
            return 0

        lax.fori_loop(0, groups, gather, 0)
        xg = _from_token_tiles(xg_ref, 0, ch).astype(jnp.bfloat16)
        glu = jnp.minimum(_dot(xg, w1g_ref[0]) + b1g_ref[0], SWIGLU_LIMIT)
        lin = jnp.clip(_dot(xg, w1l_ref[0]) + b1l_ref[0], -SWIGLU_LIMIT, SWIGLU_LIMIT)
        act = glu * jax.nn.sigmoid(SWIGLU_ALPHA * glu) * (lin + 1.0)
        _to_token_tiles(y_ref, 0, ch, _dot(act.astype(jnp.bfloat16), w2_ref[0]) + b2_ref[0])

        def scatter(i, _):
            r0 = pl.multiple_of(i * ROW_UNROLL, ROW_UNROLL)
            toks = [tok_ref[0, 0, base + r0 + u] for u in range(ROW_UNROLL)]
            new = [acc_ref[tile_rows(toks[u]), :] + wt_ref[0, 0, base + r0 + u] * y_ref[tile_rows(r0 + u), :]
                   for u in range(ROW_UNROLL)]
            for u in range(ROW_UNROLL):
                acc_ref[tile_rows(toks[u]), :] = new[u]
            return 0

        full_groups = n // ROW_UNROLL
        lax.fori_loop(0, full_groups, scatter, 0)

        def scatter_row(r, _):
            tok = tok_ref[0, 0, base + r]
            acc_ref[tile_rows(tok), :] = acc_ref[tile_rows(tok), :] + wt_ref[0, 0, base + r] * y_ref[tile_rows(r), :]
            return 0

        lax.fori_loop(full_groups * ROW_UNROLL, n, scatter_row, 0)
        return 0

    lax.fori_loop(0, (end - start + ch - 1) // ch, chunk, 0)


def _moe(offs, tok_sorted, wt_sorted, h2_tiles, w1g, w1l, b1g, b1l, w2, b2, tile):
    n_tok = h2_tiles.shape[0] // SUBLANES
    n_e, d, d_e = w1g.shape
    n_t = n_tok // tile
    a = tok_sorted.shape[-1]
    assert d == SUBLANES * LANES
    smem = lambda: pl.BlockSpec((1, 1, a), lambda ti, e, o: (ti, 0, 0), memory_space=pltpu.SMEM)
    ex = lambda shape: pl.BlockSpec((1,) + shape, lambda ti, e, o: (e, 0, 0))
    tiles = pl.BlockSpec((tile * SUBLANES, LANES), lambda ti, e, o: (ti, 0))
    grid_spec = pltpu.PrefetchScalarGridSpec(
        num_scalar_prefetch=1,
        grid=(n_t, n_e),
        in_specs=[smem(), smem(), tiles,
                  ex((d, d_e)), ex((d, d_e)), ex((1, d_e)), ex((1, d_e)), ex((d_e, d)), ex((1, d))],
        out_specs=tiles,
        scratch_shapes=[pltpu.VMEM((MOE_CHUNK * SUBLANES, LANES), jnp.float32),
                        pltpu.VMEM((MOE_CHUNK * SUBLANES, LANES), jnp.float32)],
    )
    return pl.pallas_call(
        _moe_kernel,
        grid_spec=grid_spec,
        out_shape=jax.ShapeDtypeStruct((n_tok * SUBLANES, LANES), jnp.float32),
        compiler_params=_params(("arbitrary", "arbitrary")),
        name="moe",
    )(offs, tok_sorted, wt_sorted, h2_tiles, w1g, w1l, b1g, b1l, w2, b2)


def _final_kernel(x_ref, m_ref, g2_ref, g_ref, o_ref):
    t = x_ref.shape[1]
    x = x_ref[0] + g2_ref[0] * _from_token_tiles(m_ref, 0, t)
    ms = jnp.mean(x * x, axis=-1, keepdims=True)
    o_ref[0] = x * lax.rsqrt(ms + RMS_EPS) * g_ref[...]


def _final(x1, moe_tiles, g2, final_g):
    b, s, d = x1.shape
    t = MERGE_TILE
    per = s // t
    tok = pl.BlockSpec((1, t, d), lambda bi, i: (bi, i, 0))
    return pl.pallas_call(
        _final_kernel,
        grid=(b, per),
        in_specs=[tok, pl.BlockSpec((t * SUBLANES, LANES), lambda bi, i: (bi * per + i, 0)),
                  pl.BlockSpec((1, 1, d), lambda bi, i: (bi, 0, 0)),
                  pl.BlockSpec((1, d), lambda bi, i: (0, 0))],
        out_specs=tok,
        out_shape=jax.ShapeDtypeStruct((b, s, d), jnp.float32),
        compiler_params=_params(("arbitrary", "arbitrary")),
        name="final_norm",
    )(x1, moe_tiles, g2, final_g)


def _rope_tables(s, n_ctx):
    inv = ROPE_THETA ** (-(jnp.arange(ROPE_QUARTER, dtype=jnp.float32) * 2.0 / ROPE_HALF))
    pos = jnp.arange(s)
    ang_r = (pos // GRID_W).astype(jnp.float32)[None, :] * inv[:, None]
    ang_c = (pos % GRID_W).astype(jnp.float32)[None, :] * inv[:, None]
    ang = jnp.concatenate([ang_r, ang_c], axis=0)
    cos = jnp.concatenate([jnp.cos(ang), jnp.ones((ROPE_HALF, n_ctx), jnp.float32)], axis=1)
    sin = jnp.concatenate([jnp.sin(ang), jnp.zeros((ROPE_HALF, n_ctx), jnp.float32)], axis=1)
    return cos, sin


def _route_lists(top_idx, top_w, tile):
    b, k, s = top_idx.shape
    n_t = b * s // tile
    per = s // tile
    e_flat = top_idx.reshape(b, k, per, tile).transpose(0, 2, 1, 3).reshape(n_t, k * tile)
    w_flat = top_w.reshape(b, k, per, tile).transpose(0, 2, 1, 3).reshape(n_t, k * tile)
    tok = jnp.tile(jnp.arange(tile, dtype=jnp.int32), k)[None, :].repeat(n_t, axis=0)
    e_sorted, tok_sorted, w_sorted = lax.sort((e_flat, tok, w_flat), dimension=1, num_keys=2)
    counts = jnp.sum(e_flat[:, :, None] == jnp.arange(N_EXPERTS, dtype=jnp.int32)[None, None, :], axis=1,
                     dtype=jnp.int32)
    offs = jnp.concatenate([jnp.zeros((n_t, 1), jnp.int32), jnp.cumsum(counts, axis=1, dtype=jnp.int32)], axis=1)
    pad = ((0, 0), (0, ROW_UNROLL))
    return offs.reshape(-1), jnp.pad(tok_sorted, pad)[:, None, :], jnp.pad(w_sorted, pad)[:, None, :]


def kernel(x, c, ctx, c_ctx, w_mod, b_mod, norm1_g, norm2_g, w_in, q_norm_g, k_norm_g, sink, w_br_a, w_br_b,
           w_o, w_router, b_router, w_e1, b_e1, w_e2, b_e2, final_g):
    b, s, d = x.shape
    n_ctx = ctx.shape[1]
    assert w_mod.shape[0] == 1
    bf = jnp.bfloat16
    cos, sin = _rope_tables(s, n_ctx)
    t = TOKEN_TILE
    moe_tile = min(MOE_TILE, s)
    for l in range(1):
        rows = ((b + 1 + 7) // 8) * 8
        cond = jnp.zeros((rows, d), jnp.float32).at[:b].set(c).at[b].set(c_ctx)
        mods = _adaln(cond, w_mod[l], b_mod[l])
        sh1, sc1, g1, sh2, sc2, g2 = jnp.split(mods, 6, axis=-1)
        mod1 = jnp.stack([jnp.stack([sh1[:b], sc1[:b]], axis=1),
                          jnp.broadcast_to(jnp.stack([sh1[b], sc1[b]], axis=0)[None], (b, 2, d))], axis=1)
        mod2 = jnp.stack([sh2[:b], sc2[:b]], axis=1)

        w = w_in[l]
        kv = w[:, :4 * KV_W]
        wk_t = jnp.concatenate([kv[:, 0:KV_W], kv[:, 2 * KV_W:3 * KV_W]], axis=1).T.astype(bf)
        wv_t = jnp.concatenate([kv[:, KV_W:2 * KV_W], kv[:, 3 * KV_W:4 * KV_W]], axis=1).T.astype(bf)
        wq_t = w[:, 4 * KV_W:4 * KV_W + Q_A + Q_B].T.astype(bf)
        wg_t = w[:, 4 * KV_W + Q_A + Q_B:].T.astype(bf)
        qg = jnp.broadcast_to((q_norm_g[l] * (ATTN_SCALE * LOG2E))[:, None], (HEAD_DIM, t))
        kg = jnp.broadcast_to(k_norm_g[l][:, None], (HEAD_DIM, t))

        qa_t, qb_t, ka, kb, va_t, vb_t, gate_t = _in_proj(
            x, ctx, mod1, norm1_g[l][None], wq_t, wk_t, wv_t, wg_t, qg, kg, cos, sin)

        ya_t = _attention(False, 128, 2, sink[l], qa_t, ka, va_t)
        yb_t = _attention(True, 256, 1, sink[l], qb_t, kb, vb_t)

        x1, h2, top_idx, top_w = _merge(
            x, ya_t, yb_t, gate_t, w_br_a[l].T.astype(bf), w_br_b[l].T.astype(bf), w_o[l].astype(bf),
            g1[:b, None, :], mod2, norm2_g[l][None], w_router[l].T, b_router[l][:, None])

        offs, tok_sorted, w_sorted = _route_lists(top_idx, top_w, moe_tile)
        w1g, w1l = _w1_split(w_e1[l])
        moe_tiles = _moe(offs, tok_sorted, w_sorted, h2, w1g, w1l,
                         b_e1[l][:, None, 0::2], b_e1[l][:, None, 1::2],
                         w_e2[l].astype(bf), b_e2[l][:, None, :], moe_tile)
        out = _final(x1, moe_tiles, g2[:b, None, :], final_g[None])
    return out
```

```python
import functools

import numpy as np
import jax
import jax.numpy as jnp
from jax import lax
from jax.experimental import pallas as pl
from jax.experimental.pallas import tpu as pltpu

HEAD_DIM = 64
KV_HEADS = 2
A_HEADS = 8
B_HEADS = 8
GROUP = A_HEADS // KV_HEADS
GRID_W = 64
WINDOW = 128
ROPE_HALF = HEAD_DIM // 2
ROPE_QUARTER = ROPE_HALF // 2
ROPE_THETA = 10000.0
N_EXPERTS = 32
TOP_K = 4
SWIGLU_ALPHA = 1.702
SWIGLU_LIMIT = 7.0
RMS_EPS = 1e-6
ATTN_SCALE = HEAD_DIM ** -0.5
LOG2E = 1.4426950408889634
NEG_INF = -1e30
KV_W = KV_HEADS * HEAD_DIM
Q_A = A_HEADS * HEAD_DIM
Q_B = B_HEADS * HEAD_DIM
V_ROWS = HEAD_DIM + 16

SUBLANES = 8
LANES = 128
TOKEN_TILE = 256
MERGE_TILE = 512
MOE_TILE = 2048
MOE_CHUNK = 288
ROW_UNROLL = 8
SPLIT_BLOCK = 256
VMEM_LIMIT = 56 * 1024 * 1024

_NT = (((1,), (1,)), ((), ()))


def _dot_nt(a, b, precision=None):
    return lax.dot_general(a, b, _NT, preferred_element_type=jnp.float32, precision=precision)


def _dot(a, b, precision=None):
    return jnp.dot(a, b, preferred_element_type=jnp.float32, precision=precision)


def _params(sem):
    return pltpu.CompilerParams(dimension_semantics=sem, vmem_limit_bytes=VMEM_LIMIT)


def _adaln_kernel(cond_ref, w_ref, b_ref, o_ref):
    cond = cond_ref[...]
    act = cond * jax.nn.sigmoid(cond)
    o_ref[...] = _dot(act, w_ref[...], precision=lax.Precision.HIGHEST) + b_ref[...]


def _adaln(cond, w, b):
    rows, d = cond.shape
    n = w.shape[1]
    tn = 1024
    return pl.pallas_call(
        _adaln_kernel,
        grid=(n // tn,),
        in_specs=[pl.BlockSpec((rows, d), lambda j: (0, 0)),
                  pl.BlockSpec((d, tn), lambda j: (0, j)),
                  pl.BlockSpec((1, tn), lambda j: (0, j))],
        out_specs=pl.BlockSpec((rows, tn), lambda j: (0, j)),
        out_shape=jax.ShapeDtypeStruct((rows, n), jnp.float32),
        compiler_params=_params(("arbitrary",)),
        name="adaln",
    )(cond, w, b.reshape(1, n))


def _rope_t(xh, cos, sin):
    q = ROPE_QUARTER
    a0, a1, b0, b1 = xh[0:q], xh[q:2 * q], xh[2 * q:3 * q], xh[3 * q:4 * q]
    cr, cc = cos[0:q], cos[q:2 * q]
    sr, sc = sin[0:q], sin[q:2 * q]
    return jnp.concatenate([a0 * cr - a1 * sr, a1 * cr + a0 * sr,
                            b0 * cc - b1 * sc, b1 * cc + b0 * sc], axis=0)


def _head_norm_t(xh, g):
    ms = jnp.mean(xh * xh, axis=0, keepdims=True)
    return xh * lax.rsqrt(ms + RMS_EPS) * g


def _mod_norm(x, g, shift, scale):
    ms = jnp.mean(x * x, axis=-1, keepdims=True)
    return x * lax.rsqrt(ms + RMS_EPS) * g * (1.0 + scale) + shift


def _in_proj_kernel(n_lat, x_ref, ctx_ref, mod_ref, g_ref, wq_ref, wk_ref, wv_ref, wg_ref,
                    qg_ref, kg_ref, cos_ref, sin_ref,
                    qa_ref, qb_ref, ka_ref, kb_ref, va_ref, vb_ref, gate_ref, h_ref):
    i = pl.program_id(1)
    shift = mod_ref[0, 0, 0:1, :]
    scale = mod_ref[0, 0, 1:2, :]

    @pl.when(i < n_lat)
    def _():
        h_ref[...] = _mod_norm(x_ref[0], g_ref[...], shift, scale).astype(jnp.bfloat16)

    @pl.when(i >= n_lat)
    def _():
        h_ref[...] = _mod_norm(ctx_ref[0], g_ref[...], shift, scale).astype(jnp.bfloat16)

    h = h_ref[...]
    cos = cos_ref[...]
    sin = sin_ref[...]
    t = h.shape[0]

    kt = _dot_nt(wk_ref[...], h)
    kg = kg_ref[...]
    ka = [_rope_t(_head_norm_t(kt[j * HEAD_DIM:(j + 1) * HEAD_DIM], kg), cos, sin) for j in range(KV_HEADS)]
    kb = [_rope_t(kt[KV_W + j * HEAD_DIM:KV_W + (j + 1) * HEAD_DIM], cos, sin) for j in range(KV_HEADS)]
    ka_ref[0, 0] = jnp.concatenate(ka, axis=0).T.astype(jnp.bfloat16)
    kb_ref[0, 0] = jnp.concatenate(kb, axis=0).T.astype(jnp.bfloat16)
    vt = _dot_nt(wv_ref[...], h)
    ones = jnp.ones((V_ROWS - HEAD_DIM, t), jnp.float32)
    for br, v_ref in enumerate((va_ref, vb_ref)):
        rows = []
        for j in range(KV_HEADS):
            rows += [vt[br * KV_W + j * HEAD_DIM:br * KV_W + (j + 1) * HEAD_DIM], ones]
        v_ref[0, 0] = jnp.concatenate(rows, axis=0).astype(jnp.bfloat16)

    @pl.when(i < n_lat)
    def _():
        qg = qg_ref[...]
        for half in range(2):
            qt = _dot_nt(wq_ref[half * Q_A:(half + 1) * Q_A, :], h)
            for hd in range(A_HEADS):
                xh = qt[hd * HEAD_DIM:(hd + 1) * HEAD_DIM]
                if half == 0:
                    out = _rope_t(_head_norm_t(xh, qg), cos, sin)
                    qa_ref[0, hd * HEAD_DIM:(hd + 1) * HEAD_DIM, :] = out.astype(jnp.bfloat16)
                else:
                    out = _rope_t(xh * (ATTN_SCALE * LOG2E), cos, sin)
                    qb_ref[0, hd * HEAD_DIM:(hd + 1) * HEAD_DIM, :] = out.astype(jnp.bfloat16)
        rows = 512
        for c in range(wg_ref.shape[0] // rows):
            gt = _dot_nt(wg_ref[c * rows:(c + 1) * rows, :], h)
            gate_ref[0, c * rows:(c + 1) * rows, :] = jax.nn.sigmoid(gt).astype(jnp.bfloat16)


def _in_proj(x, ctx, mod, norm_g, wq_t, wk_t, wv_t, wg_t, qg, kg, cos, sin):
    b, s, d = x.shape
    n_ctx = ctx.shape[1]
    t = TOKEN_TILE
    n_lat = s // t
    n_tiles = n_lat + n_ctx // t
    last = n_lat - 1
    full = lambda shape: pl.BlockSpec(shape, lambda bi, i: (0,) * len(shape))
    q_spec = pl.BlockSpec((1, Q_A, t), lambda bi, i: (bi, 0, jnp.minimum(i, last)))
    k_spec = pl.BlockSpec((1, 1, t, KV_W), lambda bi, i: (bi, i, 0, 0))
    v_spec = pl.BlockSpec((1, 1, KV_HEADS * V_ROWS, t), lambda bi, i: (bi, i, 0, 0))
    bf = jnp.bfloat16
    return pl.pallas_call(
        functools.partial(_in_proj_kernel, n_lat),
        grid=(b, n_tiles),
        in_specs=[pl.BlockSpec((1, t, d), lambda bi, i: (bi, jnp.minimum(i, last), 0)),
                  pl.BlockSpec((1, t, d), lambda bi, i: (bi, jnp.maximum(i - n_lat, 0), 0)),
                  pl.BlockSpec((1, 1, 2, d), lambda bi, i: (bi, i // n_lat, 0, 0)),
                  full((1, d)), full(wq_t.shape), full(wk_t.shape), full(wv_t.shape), full(wg_t.shape),
                  full((HEAD_DIM, t)), full((HEAD_DIM, t)),
                  pl.BlockSpec((ROPE_HALF, t), lambda bi, i: (0, i)),
                  pl.BlockSpec((ROPE_HALF, t), lambda bi, i: (0, i))],
        out_specs=[q_spec, q_spec, k_spec, k_spec, v_spec, v_spec,
                   pl.BlockSpec((1, 2 * d, t), lambda bi, i: (bi, 0, jnp.minimum(i, last)))],
        out_shape=[jax.ShapeDtypeStruct((b, Q_A, s), bf), jax.ShapeDtypeStruct((b, Q_B, s), bf),
                   jax.ShapeDtypeStruct((b, n_tiles, t, KV_W), bf), jax.ShapeDtypeStruct((b, n_tiles, t, KV_W), bf),
                   jax.ShapeDtypeStruct((b, n_tiles, KV_HEADS * V_ROWS, t), bf),
                   jax.ShapeDtypeStruct((b, n_tiles, KV_HEADS * V_ROWS, t), bf),
                   jax.ShapeDtypeStruct((b, 2 * d, s), bf)],
        scratch_shapes=[pltpu.VMEM((t, d), bf)],
        compiler_params=_params(("arbitrary", "arbitrary")),
        name="in_proj",
    )(x, ctx, mod, norm_g, wq_t, wk_t, wv_t, wg_t, qg, kg, cos, sin)


def _attn_kernel(windowed, tq, n_lat, n_tiles, unroll, sink_ref, q_ref, k_ref, v_ref, y_ref, s_ref):
    qi = pl.program_id(1)
    t = TOKEN_TILE
    n = GROUP * tq
    heads = [[kvh * GROUP + g for g in range(GROUP)] for kvh in range(KV_HEADS)]
    qpad = []
    for kvh in range(KV_HEADS):
        qcat = jnp.concatenate([q_ref[0, hd * HEAD_DIM:(hd + 1) * HEAD_DIM, :] for hd in heads[kvh]], axis=1)
        zeros = jnp.zeros_like(qcat)
        qpad.append(jnp.concatenate([qcat, zeros] if kvh == 0 else [zeros, qcat], axis=0))

    def scores(slot, kb, m8, lo=0, rows=t, mask=None):
        kblk = k_ref[0, kb, lo:lo + rows, :]
        out = []
        for kvh in range(KV_HEADS):
            s = _dot(kblk, qpad[kvh])
            if mask is not None:
                s = jnp.where(mask, s, NEG_INF)
            s_ref[kvh, slot, 0:rows, :] = s
            out.append(jnp.maximum(m8[kvh], jnp.max(s.reshape(rows // 8, 8, n), axis=0)))
        return tuple(out)

    def weigh(slot, kb, m, accs, lo=0, rows=t):
        out = []
        for kvh in range(KV_HEADS):
            p = jnp.exp2(s_ref[kvh, slot, 0:rows, :] - m[kvh]).astype(jnp.bfloat16)
            vblk = v_ref[0, kb, kvh * V_ROWS:(kvh + 1) * V_ROWS, lo:lo + rows]
            out.append(accs[kvh] + _dot(vblk, p))
        return tuple(out)

    acc0 = jnp.zeros((V_ROWS, n), jnp.float32)
    if windowed:
        sink = [jnp.concatenate([jnp.full((1, tq), sink_ref[hd] * LOG2E, jnp.float32) for hd in heads[kvh]], axis=1)
                for kvh in range(KV_HEADS)]
        assert tq == t and WINDOW * 2 == t
        c_half = lax.rem(lax.broadcasted_iota(jnp.int32, (WINDOW, n), 1), tq)
        r_half = lax.broadcasted_iota(jnp.int32, (WINDOW, n), 0)
        c_full = lax.rem(lax.broadcasted_iota(jnp.int32, (t, n), 1), tq)
        r_full = lax.broadcasted_iota(jnp.int32, (t, n), 0)
        off_prev = jnp.where(qi >= 1, 0, 2 * t)
        off_next = jnp.where(qi + 1 < n_lat, 0, 2 * t)
        pieces = [
            (jnp.maximum(qi - 1, 0), WINDOW, WINDOW, r_half >= c_half + off_prev),
            (qi, 0, t, jnp.abs(r_full - c_full) <= WINDOW),
            (jnp.minimum(qi + 1, n_lat - 1), 0, WINDOW, c_half - r_half >= WINDOW + off_next),
        ] + [(kb, 0, t, None) for kb in range(n_lat, n_tiles)]
        m8 = tuple(jnp.broadcast_to(sink[kvh], (8, n)) for kvh in range(KV_HEADS))
        for slot, (kb, lo, rows, ok) in enumerate(pieces):
            m8 = scores(slot, kb, m8, lo, rows, ok)
        m = [jnp.max(m8[kvh], axis=0, keepdims=True) for kvh in range(KV_HEADS)]
        row = lax.broadcasted_iota(jnp.int32, (V_ROWS, n), 0)
        accs = tuple(jnp.where(row == HEAD_DIM, jnp.exp2(sink[kvh] - m[kvh]), 0.0) for kvh in range(KV_HEADS))
        for slot, (kb, lo, rows, _) in enumerate(pieces):
            accs = weigh(slot, kb, m, accs, lo, rows)
    else:
        m8 = tuple(jnp.full((8, n), -jnp.inf, jnp.float32) for _ in range(KV_HEADS))
        m8 = lax.fori_loop(0, n_tiles, lambda kb, c: scores(kb, kb, c), m8, unroll=unroll)
        m = [jnp.max(m8[kvh], axis=0, keepdims=True) for kvh in range(KV_HEADS)]
        accs = lax.fori_loop(0, n_tiles, lambda kb, c: weigh(kb, kb, m, c), (acc0, acc0), unroll=unroll)
    for kvh in range(KV_HEADS):
        acc = accs[kvh]
        out = acc[0:HEAD_DIM] / acc[HEAD_DIM:HEAD_DIM + 1]
        for g, hd in enumerate(heads[kvh]):
            y_ref[0, hd * HEAD_DIM:(hd + 1) * HEAD_DIM, :] = out[:, g * tq:(g + 1) * tq].astype(jnp.bfloat16)


def _attention(windowed, tq, unroll, sink, q_t, k, v_t):
    b, hq, s = q_t.shape
    n_tiles = k.shape[1]
    n_lat = s // TOKEN_TILE
    slots = tq // TOKEN_TILE + 2 + (n_tiles - n_lat) if windowed else n_tiles
    grid_spec = pltpu.PrefetchScalarGridSpec(
        num_scalar_prefetch=1,
        grid=(b, s // tq),
        in_specs=[pl.BlockSpec((1, hq, tq), lambda bi, qi, sk: (bi, 0, qi)),
                  pl.BlockSpec((1,) + k.shape[1:], lambda bi, qi, sk: (bi, 0, 0, 0)),
                  pl.BlockSpec((1,) + v_t.shape[1:], lambda bi, qi, sk: (bi, 0, 0, 0))],
        out_specs=pl.BlockSpec((1, hq, tq), lambda bi, qi, sk: (bi, 0, qi)),
        scratch_shapes=[pltpu.VMEM((KV_HEADS, slots, TOKEN_TILE, GROUP * tq), jnp.float32)],
    )
    return pl.pallas_call(
        functools.partial(_attn_kernel, windowed, tq, n_lat, n_tiles, unroll),
        grid_spec=grid_spec,
        out_shape=jax.ShapeDtypeStruct((b, hq, s), jnp.bfloat16),
        compiler_params=_params(("arbitrary", "arbitrary")),
        name="attn_window" if windowed else "attn_global",
    )(sink, q_t, k, v_t)


def _to_token_tiles(ref, row0, rows, val):
    for j in range(SUBLANES):
        ref[pl.ds(row0 * SUBLANES + j, rows, stride=SUBLANES), :] = val[:, j * LANES:(j + 1) * LANES]


def _from_token_tiles(ref, row0, rows):
    return jnp.concatenate([ref[pl.ds(row0 * SUBLANES + j, rows, stride=SUBLANES), :] for j in range(SUBLANES)],
                           axis=1)


def _merge_kernel(x_ref, ya_ref, yb_ref, gate_ref, wa_ref, wb_ref, wo_ref, g1_ref, mod_ref, g2_ref,
                  wr_ref, br_ref, x1_ref, h2_ref, idx_ref, wt_ref):
    d = x_ref.shape[-1]
    t = TOKEN_TILE
    for sub in range(x_ref.shape[1] // t):
        tok = slice(sub * t, (sub + 1) * t)
        za = _dot(wa_ref[...], ya_ref[0, :, tok])
        zb = _dot(wb_ref[...], yb_ref[0, :, tok])
        mt = (gate_ref[0, 0:d, tok].astype(jnp.float32) * za
              + gate_ref[0, d:2 * d, tok].astype(jnp.float32) * zb)
        o = _dot(mt.T.astype(jnp.bfloat16), wo_ref[...])
        x1 = x_ref[0, tok, :] + g1_ref[0] * o
        x1_ref[0, tok, :] = x1
        h2 = _mod_norm(x1, g2_ref[...], mod_ref[0, 0:1, :], mod_ref[0, 1:2, :])
        _to_token_tiles(h2_ref, sub * t, t, h2)
        logits = _dot_nt(wr_ref[...], h2, precision=lax.Precision.HIGHEST) + br_ref[...]
        n_e = logits.shape[0]
        row = lax.broadcasted_iota(jnp.int32, logits.shape, 0)
        work = logits
        top_v, top_i = [], []
        for _ in range(TOP_K):
            mk = jnp.max(work, axis=0, keepdims=True)
            ik = jnp.min(jnp.where(work == mk, row, n_e), axis=0, keepdims=True)
            top_v.append(mk)
            top_i.append(ik)
            work = jnp.where(row == ik, -jnp.inf, work)
        ex = [jnp.exp(v - top_v[0]) for v in top_v]
        den = ex[0] + ex[1] + ex[2] + ex[3]
        idx_ref[0, :, tok] = jnp.concatenate(top_i, axis=0)
        wt_ref[0, :, tok] = jnp.concatenate([e / den for e in ex], axis=0)


def _merge(x, ya_t, yb_t, gate_t, wa_t, wb_t, wo, g1, mod2, norm2_g, wr_t, br):
    b, s, d = x.shape
    t = MERGE_TILE
    per = s // t
    assert d == SUBLANES * LANES
    full = lambda shape: pl.BlockSpec(shape, lambda bi, i: (0,) * len(shape))
    tok = pl.BlockSpec((1, t, d), lambda bi, i: (bi, i, 0))
    col = lambda rows: pl.BlockSpec((1, rows, t), lambda bi, i: (bi, 0, i))
    return pl.pallas_call(
        _merge_kernel,
        grid=(b, per),
        in_specs=[tok, col(Q_A), col(Q_B), col(2 * d),
                  full(wa_t.shape), full(wb_t.shape), full(wo.shape),
                  pl.BlockSpec((1, 1, d), lambda bi, i: (bi, 0, 0)),
                  pl.BlockSpec((1, 2, d), lambda bi, i: (bi, 0, 0)),
                  full((1, d)), full(wr_t.shape), full(br.shape)],
        out_specs=[tok, pl.BlockSpec((t * SUBLANES, LANES), lambda bi, i: (bi * per + i, 0)),
                   col(TOP_K), col(TOP_K)],
        out_shape=[jax.ShapeDtypeStruct((b, s, d), jnp.float32),
                   jax.ShapeDtypeStruct((b * s * SUBLANES, LANES), jnp.float32),
                   jax.ShapeDtypeStruct((b, TOP_K, s), jnp.int32), jax.ShapeDtypeStruct((b, TOP_K, s), jnp.float32)],
        compiler_params=_params(("arbitrary", "arbitrary")),
        name="merge_router",
    )(x, ya_t, yb_t, gate_t, wa_t, wb_t, wo, g1, mod2, norm2_g, wr_t, br)


def _w1_split_kernel(w_ref, p_ref, g_ref, l_ref):
    half = SPLIT_BLOCK // 2
    for jb in range(w_ref.shape[-1] // SPLIT_BLOCK):
        blk = w_ref[0, :, jb * SPLIT_BLOCK:(jb + 1) * SPLIT_BLOCK].astype(jnp.bfloat16)
        r = _dot(blk, p_ref[...])
        g_ref[0, :, jb * half:(jb + 1) * half] = r[:, :half].astype(jnp.bfloat16)
        l_ref[0, :, jb * half:(jb + 1) * half] = r[:, half:].astype(jnp.bfloat16)


def _w1_split(w1):
    n_e, d, d2 = w1.shape
    rows = 512
    half = SPLIT_BLOCK // 2
    perm = np.zeros((SPLIT_BLOCK, SPLIT_BLOCK), np.float32)
    for j in range(half):
        perm[2 * j, j] = 1.0
        perm[2 * j + 1, half + j] = 1.0
    out = pl.BlockSpec((1, rows, d2 // 2), lambda e, r: (e, r, 0))
    return pl.pallas_call(
        _w1_split_kernel,
        grid=(n_e, d // rows),
        in_specs=[pl.BlockSpec((1, rows, d2), lambda e, r: (e, r, 0)),
                  pl.BlockSpec((SPLIT_BLOCK, SPLIT_BLOCK), lambda e, r: (0, 0))],
        out_specs=[out, out],
        out_shape=[jax.ShapeDtypeStruct((n_e, d, d2 // 2), jnp.bfloat16)] * 2,
        compiler_params=_params(("arbitrary", "arbitrary")),
        name="w1_split",
    )(w1, jnp.asarray(perm, jnp.bfloat16))


def _moe_kernel(tile, offs_ref, tok_ref, wt_ref, h_ref, w1g_ref, w1l_ref, b1g_ref, b1l_ref, w2_ref, b2_ref,
                acc_ref, xg_ref, y_ref, xg2_ref, y2_ref):
    ti = pl.program_id(0)
    e = pl.program_id(1)
    n_e = pl.num_programs(1)
    ch = xg_ref.shape[0] // SUBLANES
    off = ti * (n_e + 1)

    def tile_rows(r):
        return pl.ds(pl.multiple_of(r * SUBLANES, SUBLANES), SUBLANES)

    def static_rows(r):
        return slice(r * SUBLANES, (r + 1) * SUBLANES)

    def expert_mlp(xg):
        glu = jnp.minimum(_dot(xg, w1g_ref[0]) + b1g_ref[0], SWIGLU_LIMIT)
        lin = jnp.clip(_dot(xg, w1l_ref[0]) + b1l_ref[0], -SWIGLU_LIMIT, SWIGLU_LIMIT)
        act = glu * jax.nn.sigmoid(SWIGLU_ALPHA * glu) * (lin + 1.0)
        return _dot(act.astype(jnp.bfloat16), w2_ref[0]) + b2_ref[0]

    def gather_rows(dst_ref, base, n):
        def group(i, _):
            r0 = pl.multiple_of(i * ROW_UNROLL, ROW_UNROLL)
            for u in range(ROW_UNROLL):
                dst_ref[tile_rows(r0 + u), :] = h_ref[tile_rows(tok_ref[0, 0, base + r0 + u]), :]
            return 0

        lax.fori_loop(0, (n + ROW_UNROLL - 1) // ROW_UNROLL, group, 0)

    def scatter_rows(src_ref, base, n):
        def group(i, _):
            r0 = pl.multiple_of(i * ROW_UNROLL, ROW_UNROLL)
            toks = [tok_ref[0, 0, base + r0 + u] for u in range(ROW_UNROLL)]
            new = [acc_ref[tile_rows(toks[u]), :] + wt_ref[0, 0, base + r0 + u] * src_ref[tile_rows(r0 + u), :]
                   for u in range(ROW_UNROLL)]
            for u in range(ROW_UNROLL):
                acc_ref[tile_rows(toks[u]), :] = new[u]
            return 0

        full_groups = n // ROW_UNROLL
        lax.fori_loop(0, full_groups, group, 0)

        def row(r, _):
            tok = tok_ref[0, 0, base + r]
            acc_ref[tile_rows(tok), :] = acc_ref[tile_rows(tok), :] + wt_ref[0, 0, base + r] * src_ref[tile_rows(r), :]
            return 0

        lax.fori_loop(full_groups * ROW_UNROLL, n, row, 0)

    start = offs_ref[off + e]
    count = offs_ref[off + e + 1] - start
    n_cur = jnp.minimum(count, ch)

    @pl.when((ti == 0) & (e == 0))
    def _():
        for ref in (xg_ref, y_ref, xg2_ref):
            ref[...] = jnp.zeros_like(ref)

    @pl.when(e == 0)
    def _():
        acc_ref[...] = jnp.zeros_like(acc_ref)
        gather_rows(xg_ref, start, n_cur)

    xg = _from_token_tiles(xg_ref, 0, ch).astype(jnp.bfloat16)

    start_next = offs_ref[off + jnp.minimum(e + 1, n_e - 1)]
    for r in range(ch):
        xg_ref[static_rows(r), :] = h_ref[tile_rows(tok_ref[0, 0, start_next + r]), :]

    prev = jnp.maximum(e - 1, 0)
    start_prev = offs_ref[off + prev]
    n_prev = jnp.where(e > 0, jnp.minimum(offs_ref[off + prev + 1] - start_prev, ch), 0)
    for g in range(ch // ROW_UNROLL):
        toks, new = [], []
        for u in range(ROW_UNROLL):
            r = g * ROW_UNROLL + u
            valid = r < n_prev
            tok = jnp.where(valid, tok_ref[0, 0, start_prev + r], tile + u)
            w = jnp.where(valid, wt_ref[0, 0, start_prev + r], 0.0)
            toks.append(tok)
            new.append(acc_ref[tile_rows(tok), :] + w * y_ref[static_rows(r), :])
        for u in range(ROW_UNROLL):
            acc_ref[tile_rows(toks[u]), :] = new[u]

    _to_token_tiles(y_ref, 0, ch, expert_mlp(xg))

    @pl.when(e == n_e - 1)
    def _():
        scatter_rows(y_ref, start, n_cur)

    @pl.when(count > ch)
    def _():
        def extra(c, _):
            base = start + c * ch
            n = jnp.minimum(ch, count - c * ch)
            gather_rows(xg2_ref, base, n)
            _to_token_tiles(y2_ref, 0, ch, expert_mlp(_from_token_tiles(xg2_ref, 0, ch).astype(jnp.bfloat16)))
            scatter_rows(y2_ref, base, n)
            return 0

        lax.fori_loop(1, (count + ch - 1) // ch, extra, 0)


def _moe(offs, tok_sorted, wt_sorted, h2_tiles, w1g, w1l, b1g, b1l, w2, b2, tile):
    n_tok = h2_tiles.shape[0] // SUBLANES
    n_e, d, d_e = w1g.shape
    n_t = n_tok // tile
    a = tok_sorted.shape[-1]
    assert d == SUBLANES * LANES and MOE_CHUNK % ROW_UNROLL == 0
    smem = lambda: pl.BlockSpec((1, 1, a), lambda ti, e, o: (ti, 0, 0), memory_space=pltpu.SMEM)
    ex = lambda shape: pl.BlockSpec((1,) + shape, lambda ti, e, o: (e, 0, 0))
    out_rows = (tile + ROW_UNROLL) * SUBLANES
    rows = pltpu.VMEM((MOE_CHUNK * SUBLANES, LANES), jnp.float32)
    grid_spec = pltpu.PrefetchScalarGridSpec(
        num_scalar_prefetch=1,
        grid=(n_t, n_e),
        in_specs=[smem(), smem(), pl.BlockSpec((tile * SUBLANES, LANES), lambda ti, e, o: (ti, 0)),
                  ex((d, d_e)), ex((d, d_e)), ex((1, d_e)), ex((1, d_e)), ex((d_e, d)), ex((1, d))],
        out_specs=pl.BlockSpec((out_rows, LANES), lambda ti, e, o: (ti, 0)),
        scratch_shapes=[rows, rows, rows, rows],
    )
    return pl.pallas_call(
        functools.partial(_moe_kernel, tile),
        grid_spec=grid_spec,
        out_shape=jax.ShapeDtypeStruct((n_t * out_rows, LANES), jnp.float32),
        compiler_params=_params(("arbitrary", "arbitrary")),
        name="moe",
    )(offs, tok_sorted, wt_sorted, h2_tiles, w1g, w1l, b1g, b1l, w2, b2)


def _final_kernel(sub, x_ref, m_ref, g2_ref, g_ref, o_ref):
    t = x_ref.shape[1]
    row0 = lax.rem(pl.program_id(1), sub) * t
    x = x_ref[0] + g2_ref[0] * _from_token_tiles(m_ref, row0, t)
    ms = jnp.mean(x * x, axis=-1, keepdims=True)
    o_ref[0] = x * lax.rsqrt(ms + RMS_EPS) * g_ref[...]


def _final(x1, moe_tiles, g2, final_g, moe_tile):
    b, s, d = x1.shape
    t = min(MERGE_TILE, moe_tile)
    per = s // t
    sub = moe_tile // t
    moe_rows = (moe_tile + ROW_UNROLL) * SUBLANES
    tok = pl.BlockSpec((1, t, d), lambda bi, i: (bi, i, 0))
    return pl.pallas_call(
        functools.partial(_final_kernel, sub),
        grid=(b, per),
        in_specs=[tok, pl.BlockSpec((moe_rows, LANES), lambda bi, i: (bi * (per // sub) + i // sub, 0)),
                  pl.BlockSpec((1, 1, d), lambda bi, i: (bi, 0, 0)),
                  pl.BlockSpec((1, d), lambda bi, i: (0, 0))],
        out_specs=tok,
        out_shape=jax.ShapeDtypeStruct((b, s, d), jnp.float32),
        compiler_params=_params(("arbitrary", "arbitrary")),
        name="final_norm",
    )(x1, moe_tiles, g2, final_g)


def _rope_tables(s, n_ctx):
    inv = ROPE_THETA ** (-(jnp.arange(ROPE_QUARTER, dtype=jnp.float32) * 2.0 / ROPE_HALF))
    pos = jnp.arange(s)
    ang_r = (pos // GRID_W).astype(jnp.float32)[None, :] * inv[:, None]
    ang_c = (pos % GRID_W).astype(jnp.float32)[None, :] * inv[:, None]
    ang = jnp.concatenate([ang_r, ang_c], axis=0)
    cos = jnp.concatenate([jnp.cos(ang), jnp.ones((ROPE_HALF, n_ctx), jnp.float32)], axis=1)
    sin = jnp.concatenate([jnp.sin(ang), jnp.zeros((ROPE_HALF, n_ctx), jnp.float32)], axis=1)
    return cos, sin


def _route_lists(top_idx, top_w, tile):
    b, k, s = top_idx.shape
    n_t = b * s // tile
    per = s // tile
    e_flat = top_idx.reshape(b, k, per, tile).transpose(0, 2, 1, 3).reshape(n_t, k * tile)
    w_flat = top_w.reshape(b, k, per, tile).transpose(0, 2, 1, 3).reshape(n_t, k * tile)
    tok = jnp.tile(jnp.arange(tile, dtype=jnp.int32), k)[None, :].repeat(n_t, axis=0)
    e_sorted, tok_sorted, w_sorted = lax.sort((e_flat, tok, w_flat), dimension=1, num_keys=2)
    counts = jnp.sum(e_flat[:, :, None] == jnp.arange(N_EXPERTS, dtype=jnp.int32)[None, None, :], axis=1,
                     dtype=jnp.int32)
    offs = jnp.concatenate([jnp.zeros((n_t, 1), jnp.int32), jnp.cumsum(counts, axis=1, dtype=jnp.int32)], axis=1)
    pad = ((0, 0), (0, MOE_CHUNK))
    return offs.reshape(-1), jnp.pad(tok_sorted, pad)[:, None, :], jnp.pad(w_sorted, pad)[:, None, :]


def kernel(x, c, ctx, c_ctx, w_mod, b_mod, norm1_g, norm2_g, w_in, q_norm_g, k_norm_g, sink, w_br_a, w_br_b,
           w_o, w_router, b_router, w_e1, b_e1, w_e2, b_e2, final_g):
    b, s, d = x.shape
    n_ctx = ctx.shape[1]
    assert w_mod.shape[0] == 1
    bf = jnp.bfloat16
    cos, sin = _rope_tables(s, n_ctx)
    t = TOKEN_TILE
    moe_tile = min(MOE_TILE, s)
    for l in range(1):
        rows = ((b + 1 + 7) // 8) * 8
        cond = jnp.zeros((rows, d), jnp.float32).at[:b].set(c).at[b].set(c_ctx)
        mods = _adaln(cond, w_mod[l], b_mod[l])
        sh1, sc1, g1, sh2, sc2, g2 = jnp.split(mods, 6, axis=-1)
        mod1 = jnp.stack([jnp.stack([sh1[:b], sc1[:b]], axis=1),
                          jnp.broadcast_to(jnp.stack([sh1[b], sc1[b]], axis=0)[None], (b, 2, d))], axis=1)
        mod2 = jnp.stack([sh2[:b], sc2[:b]], axis=1)

        w = w_in[l]
        kv = w[:, :4 * KV_W]
        wk_t = jnp.concatenate([kv[:, 0:KV_W], kv[:, 2 * KV_W:3 * KV_W]], axis=1).T.astype(bf)
        wv_t = jnp.concatenate([kv[:, KV_W:2 * KV_W], kv[:, 3 * KV_W:4 * KV_W]], axis=1).T.astype(bf)
        wq_t = w[:, 4 * KV_W:4 * KV_W + Q_A + Q_B].T.astype(bf)
        wg_t = w[:, 4 * KV_W + Q_A + Q_B:].T.astype(bf)
        qg = jnp.broadcast_to((q_norm_g[l] * (ATTN_SCALE * LOG2E))[:, None], (HEAD_DIM, t))
        kg = jnp.broadcast_to(k_norm_g[l][:, None], (HEAD_DIM, t))

        qa_t, qb_t, ka, kb, va_t, vb_t, gate_t = _in_proj(
            x, ctx, mod1, norm1_g[l][None], wq_t, wk_t, wv_t, wg_t, qg, kg, cos, sin)

        ya_t = _attention(False, 128, 2, sink[l], qa_t, ka, va_t)
        yb_t = _attention(True, 256, 1, sink[l], qb_t, kb, vb_t)

        x1, h2, top_idx, top_w = _merge(
            x, ya_t, yb_t, gate_t, w_br_a[l].T.astype(bf), w_br_b[l].T.astype(bf), w_o[l].astype(bf),
            g1[:b, None, :], mod2, norm2_g[l][None], w_router[l].T, b_router[l][:, None])

        offs, tok_sorted, w_sorted = _route_lists(top_idx, top_w, moe_tile)
        w1g, w1l = _w1_split(w_e1[l])
        moe_tiles = _moe(offs, tok_sorted, w_sorted, h2, w1g, w1l,
                         b_e1[l][:, None, 0::2], b_e1[l][:, None, 1::2],
                         w_e2[l].astype(bf), b_e2[l][:, None, :], moe_tile)
        out = _final(x1, moe_tiles, g2[:b, None, :], final_g[None], moe_tile)
    return out
```

```python
import functools

import numpy as np
import jax
import jax.numpy as jnp
from jax import lax
from jax.experimental import pallas as pl
from jax.experimental.pallas import tpu as pltpu

HEAD_DIM = 64
KV_HEADS = 2
A_HEADS = 8
B_HEADS = 8
GROUP = A_HEADS // KV_HEADS
GRID_W = 64
WINDOW = 128
ROPE_HALF = HEAD_DIM // 2
ROPE_QUARTER = ROPE_HALF // 2
ROPE_THETA = 10000.0
N_EXPERTS = 32
TOP_K = 4
SWIGLU_ALPHA = 1.702
SWIGLU_LIMIT = 7.0
RMS_EPS = 1e-6
ATTN_SCALE = HEAD_DIM ** -0.5
LOG2E = 1.4426950408889634
NEG_INF = -1e30
KV_W = KV_HEADS * HEAD_DIM
Q_A = A_HEADS * HEAD_DIM
Q_B = B_HEADS * HEAD_DIM
V_ROWS = HEAD_DIM + 16

SUBLANES = 8
LANES = 128
TOKEN_TILE = 256
MERGE_TILE = 512
MOE_TILE = 4096
MOE_CHUNK = 560
MOE_EXTRA_CHUNK = 64
MOE_VMEM_LIMIT = 60 * 1024 * 1024
ROW_UNROLL = 8
SPLIT_BLOCK = 256
VMEM_LIMIT = 56 * 1024 * 1024

_NT = (((1,), (1,)), ((), ()))


def _dot_nt(a, b, precision=None):
    return lax.dot_general(a, b, _NT, preferred_element_type=jnp.float32, precision=precision)


def _dot(a, b, precision=None):
    return jnp.dot(a, b, preferred_element_type=jnp.float32, precision=precision)


def _params(sem):
    return pltpu.CompilerParams(dimension_semantics=sem, vmem_limit_bytes=VMEM_LIMIT)


def _adaln_kernel(cond_ref, w_ref, b_ref, o_ref):
    cond = cond_ref[...]
    act = cond * jax.nn.sigmoid(cond)
    o_ref[...] = _dot(act, w_ref[...], precision=lax.Precision.HIGHEST) + b_ref[...]


def _adaln(cond, w, b):
    rows, d = cond.shape
    n = w.shape[1]
    tn = 1024
    return pl.pallas_call(
        _adaln_kernel,
        grid=(n // tn,),
        in_specs=[pl.BlockSpec((rows, d), lambda j: (0, 0)),
                  pl.BlockSpec((d, tn), lambda j: (0, j)),
                  pl.BlockSpec((1, tn), lambda j: (0, j))],
        out_specs=pl.BlockSpec((rows, tn), lambda j: (0, j)),
        out_shape=jax.ShapeDtypeStruct((rows, n), jnp.float32),
        compiler_params=_params(("arbitrary",)),
        name="adaln",
    )(cond, w, b.reshape(1, n))


def _rope_t(xh, cos, sin):
    q = ROPE_QUARTER
    a0, a1, b0, b1 = xh[0:q], xh[q:2 * q], xh[2 * q:3 * q], xh[3 * q:4 * q]
    cr, cc = cos[0:q], cos[q:2 * q]
    sr, sc = sin[0:q], sin[q:2 * q]
    return jnp.concatenate([a0 * cr - a1 * sr, a1 * cr + a0 * sr,
                            b0 * cc - b1 * sc, b1 * cc + b0 * sc], axis=0)


def _head_norm_t(xh, g):
    ms = jnp.mean(xh * xh, axis=0, keepdims=True)
    return xh * lax.rsqrt(ms + RMS_EPS) * g


def _mod_norm(x, g, shift, scale):
    ms = jnp.mean(x * x, axis=-1, keepdims=True)
    return x * lax.rsqrt(ms + RMS_EPS) * g * (1.0 + scale) + shift


def _in_proj_kernel(n_lat, x_ref, ctx_ref, mod_ref, g_ref, wq_ref, wk_ref, wv_ref, wg_ref,
                    qg_ref, kg_ref, cos_ref, sin_ref,
                    qa_ref, qb_ref, ka_ref, kb_ref, va_ref, vb_ref, gate_ref, h_ref):
    i = pl.program_id(1)
    shift = mod_ref[0, 0, 0:1, :]
    scale = mod_ref[0, 0, 1:2, :]

    @pl.when(i < n_lat)
    def _():
        h_ref[...] = _mod_norm(x_ref[0], g_ref[...], shift, scale).astype(jnp.bfloat16)

    @pl.when(i >= n_lat)
    def _():
        h_ref[...] = _mod_norm(ctx_ref[0], g_ref[...], shift, scale).astype(jnp.bfloat16)

    h = h_ref[...]
    cos = cos_ref[...]
    sin = sin_ref[...]
    t = h.shape[0]

    kt = _dot_nt(wk_ref[...], h)
    kg = kg_ref[...]
    ka = [_rope_t(_head_norm_t(kt[j * HEAD_DIM:(j + 1) * HEAD_DIM], kg), cos, sin) for j in range(KV_HEADS)]
    kb = [_rope_t(kt[KV_W + j * HEAD_DIM:KV_W + (j + 1) * HEAD_DIM], cos, sin) for j in range(KV_HEADS)]
    ka_ref[0, 0] = jnp.concatenate(ka, axis=0).T.astype(jnp.bfloat16)
    kb_ref[0, 0] = jnp.concatenate(kb, axis=0).T.astype(jnp.bfloat16)
    vt = _dot_nt(wv_ref[...], h)
    ones = jnp.ones((V_ROWS - HEAD_DIM, t), jnp.float32)
    for br, v_ref in enumerate((va_ref, vb_ref)):
        rows = []
        for j in range(KV_HEADS):
            rows += [vt[br * KV_W + j * HEAD_DIM:br * KV_W + (j + 1) * HEAD_DIM], ones]
        v_ref[0, 0] = jnp.concatenate(rows, axis=0).astype(jnp.bfloat16)

    @pl.when(i < n_lat)
    def _():
        qg = qg_ref[...]
        for half in range(2):
            qt = _dot_nt(wq_ref[half * Q_A:(half + 1) * Q_A, :], h)
            for hd in range(A_HEADS):
                xh = qt[hd * HEAD_DIM:(hd + 1) * HEAD_DIM]
                if half == 0:
                    out = _rope_t(_head_norm_t(xh, qg), cos, sin)
                    qa_ref[0, hd * HEAD_DIM:(hd + 1) * HEAD_DIM, :] = out.astype(jnp.bfloat16)
                else:
                    out = _rope_t(xh * (ATTN_SCALE * LOG2E), cos, sin)
                    qb_ref[0, hd * HEAD_DIM:(hd + 1) * HEAD_DIM, :] = out.astype(jnp.bfloat16)
        rows = 512
        for c in range(wg_ref.shape[0] // rows):
            gt = _dot_nt(wg_ref[c * rows:(c + 1) * rows, :], h)
            gate_ref[0, c * rows:(c + 1) * rows, :] = jax.nn.sigmoid(gt).astype(jnp.bfloat16)


def _in_proj(x, ctx, mod, norm_g, wq_t, wk_t, wv_t, wg_t, qg, kg, cos, sin):
    b, s, d = x.shape
    n_ctx = ctx.shape[1]
    t = TOKEN_TILE
    n_lat = s // t
    n_tiles = n_lat + n_ctx // t
    last = n_lat - 1
    full = lambda shape: pl.BlockSpec(shape, lambda bi, i: (0,) * len(shape))
    q_spec = pl.BlockSpec((1, Q_A, t), lambda bi, i: (bi, 0, jnp.minimum(i, last)))
    k_spec = pl.BlockSpec((1, 1, t, KV_W), lambda bi, i: (bi, i, 0, 0))
    v_spec = pl.BlockSpec((1, 1, KV_HEADS * V_ROWS, t), lambda bi, i: (bi, i, 0, 0))
    bf = jnp.bfloat16
    return pl.pallas_call(
        functools.partial(_in_proj_kernel, n_lat),
        grid=(b, n_tiles),
        in_specs=[pl.BlockSpec((1, t, d), lambda bi, i: (bi, jnp.minimum(i, last), 0)),
                  pl.BlockSpec((1, t, d), lambda bi, i: (bi, jnp.maximum(i - n_lat, 0), 0)),
                  pl.BlockSpec((1, 1, 2, d), lambda bi, i: (bi, i // n_lat, 0, 0)),
                  full((1, d)), full(wq_t.shape), full(wk_t.shape), full(wv_t.shape), full(wg_t.shape),
                  full((HEAD_DIM, t)), full((HEAD_DIM, t)),
                  pl.BlockSpec((ROPE_HALF, t), lambda bi, i: (0, i)),
                  pl.BlockSpec((ROPE_HALF, t), lambda bi, i: (0, i))],
        out_specs=[q_spec, q_spec, k_spec, k_spec, v_spec, v_spec,
                   pl.BlockSpec((1, 2 * d, t), lambda bi, i: (bi, 0, jnp.minimum(i, last)))],
        out_shape=[jax.ShapeDtypeStruct((b, Q_A, s), bf), jax.ShapeDtypeStruct((b, Q_B, s), bf),
                   jax.ShapeDtypeStruct((b, n_tiles, t, KV_W), bf), jax.ShapeDtypeStruct((b, n_tiles, t, KV_W), bf),
                   jax.ShapeDtypeStruct((b, n_tiles, KV_HEADS * V_ROWS, t), bf),
                   jax.ShapeDtypeStruct((b, n_tiles, KV_HEADS * V_ROWS, t), bf),
                   jax.ShapeDtypeStruct((b, 2 * d, s), bf)],
        scratch_shapes=[pltpu.VMEM((t, d), bf)],
        compiler_params=_params(("arbitrary", "arbitrary")),
        name="in_proj",
    )(x, ctx, mod, norm_g, wq_t, wk_t, wv_t, wg_t, qg, kg, cos, sin)


def _attn_kernel(windowed, tq, n_lat, n_tiles, unroll, sink_ref, q_ref, k_ref, v_ref, y_ref, s_ref):
    qi = pl.program_id(1)
    t = TOKEN_TILE
    n = GROUP * tq
    heads = [[kvh * GROUP + g for g in range(GROUP)] for kvh in range(KV_HEADS)]
    qpad = []
    for kvh in range(KV_HEADS):
        qcat = jnp.concatenate([q_ref[0, hd * HEAD_DIM:(hd + 1) * HEAD_DIM, :] for hd in heads[kvh]], axis=1)
        zeros = jnp.zeros_like(qcat)
        qpad.append(jnp.concatenate([qcat, zeros] if kvh == 0 else [zeros, qcat], axis=0))

    def scores(slot, kb, m8, lo=0, rows=t, mask=None):
        kblk = k_ref[0, kb, lo:lo + rows, :]
        out = []
        for kvh in range(KV_HEADS):
            s = _dot(kblk, qpad[kvh])
            if mask is not None:
                s = jnp.where(mask, s, NEG_INF)
            s_ref[kvh, slot, 0:rows, :] = s
            out.append(jnp.maximum(m8[kvh], jnp.max(s.reshape(rows // 8, 8, n), axis=0)))
        return tuple(out)

    def weigh(slot, kb, m, accs, lo=0, rows=t):
        out = []
        for kvh in range(KV_HEADS):
            p = jnp.exp2(s_ref[kvh, slot, 0:rows, :] - m[kvh]).astype(jnp.bfloat16)
            vblk = v_ref[0, kb, kvh * V_ROWS:(kvh + 1) * V_ROWS, lo:lo + rows]
            out.append(accs[kvh] + _dot(vblk, p))
        return tuple(out)

    acc0 = jnp.zeros((V_ROWS, n), jnp.float32)
    if windowed:
        sink = [jnp.concatenate([jnp.full((1, tq), sink_ref[hd] * LOG2E, jnp.float32) for hd in heads[kvh]], axis=1)
                for kvh in range(KV_HEADS)]
        assert tq == t and WINDOW * 2 == t
        c_half = lax.rem(lax.broadcasted_iota(jnp.int32, (WINDOW, n), 1), tq)
        r_half = lax.broadcasted_iota(jnp.int32, (WINDOW, n), 0)
        c_full = lax.rem(lax.broadcasted_iota(jnp.int32, (t, n), 1), tq)
        r_full = lax.broadcasted_iota(jnp.int32, (t, n), 0)
        off_prev = jnp.where(qi >= 1, 0, 2 * t)
        off_next = jnp.where(qi + 1 < n_lat, 0, 2 * t)
        pieces = [
            (jnp.maximum(qi - 1, 0), WINDOW, WINDOW, r_half >= c_half + off_prev),
            (qi, 0, t, jnp.abs(r_full - c_full) <= WINDOW),
            (jnp.minimum(qi + 1, n_lat - 1), 0, WINDOW, c_half - r_half >= WINDOW + off_next),
        ] + [(kb, 0, t, None) for kb in range(n_lat, n_tiles)]
        m8 = tuple(jnp.broadcast_to(sink[kvh], (8, n)) for kvh in range(KV_HEADS))
        for slot, (kb, lo, rows, ok) in enumerate(pieces):
            m8 = scores(slot, kb, m8, lo, rows, ok)
        m = [jnp.max(m8[kvh], axis=0, keepdims=True) for kvh in range(KV_HEADS)]
        row = lax.broadcasted_iota(jnp.int32, (V_ROWS, n), 0)
        accs = tuple(jnp.where(row == HEAD_DIM, jnp.exp2(sink[kvh] - m[kvh]), 0.0) for kvh in range(KV_HEADS))
        for slot, (kb, lo, rows, _) in enumerate(pieces):
            accs = weigh(slot, kb, m, accs, lo, rows)
    else:
        m8 = tuple(jnp.full((8, n), -jnp.inf, jnp.float32) for _ in range(KV_HEADS))
        m8 = lax.fori_loop(0, n_tiles, lambda kb, c: scores(kb, kb, c), m8, unroll=unroll)
        m = [jnp.max(m8[kvh], axis=0, keepdims=True) for kvh in range(KV_HEADS)]
        accs = lax.fori_loop(0, n_tiles, lambda kb, c: weigh(kb, kb, m, c), (acc0, acc0), unroll=unroll)
    for kvh in range(KV_HEADS):
        acc = accs[kvh]
        out = acc[0:HEAD_DIM] / acc[HEAD_DIM:HEAD_DIM + 1]
        for g, hd in enumerate(heads[kvh]):
            y_ref[0, hd * HEAD_DIM:(hd + 1) * HEAD_DIM, :] = out[:, g * tq:(g + 1) * tq].astype(jnp.bfloat16)


def _attention(windowed, tq, unroll, sink, q_t, k, v_t):
    b, hq, s = q_t.shape
    n_tiles = k.shape[1]
    n_lat = s // TOKEN_TILE
    slots = tq // TOKEN_TILE + 2 + (n_tiles - n_lat) if windowed else n_tiles
    grid_spec = pltpu.PrefetchScalarGridSpec(
        num_scalar_prefetch=1,
        grid=(b, s // tq),
        in_specs=[pl.BlockSpec((1, hq, tq), lambda bi, qi, sk: (bi, 0, qi)),
                  pl.BlockSpec((1,) + k.shape[1:], lambda bi, qi, sk: (bi, 0, 0, 0)),
                  pl.BlockSpec((1,) + v_t.shape[1:], lambda bi, qi, sk: (bi, 0, 0, 0))],
        out_specs=pl.BlockSpec((1, hq, tq), lambda bi, qi, sk: (bi, 0, qi)),
        scratch_shapes=[pltpu.VMEM((KV_HEADS, slots, TOKEN_TILE, GROUP * tq), jnp.float32)],
    )
    return pl.pallas_call(
        functools.partial(_attn_kernel, windowed, tq, n_lat, n_tiles, unroll),
        grid_spec=grid_spec,
        out_shape=jax.ShapeDtypeStruct((b, hq, s), jnp.bfloat16),
        compiler_params=_params(("arbitrary", "arbitrary")),
        name="attn_window" if windowed else "attn_global",
    )(sink, q_t, k, v_t)


def _to_token_tiles(ref, row0, rows, val):
    for j in range(SUBLANES):
        ref[pl.ds(row0 * SUBLANES + j, rows, stride=SUBLANES), :] = val[:, j * LANES:(j + 1) * LANES]


def _from_token_tiles(ref, row0, rows):
    return jnp.concatenate([ref[pl.ds(row0 * SUBLANES + j, rows, stride=SUBLANES), :] for j in range(SUBLANES)],
                           axis=1)


def _merge_kernel(x_ref, ya_ref, yb_ref, gate_ref, wa_ref, wb_ref, wo_ref, g1_ref, mod_ref, g2_ref,
                  wr_ref, br_ref, x1_ref, h2_ref, idx_ref, wt_ref):
    d = x_ref.shape[-1]
    t = TOKEN_TILE
    for sub in range(x_ref.shape[1] // t):
        tok = slice(sub * t, (sub + 1) * t)
        za = _dot(wa_ref[...], ya_ref[0, :, tok])
        zb = _dot(wb_ref[...], yb_ref[0, :, tok])
        mt = (gate_ref[0, 0:d, tok].astype(jnp.float32) * za
              + gate_ref[0, d:2 * d, tok].astype(jnp.float32) * zb)
        o = _dot(mt.T.astype(jnp.bfloat16), wo_ref[...])
        x1 = x_ref[0, tok, :] + g1_ref[0] * o
        x1_ref[0, tok, :] = x1
        h2 = _mod_norm(x1, g2_ref[...], mod_ref[0, 0:1, :], mod_ref[0, 1:2, :])
        _to_token_tiles(h2_ref, sub * t, t, h2)
        logits = _dot_nt(wr_ref[...], h2, precision=lax.Precision.HIGHEST) + br_ref[...]
        n_e = logits.shape[0]
        row = lax.broadcasted_iota(jnp.int32, logits.shape, 0)
        work = logits
        top_v, top_i = [], []
        for _ in range(TOP_K):
            mk = jnp.max(work, axis=0, keepdims=True)
            ik = jnp.min(jnp.where(work == mk, row, n_e), axis=0, keepdims=True)
            top_v.append(mk)
            top_i.append(ik)
            work = jnp.where(row == ik, -jnp.inf, work)
        ex = [jnp.exp(v - top_v[0]) for v in top_v]
        den = ex[0] + ex[1] + ex[2] + ex[3]
        idx_ref[0, :, tok] = jnp.concatenate(top_i, axis=0)
        wt_ref[0, :, tok] = jnp.concatenate([e / den for e in ex], axis=0)


def _merge(x, ya_t, yb_t, gate_t, wa_t, wb_t, wo, g1, mod2, norm2_g, wr_t, br):
    b, s, d = x.shape
    t = MERGE_TILE
    per = s // t
    assert d == SUBLANES * LANES
    full = lambda shape: pl.BlockSpec(shape, lambda bi, i: (0,) * len(shape))
    tok = pl.BlockSpec((1, t, d), lambda bi, i: (bi, i, 0))
    col = lambda rows: pl.BlockSpec((1, rows, t), lambda bi, i: (bi, 0, i))
    return pl.pallas_call(
        _merge_kernel,
        grid=(b, per),
        in_specs=[tok, col(Q_A), col(Q_B), col(2 * d),
                  full(wa_t.shape), full(wb_t.shape), full(wo.shape),
                  pl.BlockSpec((1, 1, d), lambda bi, i: (bi, 0, 0)),
                  pl.BlockSpec((1, 2, d), lambda bi, i: (bi, 0, 0)),
                  full((1, d)), full(wr_t.shape), full(br.shape)],
        out_specs=[tok, pl.BlockSpec((t * SUBLANES, LANES), lambda bi, i: (bi * per + i, 0)),
                   col(TOP_K), col(TOP_K)],
        out_shape=[jax.ShapeDtypeStruct((b, s, d), jnp.float32),
                   jax.ShapeDtypeStruct((b * s * SUBLANES, LANES), jnp.float32),
                   jax.ShapeDtypeStruct((b, TOP_K, s), jnp.int32), jax.ShapeDtypeStruct((b, TOP_K, s), jnp.float32)],
        compiler_params=_params(("arbitrary", "arbitrary")),
        name="merge_router",
    )(x, ya_t, yb_t, gate_t, wa_t, wb_t, wo, g1, mod2, norm2_g, wr_t, br)


def _w1_split_kernel(w_ref, p_ref, g_ref, l_ref):
    half = SPLIT_BLOCK // 2
    for jb in range(w_ref.shape[-1] // SPLIT_BLOCK):
        blk = w_ref[0, :, jb * SPLIT_BLOCK:(jb + 1) * SPLIT_BLOCK].astype(jnp.bfloat16)
        r = _dot(blk, p_ref[...])
        g_ref[0, :, jb * half:(jb + 1) * half] = r[:, :half].astype(jnp.bfloat16)
        l_ref[0, :, jb * half:(jb + 1) * half] = r[:, half:].astype(jnp.bfloat16)


def _w1_split(w1):
    n_e, d, d2 = w1.shape
    rows = 512
    half = SPLIT_BLOCK // 2
    perm = np.zeros((SPLIT_BLOCK, SPLIT_BLOCK), np.float32)
    for j in range(half):
        perm[2 * j, j] = 1.0
        perm[2 * j + 1, half + j] = 1.0
    out = pl.BlockSpec((1, rows, d2 // 2), lambda e, r: (e, r, 0))
    return pl.pallas_call(
        _w1_split_kernel,
        grid=(n_e, d // rows),
        in_specs=[pl.BlockSpec((1, rows, d2), lambda e, r: (e, r, 0)),
                  pl.BlockSpec((SPLIT_BLOCK, SPLIT_BLOCK), lambda e, r: (0, 0))],
        out_specs=[out, out],
        out_shape=[jax.ShapeDtypeStruct((n_e, d, d2 // 2), jnp.bfloat16)] * 2,
        compiler_params=_params(("arbitrary", "arbitrary")),
        name="w1_split",
    )(w1, jnp.asarray(perm, jnp.bfloat16))


def _moe_kernel(tile, offs_ref, tok_ref, wt_ref, h_ref, w1g_ref, w1l_ref, b1g_ref, b1l_ref, w2_ref, b2_ref,
                acc_ref, xg_ref, y_ref, xg2_ref, y2_ref):
    ti = pl.program_id(0)
    e = pl.program_id(1)
    n_e = pl.num_programs(1)
    ch = xg_ref.shape[0] // SUBLANES
    off = ti * (n_e + 1)

    def tile_rows(r):
        return pl.ds(pl.multiple_of(r * SUBLANES, SUBLANES), SUBLANES)

    def static_rows(r):
        return slice(r * SUBLANES, (r + 1) * SUBLANES)

    def expert_mlp(xg):
        glu = jnp.minimum(_dot(xg, w1g_ref[0]) + b1g_ref[0], SWIGLU_LIMIT)
        lin = jnp.clip(_dot(xg, w1l_ref[0]) + b1l_ref[0], -SWIGLU_LIMIT, SWIGLU_LIMIT)
        act = glu * jax.nn.sigmoid(SWIGLU_ALPHA * glu) * (lin + 1.0)
        return _dot(act.astype(jnp.bfloat16), w2_ref[0]) + b2_ref[0]

    def gather_rows(dst_ref, base, n):
        def group(i, _):
            r0 = pl.multiple_of(i * ROW_UNROLL, ROW_UNROLL)
            for u in range(ROW_UNROLL):
                dst_ref[tile_rows(r0 + u), :] = h_ref[tile_rows(tok_ref[0, 0, base + r0 + u]), :]
            return 0

        lax.fori_loop(0, (n + ROW_UNROLL - 1) // ROW_UNROLL, group, 0)

    def scatter_rows(src_ref, base, n):
        def group(i, _):
            r0 = pl.multiple_of(i * ROW_UNROLL, ROW_UNROLL)
            toks = [tok_ref[0, 0, base + r0 + u] for u in range(ROW_UNROLL)]
            new = [acc_ref[tile_rows(toks[u]), :] + wt_ref[0, 0, base + r0 + u] * src_ref[tile_rows(r0 + u), :]
                   for u in range(ROW_UNROLL)]
            for u in range(ROW_UNROLL):
                acc_ref[tile_rows(toks[u]), :] = new[u]
            return 0

        full_groups = n // ROW_UNROLL
        lax.fori_loop(0, full_groups, group, 0)

        def row(r, _):
            tok = tok_ref[0, 0, base + r]
            acc_ref[tile_rows(tok), :] = acc_ref[tile_rows(tok), :] + wt_ref[0, 0, base + r] * src_ref[tile_rows(r), :]
            return 0

        lax.fori_loop(full_groups * ROW_UNROLL, n, row, 0)

    start = offs_ref[off + e]
    count = offs_ref[off + e + 1] - start
    n_cur = jnp.minimum(count, ch)

    @pl.when((ti == 0) & (e == 0))
    def _():
        for ref in (xg_ref, y_ref, xg2_ref):
            ref[...] = jnp.zeros_like(ref)

    @pl.when(e == 0)
    def _():
        acc_ref[...] = jnp.zeros_like(acc_ref)
        gather_rows(xg_ref, start, n_cur)

    xg = _from_token_tiles(xg_ref, 0, ch).astype(jnp.bfloat16)

    start_next = offs_ref[off + jnp.minimum(e + 1, n_e - 1)]
    for r in range(ch):
        xg_ref[static_rows(r), :] = h_ref[tile_rows(tok_ref[0, 0, start_next + r]), :]

    prev = jnp.maximum(e - 1, 0)
    start_prev = offs_ref[off + prev]
    n_prev = jnp.where(e > 0, jnp.minimum(offs_ref[off + prev + 1] - start_prev, ch), 0)
    for g in range(ch // ROW_UNROLL):
        toks, new = [], []
        for u in range(ROW_UNROLL):
            r = g * ROW_UNROLL + u
            valid = r < n_prev
            tok = jnp.where(valid, tok_ref[0, 0, start_prev + r], tile + u)
            w = jnp.where(valid, wt_ref[0, 0, start_prev + r], 0.0)
            toks.append(tok)
            new.append(acc_ref[tile_rows(tok), :] + w * y_ref[static_rows(r), :])
        for u in range(ROW_UNROLL):
            acc_ref[tile_rows(toks[u]), :] = new[u]

    _to_token_tiles(y_ref, 0, ch, expert_mlp(xg))

    @pl.when(e == n_e - 1)
    def _():
        scatter_rows(y_ref, start, n_cur)

    @pl.when(count > ch)
    def _():
        ch2 = xg2_ref.shape[0] // SUBLANES

        def extra(c, _):
            base = start + ch + c * ch2
            n = jnp.minimum(ch2, count - ch - c * ch2)
            gather_rows(xg2_ref, base, n)
            _to_token_tiles(y2_ref, 0, ch2, expert_mlp(_from_token_tiles(xg2_ref, 0, ch2).astype(jnp.bfloat16)))
            scatter_rows(y2_ref, base, n)
            return 0

        lax.fori_loop(0, (count - ch + ch2 - 1) // ch2, extra, 0)


def _moe(offs, tok_sorted, wt_sorted, h2_tiles, w1g, w1l, b1g, b1l, w2, b2, tile):
    n_tok = h2_tiles.shape[0] // SUBLANES
    n_e, d, d_e = w1g.shape
    n_t = n_tok // tile
    a = tok_sorted.shape[-1]
    assert d == SUBLANES * LANES and MOE_CHUNK % ROW_UNROLL == 0
    smem = lambda: pl.BlockSpec((1, 1, a), lambda ti, e, o: (ti, 0, 0), memory_space=pltpu.SMEM)
    ex = lambda shape: pl.BlockSpec((1,) + shape, lambda ti, e, o: (e, 0, 0))
    out_rows = (tile + ROW_UNROLL) * SUBLANES
    rows = pltpu.VMEM((MOE_CHUNK * SUBLANES, LANES), jnp.float32)
    rows2 = pltpu.VMEM((MOE_EXTRA_CHUNK * SUBLANES, LANES), jnp.float32)
    once = pl.Buffered(1)
    grid_spec = pltpu.PrefetchScalarGridSpec(
        num_scalar_prefetch=1,
        grid=(n_t, n_e),
        in_specs=[smem(), smem(),
                  pl.BlockSpec((tile * SUBLANES, LANES), lambda ti, e, o: (ti, 0), pipeline_mode=once),
                  ex((d, d_e)), ex((d, d_e)), ex((1, d_e)), ex((1, d_e)), ex((d_e, d)), ex((1, d))],
        out_specs=pl.BlockSpec((out_rows, LANES), lambda ti, e, o: (ti, 0), pipeline_mode=once),
        scratch_shapes=[rows, rows, rows2, rows2],
    )
    return pl.pallas_call(
        functools.partial(_moe_kernel, tile),
        grid_spec=grid_spec,
        out_shape=jax.ShapeDtypeStruct((n_t * out_rows, LANES), jnp.float32),
        compiler_params=pltpu.CompilerParams(dimension_semantics=("arbitrary", "arbitrary"),
                                             vmem_limit_bytes=MOE_VMEM_LIMIT),
        name="moe",
    )(offs, tok_sorted, wt_sorted, h2_tiles, w1g, w1l, b1g, b1l, w2, b2)


def _final_kernel(sub, x_ref, m_ref, g2_ref, g_ref, o_ref):
    t = x_ref.shape[1]
    row0 = lax.rem(pl.program_id(1), sub) * t
    x = x_ref[0] + g2_ref[0] * _from_token_tiles(m_ref, row0, t)
    ms = jnp.mean(x * x, axis=-1, keepdims=True)
    o_ref[0] = x * lax.rsqrt(ms + RMS_EPS) * g_ref[...]


def _final(x1, moe_tiles, g2, final_g, moe_tile):
    b, s, d = x1.shape
    t = min(MERGE_TILE, moe_tile)
    per = s // t
    sub = moe_tile // t
    moe_rows = (moe_tile + ROW_UNROLL) * SUBLANES
    tok = pl.BlockSpec((1, t, d), lambda bi, i: (bi, i, 0))
    return pl.pallas_call(
        functools.partial(_final_kernel, sub),
        grid=(b, per),
        in_specs=[tok, pl.BlockSpec((moe_rows, LANES), lambda bi, i: (bi * (per // sub) + i // sub, 0)),
                  pl.BlockSpec((1, 1, d), lambda bi, i: (bi, 0, 0)),
                  pl.BlockSpec((1, d), lambda bi, i: (0, 0))],
        out_specs=tok,
        out_shape=jax.ShapeDtypeStruct((b, s, d), jnp.float32),
        compiler_params=_params(("arbitrary", "arbitrary")),
        name="final_norm",
    )(x1, moe_tiles, g2, final_g)


def _rope_tables(s, n_ctx):
    inv = ROPE_THETA ** (-(jnp.arange(ROPE_QUARTER, dtype=jnp.float32) * 2.0 / ROPE_HALF))
    pos = jnp.arange(s)
    ang_r = (pos // GRID_W).astype(jnp.float32)[None, :] * inv[:, None]
    ang_c = (pos % GRID_W).astype(jnp.float32)[None, :] * inv[:, None]
    ang = jnp.concatenate([ang_r, ang_c], axis=0)
    cos = jnp.concatenate([jnp.cos(ang), jnp.ones((ROPE_HALF, n_ctx), jnp.float32)], axis=1)
    sin = jnp.concatenate([jnp.sin(ang), jnp.zeros((ROPE_HALF, n_ctx), jnp.float32)], axis=1)
    return cos, sin


def _route_lists(top_idx, top_w, tile):
    b, k, s = top_idx.shape
    n_t = b * s // tile
    per = s // tile
    e_flat = top_idx.reshape(b, k, per, tile).transpose(0, 2, 1, 3).reshape(n_t, k * tile)
    w_flat = top_w.reshape(b, k, per, tile).transpose(0, 2, 1, 3).reshape(n_t, k * tile)
    tok = jnp.tile(jnp.arange(tile, dtype=jnp.int32), k)[None, :].repeat(n_t, axis=0)
    e_sorted, tok_sorted, w_sorted = lax.sort((e_flat, tok, w_flat), dimension=1, num_keys=2)
    counts = jnp.sum(e_flat[:, :, None] == jnp.arange(N_EXPERTS, dtype=jnp.int32)[None, None, :], axis=1,
                     dtype=jnp.int32)
    offs = jnp.concatenate([jnp.zeros((n_t, 1), jnp.int32), jnp.cumsum(counts, axis=1, dtype=jnp.int32)], axis=1)
    pad = ((0, 0), (0, MOE_CHUNK))
    return offs.reshape(-1), jnp.pad(tok_sorted, pad)[:, None, :], jnp.pad(w_sorted, pad)[:, None, :]


def kernel(x, c, ctx, c_ctx, w_mod, b_mod, norm1_g, norm2_g, w_in, q_norm_g, k_norm_g, sink, w_br_a, w_br_b,
           w_o, w_router, b_router, w_e1, b_e1, w_e2, b_e2, final_g):
    b, s, d = x.shape
    n_ctx = ctx.shape[1]
    assert w_mod.shape[0] == 1
    bf = jnp.bfloat16
    cos, sin = _rope_tables(s, n_ctx)
    t = TOKEN_TILE
    moe_tile = min(MOE_TILE, s)
    for l in range(1):
        rows = ((b + 1 + 7) // 8) * 8
        cond = jnp.zeros((rows, d), jnp.float32).at[:b].set(c).at[b].set(c_ctx)
        mods = _adaln(cond, w_mod[l], b_mod[l])
        sh1, sc1, g1, sh2, sc2, g2 = jnp.split(mods, 6, axis=-1)
        mod1 = jnp.stack([jnp.stack([sh1[:b], sc1[:b]], axis=1),
                          jnp.broadcast_to(jnp.stack([sh1[b], sc1[b]], axis=0)[None], (b, 2, d))], axis=1)
        mod2 = jnp.stack([sh2[:b], sc2[:b]], axis=1)

        w = w_in[l]
        kv = w[:, :4 * KV_W]
        wk_t = jnp.concatenate([kv[:, 0:KV_W], kv[:, 2 * KV_W:3 * KV_W]], axis=1).T.astype(bf)
        wv_t = jnp.concatenate([kv[:, KV_W:2 * KV_W], kv[:, 3 * KV_W:4 * KV_W]], axis=1).T.astype(bf)
        wq_t = w[:, 4 * KV_W:4 * KV_W + Q_A + Q_B].T.astype(bf)
        wg_t = w[:, 4 * KV_W + Q_A + Q_B:].T.astype(bf)
        qg = jnp.broadcast_to((q_norm_g[l] * (ATTN_SCALE * LOG2E))[:, None], (HEAD_DIM, t))
        kg = jnp.broadcast_to(k_norm_g[l][:, None], (HEAD_DIM, t))

        qa_t, qb_t, ka, kb, va_t, vb_t, gate_t = _in_proj(
            x, ctx, mod1, norm1_g[l][None], wq_t, wk_t, wv_t, wg_t, qg, kg, cos, sin)

        ya_t = _attention(False, 128, 2, sink[l], qa_t, ka, va_t)
        yb_t = _attention(True, 256, 1, sink[l], qb_t, kb, vb_t)

        x1, h2, top_idx, top_w = _merge(
            x, ya_t, yb_t, gate_t, w_br_a[l].T.astype(bf), w_br_b[l].T.astype(bf), w_o[l].astype(bf),
            g1[:b, None, :], mod2, norm2_g[l][None], w_router[l].T, b_router[l][:, None])

        offs, tok_sorted, w_sorted = _route_lists(top_idx, top_w, moe_tile)
        w1g, w1l = _w1_split(w_e1[l])
        moe_tiles = _moe(offs, tok_sorted, w_sorted, h2, w1g, w1l,
                         b_e1[l][:, None, 0::2], b_e1[l][:, None, 1::2],
                         w_e2[l].astype(bf), b_e2[l][:, None, :], moe_tile)
        out = _final(x1, moe_tiles, g2[:b, None, :], final_g[None], moe_tile)
    return out
```

```python
import functools

import numpy as np
import jax
import jax.numpy as jnp
from jax import lax
from jax.experimental import pallas as pl
from jax.experimental.pallas import tpu as pltpu

HEAD_DIM = 64
KV_HEADS = 2
A_HEADS = 8
B_HEADS = 8
GROUP = A_HEADS // KV_HEADS
GRID_W = 64
WINDOW = 128
ROPE_HALF = HEAD_DIM // 2
ROPE_QUARTER = ROPE_HALF // 2
ROPE_THETA = 10000.0
N_EXPERTS = 32
TOP_K = 4
SWIGLU_ALPHA = 1.702
SWIGLU_LIMIT = 7.0
RMS_EPS = 1e-6
ATTN_SCALE = HEAD_DIM ** -0.5
LOG2E = 1.4426950408889634
NEG_INF = -1e30
KV_W = KV_HEADS * HEAD_DIM
Q_A = A_HEADS * HEAD_DIM
Q_B = B_HEADS * HEAD_DIM
V_ROWS = HEAD_DIM + 16

SUBLANES = 8
LANES = 128
TOKEN_TILE = 256
MERGE_TILE = 512
MOE_TILE = 4096
MOE_CHUNK = 256
MOE_VMEM_LIMIT = 60 * 1024 * 1024
ROW_UNROLL = 8
SPLIT_BLOCK = 256
VMEM_LIMIT = 56 * 1024 * 1024

_NT = (((1,), (1,)), ((), ()))


def _dot_nt(a, b, precision=None):
    return lax.dot_general(a, b, _NT, preferred_element_type=jnp.float32, precision=precision)


def _dot(a, b, precision=None):
    return jnp.dot(a, b, preferred_element_type=jnp.float32, precision=precision)


def _params(sem):
    return pltpu.CompilerParams(dimension_semantics=sem, vmem_limit_bytes=VMEM_LIMIT)


def _adaln_kernel(cond_ref, w_ref, b_ref, o_ref):
    cond = cond_ref[...]
    act = cond * jax.nn.sigmoid(cond)
    o_ref[...] = _dot(act, w_ref[...], precision=lax.Precision.HIGHEST) + b_ref[...]


def _adaln(cond, w, b):
    rows, d = cond.shape
    n = w.shape[1]
    tn = 1024
    return pl.pallas_call(
        _adaln_kernel,
        grid=(n // tn,),
        in_specs=[pl.BlockSpec((rows, d), lambda j: (0, 0)),
                  pl.BlockSpec((d, tn), lambda j: (0, j)),
                  pl.BlockSpec((1, tn), lambda j: (0, j))],
        out_specs=pl.BlockSpec((rows, tn), lambda j: (0, j)),
        out_shape=jax.ShapeDtypeStruct((rows, n), jnp.float32),
        compiler_params=_params(("arbitrary",)),
        name="adaln",
    )(cond, w, b.reshape(1, n))


def _rope_t(xh, cos, sin):
    q = ROPE_QUARTER
    a0, a1, b0, b1 = xh[0:q], xh[q:2 * q], xh[2 * q:3 * q], xh[3 * q:4 * q]
    cr, cc = cos[0:q], cos[q:2 * q]
    sr, sc = sin[0:q], sin[q:2 * q]
    return jnp.concatenate([a0 * cr - a1 * sr, a1 * cr + a0 * sr,
                            b0 * cc - b1 * sc, b1 * cc + b0 * sc], axis=0)


def _head_norm_t(xh, g):
    ms = jnp.mean(xh * xh, axis=0, keepdims=True)
    return xh * lax.rsqrt(ms + RMS_EPS) * g


def _mod_norm(x, g, shift, scale):
    ms = jnp.mean(x * x, axis=-1, keepdims=True)
    return x * lax.rsqrt(ms + RMS_EPS) * g * (1.0 + scale) + shift


def _in_proj_kernel(n_lat, x_ref, ctx_ref, mod_ref, g_ref, wq_ref, wk_ref, wv_ref, wg_ref,
                    qg_ref, kg_ref, cos_ref, sin_ref,
                    qa_ref, qb_ref, ka_ref, kb_ref, va_ref, vb_ref, gate_ref, h_ref):
    i = pl.program_id(1)
    shift = mod_ref[0, 0, 0:1, :]
    scale = mod_ref[0, 0, 1:2, :]

    @pl.when(i < n_lat)
    def _():
        h_ref[...] = _mod_norm(x_ref[0], g_ref[...], shift, scale).astype(jnp.bfloat16)

    @pl.when(i >= n_lat)
    def _():
        h_ref[...] = _mod_norm(ctx_ref[0], g_ref[...], shift, scale).astype(jnp.bfloat16)

    h = h_ref[...]
    cos = cos_ref[...]
    sin = sin_ref[...]
    t = h.shape[0]

    kt = _dot_nt(wk_ref[...], h)
    kg = kg_ref[...]
    ka = [_rope_t(_head_norm_t(kt[j * HEAD_DIM:(j + 1) * HEAD_DIM], kg), cos, sin) for j in range(KV_HEADS)]
    kb = [_rope_t(kt[KV_W + j * HEAD_DIM:KV_W + (j + 1) * HEAD_DIM], cos, sin) for j in range(KV_HEADS)]
    ka_ref[0, 0] = jnp.concatenate(ka, axis=0).T.astype(jnp.bfloat16)
    kb_ref[0, 0] = jnp.concatenate(kb, axis=0).T.astype(jnp.bfloat16)
    vt = _dot_nt(wv_ref[...], h)
    ones = jnp.ones((V_ROWS - HEAD_DIM, t), jnp.float32)
    for br, v_ref in enumerate((va_ref, vb_ref)):
        rows = []
        for j in range(KV_HEADS):
            rows += [vt[br * KV_W + j * HEAD_DIM:br * KV_W + (j + 1) * HEAD_DIM], ones]
        v_ref[0, 0] = jnp.concatenate(rows, axis=0).astype(jnp.bfloat16)

    @pl.when(i < n_lat)
    def _():
        qg = qg_ref[...]
        for half in range(2):
            qt = _dot_nt(wq_ref[half * Q_A:(half + 1) * Q_A, :], h)
            for hd in range(A_HEADS):
                xh = qt[hd * HEAD_DIM:(hd + 1) * HEAD_DIM]
                if half == 0:
                    out = _rope_t(_head_norm_t(xh, qg), cos, sin)
                    qa_ref[0, hd * HEAD_DIM:(hd + 1) * HEAD_DIM, :] = out.astype(jnp.bfloat16)
                else:
                    out = _rope_t(xh * (ATTN_SCALE * LOG2E), cos, sin)
                    qb_ref[0, hd * HEAD_DIM:(hd + 1) * HEAD_DIM, :] = out.astype(jnp.bfloat16)
        rows = 512
        for c in range(wg_ref.shape[0] // rows):
            gt = _dot_nt(wg_ref[c * rows:(c + 1) * rows, :], h)
            gate_ref[0, c * rows:(c + 1) * rows, :] = jax.nn.sigmoid(gt).astype(jnp.bfloat16)


def _in_proj(x, ctx, mod, norm_g, wq_t, wk_t, wv_t, wg_t, qg, kg, cos, sin):
    b, s, d = x.shape
    n_ctx = ctx.shape[1]
    t = TOKEN_TILE
    n_lat = s // t
    n_tiles = n_lat + n_ctx // t
    last = n_lat - 1
    full = lambda shape: pl.BlockSpec(shape, lambda bi, i: (0,) * len(shape))
    q_spec = pl.BlockSpec((1, Q_A, t), lambda bi, i: (bi, 0, jnp.minimum(i, last)))
    k_spec = pl.BlockSpec((1, 1, t, KV_W), lambda bi, i: (bi, i, 0, 0))
    v_spec = pl.BlockSpec((1, 1, KV_HEADS * V_ROWS, t), lambda bi, i: (bi, i, 0, 0))
    bf = jnp.bfloat16
    return pl.pallas_call(
        functools.partial(_in_proj_kernel, n_lat),
        grid=(b, n_tiles),
        in_specs=[pl.BlockSpec((1, t, d), lambda bi, i: (bi, jnp.minimum(i, last), 0)),
                  pl.BlockSpec((1, t, d), lambda bi, i: (bi, jnp.maximum(i - n_lat, 0), 0)),
                  pl.BlockSpec((1, 1, 2, d), lambda bi, i: (bi, i // n_lat, 0, 0)),
                  full((1, d)), full(wq_t.shape), full(wk_t.shape), full(wv_t.shape), full(wg_t.shape),
                  full((HEAD_DIM, t)), full((HEAD_DIM, t)),
                  pl.BlockSpec((ROPE_HALF, t), lambda bi, i: (0, i)),
                  pl.BlockSpec((ROPE_HALF, t), lambda bi, i: (0, i))],
        out_specs=[q_spec, q_spec, k_spec, k_spec, v_spec, v_spec,
                   pl.BlockSpec((1, 2 * d, t), lambda bi, i: (bi, 0, jnp.minimum(i, last)))],
        out_shape=[jax.ShapeDtypeStruct((b, Q_A, s), bf), jax.ShapeDtypeStruct((b, Q_B, s), bf),
                   jax.ShapeDtypeStruct((b, n_tiles, t, KV_W), bf), jax.ShapeDtypeStruct((b, n_tiles, t, KV_W), bf),
                   jax.ShapeDtypeStruct((b, n_tiles, KV_HEADS * V_ROWS, t), bf),
                   jax.ShapeDtypeStruct((b, n_tiles, KV_HEADS * V_ROWS, t), bf),
                   jax.ShapeDtypeStruct((b, 2 * d, s), bf)],
        scratch_shapes=[pltpu.VMEM((t, d), bf)],
        compiler_params=_params(("arbitrary", "arbitrary")),
        name="in_proj",
    )(x, ctx, mod, norm_g, wq_t, wk_t, wv_t, wg_t, qg, kg, cos, sin)


def _attn_kernel(windowed, tq, n_lat, n_tiles, unroll, sink_ref, q_ref, k_ref, v_ref, y_ref, s_ref):
    qi = pl.program_id(1)
    t = TOKEN_TILE
    n = GROUP * tq
    heads = [[kvh * GROUP + g for g in range(GROUP)] for kvh in range(KV_HEADS)]
    qpad = []
    for kvh in range(KV_HEADS):
        qcat = jnp.concatenate([q_ref[0, hd * HEAD_DIM:(hd + 1) * HEAD_DIM, :] for hd in heads[kvh]], axis=1)
        zeros = jnp.zeros_like(qcat)
        qpad.append(jnp.concatenate([qcat, zeros] if kvh == 0 else [zeros, qcat], axis=0))

    def scores(slot, kb, m8, lo=0, rows=t, mask=None):
        kblk = k_ref[0, kb, lo:lo + rows, :]
        out = []
        for kvh in range(KV_HEADS):
            s = _dot(kblk, qpad[kvh])
            if mask is not None:
                s = jnp.where(mask, s, NEG_INF)
            s_ref[kvh, slot, 0:rows, :] = s
            out.append(jnp.maximum(m8[kvh], jnp.max(s.reshape(rows // 8, 8, n), axis=0)))
        return tuple(out)

    def weigh(slot, kb, m, accs, lo=0, rows=t):
        out = []
        for kvh in range(KV_HEADS):
            p = jnp.exp2(s_ref[kvh, slot, 0:rows, :] - m[kvh]).astype(jnp.bfloat16)
            vblk = v_ref[0, kb, kvh * V_ROWS:(kvh + 1) * V_ROWS, lo:lo + rows]
            out.append(accs[kvh] + _dot(vblk, p))
        return tuple(out)

    acc0 = jnp.zeros((V_ROWS, n), jnp.float32)
    if windowed:
        sink = [jnp.concatenate([jnp.full((1, tq), sink_ref[hd] * LOG2E, jnp.float32) for hd in heads[kvh]], axis=1)
                for kvh in range(KV_HEADS)]
        assert tq == t and WINDOW * 2 == t
        c_half = lax.rem(lax.broadcasted_iota(jnp.int32, (WINDOW, n), 1), tq)
        r_half = lax.broadcasted_iota(jnp.int32, (WINDOW, n), 0)
        c_full = lax.rem(lax.broadcasted_iota(jnp.int32, (t, n), 1), tq)
        r_full = lax.broadcasted_iota(jnp.int32, (t, n), 0)
        off_prev = jnp.where(qi >= 1, 0, 2 * t)
        off_next = jnp.where(qi + 1 < n_lat, 0, 2 * t)
        pieces = [
            (jnp.maximum(qi - 1, 0), WINDOW, WINDOW, r_half >= c_half + off_prev),
            (qi, 0, t, jnp.abs(r_full - c_full) <= WINDOW),
            (jnp.minimum(qi + 1, n_lat - 1), 0, WINDOW, c_half - r_half >= WINDOW + off_next),
        ] + [(kb, 0, t, None) for kb in range(n_lat, n_tiles)]
        m8 = tuple(jnp.broadcast_to(sink[kvh], (8, n)) for kvh in range(KV_HEADS))
        for slot, (kb, lo, rows, ok) in enumerate(pieces):
            m8 = scores(slot, kb, m8, lo, rows, ok)
        m = [jnp.max(m8[kvh], axis=0, keepdims=True) for kvh in range(KV_HEADS)]
        row = lax.broadcasted_iota(jnp.int32, (V_ROWS, n), 0)
        accs = tuple(jnp.where(row == HEAD_DIM, jnp.exp2(sink[kvh] - m[kvh]), 0.0) for kvh in range(KV_HEADS))
        for slot, (kb, lo, rows, _) in enumerate(pieces):
            accs = weigh(slot, kb, m, accs, lo, rows)
    else:
        m8 = tuple(jnp.full((8, n), -jnp.inf, jnp.float32) for _ in range(KV_HEADS))
        m8 = lax.fori_loop(0, n_tiles, lambda kb, c: scores(kb, kb, c), m8, unroll=unroll)
        m = [jnp.max(m8[kvh], axis=0, keepdims=True) for kvh in range(KV_HEADS)]
        accs = lax.fori_loop(0, n_tiles, lambda kb, c: weigh(kb, kb, m, c), (acc0, acc0), unroll=unroll)
    for kvh in range(KV_HEADS):
        acc = accs[kvh]
        out = acc[0:HEAD_DIM] / acc[HEAD_DIM:HEAD_DIM + 1]
        for g, hd in enumerate(heads[kvh]):
            y_ref[0, hd * HEAD_DIM:(hd + 1) * HEAD_DIM, :] = out[:, g * tq:(g + 1) * tq].astype(jnp.bfloat16)


def _attention(windowed, tq, unroll, sink, q_t, k, v_t):
    b, hq, s = q_t.shape
    n_tiles = k.shape[1]
    n_lat = s // TOKEN_TILE
    slots = tq // TOKEN_TILE + 2 + (n_tiles - n_lat) if windowed else n_tiles
    grid_spec = pltpu.PrefetchScalarGridSpec(
        num_scalar_prefetch=1,
        grid=(b, s // tq),
        in_specs=[pl.BlockSpec((1, hq, tq), lambda bi, qi, sk: (bi, 0, qi)),
                  pl.BlockSpec((1,) + k.shape[1:], lambda bi, qi, sk: (bi, 0, 0, 0)),
                  pl.BlockSpec((1,) + v_t.shape[1:], lambda bi, qi, sk: (bi, 0, 0, 0))],
        out_specs=pl.BlockSpec((1, hq, tq), lambda bi, qi, sk: (bi, 0, qi)),
        scratch_shapes=[pltpu.VMEM((KV_HEADS, slots, TOKEN_TILE, GROUP * tq), jnp.float32)],
    )
    return pl.pallas_call(
        functools.partial(_attn_kernel, windowed, tq, n_lat, n_tiles, unroll),
        grid_spec=grid_spec,
        out_shape=jax.ShapeDtypeStruct((b, hq, s), jnp.bfloat16),
        compiler_params=_params(("arbitrary", "arbitrary")),
        name="attn_window" if windowed else "attn_global",
    )(sink, q_t, k, v_t)


def _to_token_tiles(ref, row0, rows, val):
    for j in range(SUBLANES):
        ref[pl.ds(row0 * SUBLANES + j, rows, stride=SUBLANES), :] = val[:, j * LANES:(j + 1) * LANES]


def _from_token_tiles(ref, row0, rows):
    return jnp.concatenate([ref[pl.ds(row0 * SUBLANES + j, rows, stride=SUBLANES), :] for j in range(SUBLANES)],
                           axis=1)


def _merge_kernel(x_ref, ya_ref, yb_ref, gate_ref, wa_ref, wb_ref, wo_ref, g1_ref, mod_ref, g2_ref,
                  wr_ref, br_ref, x1_ref, h2_ref, idx_ref, wt_ref):
    d = x_ref.shape[-1]
    t = TOKEN_TILE
    for sub in range(x_ref.shape[1] // t):
        tok = slice(sub * t, (sub + 1) * t)
        za = _dot(wa_ref[...], ya_ref[0, :, tok])
        zb = _dot(wb_ref[...], yb_ref[0, :, tok])
        mt = (gate_ref[0, 0:d, tok].astype(jnp.float32) * za
              + gate_ref[0, d:2 * d, tok].astype(jnp.float32) * zb)
        o = _dot(mt.T.astype(jnp.bfloat16), wo_ref[...])
        x1 = x_ref[0, tok, :] + g1_ref[0] * o
        x1_ref[0, tok, :] = x1
        h2 = _mod_norm(x1, g2_ref[...], mod_ref[0, 0:1, :], mod_ref[0, 1:2, :])
        _to_token_tiles(h2_ref, sub * t, t, h2)
        logits = _dot_nt(wr_ref[...], h2, precision=lax.Precision.HIGHEST) + br_ref[...]
        n_e = logits.shape[0]
        row = lax.broadcasted_iota(jnp.int32, logits.shape, 0)
        work = logits
        top_v, top_i = [], []
        for _ in range(TOP_K):
            mk = jnp.max(work, axis=0, keepdims=True)
            ik = jnp.min(jnp.where(work == mk, row, n_e), axis=0, keepdims=True)
            top_v.append(mk)
            top_i.append(ik)
            work = jnp.where(row == ik, -jnp.inf, work)
        ex = [jnp.exp(v - top_v[0]) for v in top_v]
        den = ex[0] + ex[1] + ex[2] + ex[3]
        idx_ref[0, :, tok] = jnp.concatenate(top_i, axis=0)
        wt_ref[0, :, tok] = jnp.concatenate([e / den for e in ex], axis=0)


def _merge(x, ya_t, yb_t, gate_t, wa_t, wb_t, wo, g1, mod2, norm2_g, wr_t, br):
    b, s, d = x.shape
    t = MERGE_TILE
    per = s // t
    assert d == SUBLANES * LANES
    full = lambda shape: pl.BlockSpec(shape, lambda bi, i: (0,) * len(shape))
    tok = pl.BlockSpec((1, t, d), lambda bi, i: (bi, i, 0))
    col = lambda rows: pl.BlockSpec((1, rows, t), lambda bi, i: (bi, 0, i))
    return pl.pallas_call(
        _merge_kernel,
        grid=(b, per),
        in_specs=[tok, col(Q_A), col(Q_B), col(2 * d),
                  full(wa_t.shape), full(wb_t.shape), full(wo.shape),
                  pl.BlockSpec((1, 1, d), lambda bi, i: (bi, 0, 0)),
                  pl.BlockSpec((1, 2, d), lambda bi, i: (bi, 0, 0)),
                  full((1, d)), full(wr_t.shape), full(br.shape)],
        out_specs=[tok, pl.BlockSpec((t * SUBLANES, LANES), lambda bi, i: (bi * per + i, 0)),
                   col(TOP_K), col(TOP_K)],
        out_shape=[jax.ShapeDtypeStruct((b, s, d), jnp.float32),
                   jax.ShapeDtypeStruct((b * s * SUBLANES, LANES), jnp.float32),
                   jax.ShapeDtypeStruct((b, TOP_K, s), jnp.int32), jax.ShapeDtypeStruct((b, TOP_K, s), jnp.float32)],
        compiler_params=_params(("arbitrary", "arbitrary")),
        name="merge_router",
    )(x, ya_t, yb_t, gate_t, wa_t, wb_t, wo, g1, mod2, norm2_g, wr_t, br)


def _w1_split_kernel(w_ref, p_ref, g_ref, l_ref):
    half = SPLIT_BLOCK // 2
    for jb in range(w_ref.shape[-1] // SPLIT_BLOCK):
        blk = w_ref[0, :, jb * SPLIT_BLOCK:(jb + 1) * SPLIT_BLOCK].astype(jnp.bfloat16)
        r = _dot(blk, p_ref[...])
        g_ref[0, :, jb * half:(jb + 1) * half] = r[:, :half].astype(jnp.bfloat16)
        l_ref[0, :, jb * half:(jb + 1) * half] = r[:, half:].astype(jnp.bfloat16)


def _w1_split(w1):
    n_e, d, d2 = w1.shape
    rows = 512
    half = SPLIT_BLOCK // 2
    perm = np.zeros((SPLIT_BLOCK, SPLIT_BLOCK), np.float32)
    for j in range(half):
        perm[2 * j, j] = 1.0
        perm[2 * j + 1, half + j] = 1.0
    out = pl.BlockSpec((1, rows, d2 // 2), lambda e, r: (e, r, 0))
    return pl.pallas_call(
        _w1_split_kernel,
        grid=(n_e, d // rows),
        in_specs=[pl.BlockSpec((1, rows, d2), lambda e, r: (e, r, 0)),
                  pl.BlockSpec((SPLIT_BLOCK, SPLIT_BLOCK), lambda e, r: (0, 0))],
        out_specs=[out, out],
        out_shape=[jax.ShapeDtypeStruct((n_e, d, d2 // 2), jnp.bfloat16)] * 2,
        compiler_params=_params(("arbitrary", "arbitrary")),
        name="w1_split",
    )(w1, jnp.asarray(perm, jnp.bfloat16))


def _moe_kernel(tile, offs_ref, tok_ref, wt_ref, h_ref, w1g_ref, w1l_ref, b1g_ref, b1l_ref, w2_ref, b2_ref,
                acc_ref, xg_ref, y_ref, pend_ref):
    ti = pl.program_id(0)
    e = pl.program_id(1)
    n_e = pl.num_programs(1)
    ch = xg_ref.shape[0] // SUBLANES
    off = ti * (n_e + 1)

    def tile_rows(r):
        return pl.ds(pl.multiple_of(r * SUBLANES, SUBLANES), SUBLANES)

    def static_rows(r):
        return slice(r * SUBLANES, (r + 1) * SUBLANES)

    def expert_mlp(xg):
        glu = jnp.minimum(_dot(xg, w1g_ref[0]) + b1g_ref[0], SWIGLU_LIMIT)
        lin = jnp.clip(_dot(xg, w1l_ref[0]) + b1l_ref[0], -SWIGLU_LIMIT, SWIGLU_LIMIT)
        act = glu * jax.nn.sigmoid(SWIGLU_ALPHA * glu) * (lin + 1.0)
        return _dot(act.astype(jnp.bfloat16), w2_ref[0]) + b2_ref[0]

    def gather_rows(dst_ref, base, n):
        def group(i, _):
            r0 = pl.multiple_of(i * ROW_UNROLL, ROW_UNROLL)
            for u in range(ROW_UNROLL):
                dst_ref[tile_rows(r0 + u), :] = h_ref[tile_rows(tok_ref[0, 0, base + r0 + u]), :]
            return 0

        lax.fori_loop(0, (n + ROW_UNROLL - 1) // ROW_UNROLL, group, 0)

    def scatter_rows(src_ref, base, n):
        def group(i, _):
            r0 = pl.multiple_of(i * ROW_UNROLL, ROW_UNROLL)
            toks = [tok_ref[0, 0, base + r0 + u] for u in range(ROW_UNROLL)]
            new = [acc_ref[tile_rows(toks[u]), :] + wt_ref[0, 0, base + r0 + u] * src_ref[tile_rows(r0 + u), :]
                   for u in range(ROW_UNROLL)]
            for u in range(ROW_UNROLL):
                acc_ref[tile_rows(toks[u]), :] = new[u]
            return 0

        full_groups = n // ROW_UNROLL
        lax.fori_loop(0, full_groups, group, 0)

        def row(r, _):
            tok = tok_ref[0, 0, base + r]
            acc_ref[tile_rows(tok), :] = acc_ref[tile_rows(tok), :] + wt_ref[0, 0, base + r] * src_ref[tile_rows(r), :]
            return 0

        lax.fori_loop(full_groups * ROW_UNROLL, n, row, 0)

    start = offs_ref[off + e]
    count = offs_ref[off + e + 1] - start
    n_blocks = jnp.maximum((count + ch - 1) // ch, 1)
    start_next = offs_ref[off + jnp.minimum(e + 1, n_e - 1)]

    @pl.when((ti == 0) & (e == 0))
    def _():
        for ref in (xg_ref, y_ref):
            ref[...] = jnp.zeros_like(ref)

    @pl.when(e == 0)
    def _():
        acc_ref[...] = jnp.zeros_like(acc_ref)
        pend_ref[0] = 0
        pend_ref[1] = 0
        gather_rows(xg_ref, start, jnp.minimum(count, ch))

    def block(k, pending):
        base_prev, n_prev = pending
        xg = _from_token_tiles(xg_ref, 0, ch).astype(jnp.bfloat16)

        base_next = jnp.where(k + 1 < n_blocks, start + (k + 1) * ch, start_next)
        for r in range(ch):
            xg_ref[static_rows(r), :] = h_ref[tile_rows(tok_ref[0, 0, base_next + r]), :]

        for g in range(ch // ROW_UNROLL):
            toks, new = [], []
            for u in range(ROW_UNROLL):
                r = g * ROW_UNROLL + u
                valid = r < n_prev
                tok = jnp.where(valid, tok_ref[0, 0, base_prev + r], tile + u)
                w = jnp.where(valid, wt_ref[0, 0, base_prev + r], 0.0)
                toks.append(tok)
                new.append(acc_ref[tile_rows(tok), :] + w * y_ref[static_rows(r), :])
            for u in range(ROW_UNROLL):
                acc_ref[tile_rows(toks[u]), :] = new[u]

        _to_token_tiles(y_ref, 0, ch, expert_mlp(xg))
        return start + k * ch, jnp.minimum(ch, count - k * ch)

    base_prev, n_prev = lax.fori_loop(0, n_blocks, block, (pend_ref[0], pend_ref[1]))
    pend_ref[0] = base_prev
    pend_ref[1] = n_prev

    @pl.when(e == n_e - 1)
    def _():
        scatter_rows(y_ref, base_prev, n_prev)


def _moe(offs, tok_sorted, wt_sorted, h2_tiles, w1g, w1l, b1g, b1l, w2, b2, tile):
    n_tok = h2_tiles.shape[0] // SUBLANES
    n_e, d, d_e = w1g.shape
    n_t = n_tok // tile
    a = tok_sorted.shape[-1]
    assert d == SUBLANES * LANES and MOE_CHUNK % ROW_UNROLL == 0
    smem = lambda: pl.BlockSpec((1, 1, a), lambda ti, e, o: (ti, 0, 0), memory_space=pltpu.SMEM)
    ex = lambda shape: pl.BlockSpec((1,) + shape, lambda ti, e, o: (e, 0, 0))
    out_rows = (tile + ROW_UNROLL) * SUBLANES
    rows = pltpu.VMEM((MOE_CHUNK * SUBLANES, LANES), jnp.float32)
    once = pl.Buffered(1)
    grid_spec = pltpu.PrefetchScalarGridSpec(
        num_scalar_prefetch=1,
        grid=(n_t, n_e),
        in_specs=[smem(), smem(),
                  pl.BlockSpec((tile * SUBLANES, LANES), lambda ti, e, o: (ti, 0), pipeline_mode=once),
                  ex((d, d_e)), ex((d, d_e)), ex((1, d_e)), ex((1, d_e)), ex((d_e, d)), ex((1, d))],
        out_specs=pl.BlockSpec((out_rows, LANES), lambda ti, e, o: (ti, 0), pipeline_mode=once),
        scratch_shapes=[rows, rows, pltpu.SMEM((2,), jnp.int32)],
    )
    return pl.pallas_call(
        functools.partial(_moe_kernel, tile),
        grid_spec=grid_spec,
        out_shape=jax.ShapeDtypeStruct((n_t * out_rows, LANES), jnp.float32),
        compiler_params=pltpu.CompilerParams(dimension_semantics=("arbitrary", "arbitrary"),
                                             vmem_limit_bytes=MOE_VMEM_LIMIT),
        name="moe",
    )(offs, tok_sorted, wt_sorted, h2_tiles, w1g, w1l, b1g, b1l, w2, b2)


def _final_kernel(sub, x_ref, m_ref, g2_ref, g_ref, o_ref):
    t = x_ref.shape[1]
    row0 = lax.rem(pl.program_id(1), sub) * t
    x = x_ref[0] + g2_ref[0] * _from_token_tiles(m_ref, row0, t)
    ms = jnp.mean(x * x, axis=-1, keepdims=True)
    o_ref[0] = x * lax.rsqrt(ms + RMS_EPS) * g_ref[...]


def _final(x1, moe_tiles, g2, final_g, moe_tile):
    b, s, d = x1.shape
    t = min(MERGE_TILE, moe_tile)
    per = s // t
    sub = moe_tile // t
    moe_rows = (moe_tile + ROW_UNROLL) * SUBLANES
    tok = pl.BlockSpec((1, t, d), lambda bi, i: (bi, i, 0))
    return pl.pallas_call(
        functools.partial(_final_kernel, sub),
        grid=(b, per),
        in_specs=[tok, pl.BlockSpec((moe_rows, LANES), lambda bi, i: (bi * (per // sub) + i // sub, 0)),
                  pl.BlockSpec((1, 1, d), lambda bi, i: (bi, 0, 0)),
                  pl.BlockSpec((1, d), lambda bi, i: (0, 0))],
        out_specs=tok,
        out_shape=jax.ShapeDtypeStruct((b, s, d), jnp.float32),
        compiler_params=_params(("arbitrary", "arbitrary")),
        name="final_norm",
    )(x1, moe_tiles, g2, final_g)


def _rope_tables(s, n_ctx):
    inv = ROPE_THETA ** (-(jnp.arange(ROPE_QUARTER, dtype=jnp.float32) * 2.0 / ROPE_HALF))
    pos = jnp.arange(s)
    ang_r = (pos // GRID_W).astype(jnp.float32)[None, :] * inv[:, None]
    ang_c = (pos % GRID_W).astype(jnp.float32)[None, :] * inv[:, None]
    ang = jnp.concatenate([ang_r, ang_c], axis=0)
    cos = jnp.concatenate([jnp.cos(ang), jnp.ones((ROPE_HALF, n_ctx), jnp.float32)], axis=1)
    sin = jnp.concatenate([jnp.sin(ang), jnp.zeros((ROPE_HALF, n_ctx), jnp.float32)], axis=1)
    return cos, sin


def _route_lists(top_idx, top_w, tile):
    b, k, s = top_idx.shape
    n_t = b * s // tile
    per = s // tile
    e_flat = top_idx.reshape(b, k, per, tile).transpose(0, 2, 1, 3).reshape(n_t, k * tile)
    w_flat = top_w.reshape(b, k, per, tile).transpose(0, 2, 1, 3).reshape(n_t, k * tile)
    tok = jnp.tile(jnp.arange(tile, dtype=jnp.int32), k)[None, :].repeat(n_t, axis=0)
    e_sorted, tok_sorted, w_sorted = lax.sort((e_flat, tok, w_flat), dimension=1, num_keys=2)
    counts = jnp.sum(e_flat[:, :, None] == jnp.arange(N_EXPERTS, dtype=jnp.int32)[None, None, :], axis=1,
                     dtype=jnp.int32)
    offs = jnp.concatenate([jnp.zeros((n_t, 1), jnp.int32), jnp.cumsum(counts, axis=1, dtype=jnp.int32)], axis=1)
    pad = ((0, 0), (0, MOE_CHUNK))
    return offs.reshape(-1), jnp.pad(tok_sorted, pad)[:, None, :], jnp.pad(w_sorted, pad)[:, None, :]


def kernel(x, c, ctx, c_ctx, w_mod, b_mod, norm1_g, norm2_g, w_in, q_norm_g, k_norm_g, sink, w_br_a, w_br_b,
           w_o, w_router, b_router, w_e1, b_e1, w_e2, b_e2, final_g):
    b, s, d = x.shape
    n_ctx = ctx.shape[1]
    assert w_mod.shape[0] == 1
    bf = jnp.bfloat16
    cos, sin = _rope_tables(s, n_ctx)
    t = TOKEN_TILE
    moe_tile = min(MOE_TILE, s)
    for l in range(1):
        rows = ((b + 1 + 7) // 8) * 8
        cond = jnp.zeros((rows, d), jnp.float32).at[:b].set(c).at[b].set(c_ctx)
        mods = _adaln(cond, w_mod[l], b_mod[l])
        sh1, sc1, g1, sh2, sc2, g2 = jnp.split(mods, 6, axis=-1)
        mod1 = jnp.stack([jnp.stack([sh1[:b], sc1[:b]], axis=1),
                          jnp.broadcast_to(jnp.stack([sh1[b], sc1[b]], axis=0)[None], (b, 2, d))], axis=1)
        mod2 = jnp.stack([sh2[:b], sc2[:b]], axis=1)

        w = w_in[l]
        kv = w[:, :4 * KV_W]
        wk_t = jnp.concatenate([kv[:, 0:KV_W], kv[:, 2 * KV_W:3 * KV_W]], axis=1).T.astype(bf)
        wv_t = jnp.concatenate([kv[:, KV_W:2 * KV_W], kv[:, 3 * KV_W:4 * KV_W]], axis=1).T.astype(bf)
        wq_t = w[:, 4 * KV_W:4 * KV_W + Q_A + Q_B].T.astype(bf)
        wg_t = w[:, 4 * KV_W + Q_A + Q_B:].T.astype(bf)
        qg = jnp.broadcast_to((q_norm_g[l] * (ATTN_SCALE * LOG2E))[:, None], (HEAD_DIM, t))
        kg = jnp.broadcast_to(k_norm_g[l][:, None], (HEAD_DIM, t))

        qa_t, qb_t, ka, kb, va_t, vb_t, gate_t = _in_proj(
            x, ctx, mod1, norm1_g[l][None], wq_t, wk_t, wv_t, wg_t, qg, kg, cos, sin)

        ya_t = _attention(False, 128, 2, sink[l], qa_t, ka, va_t)
        yb_t = _attention(True, 256, 1, sink[l], qb_t, kb, vb_t)

        x1, h2, top_idx, top_w = _merge(
            x, ya_t, yb_t, gate_t, w_br_a[l].T.astype(bf), w_br_b[l].T.astype(bf), w_o[l].astype(bf),
            g1[:b, None, :], mod2, norm2_g[l][None], w_router[l].T, b_router[l][:, None])

        offs, tok_sorted, w_sorted = _route_lists(top_idx, top_w, moe_tile)
        w1g, w1l = _w1_split(w_e1[l])
        moe_tiles = _moe(offs, tok_sorted, w_sorted, h2, w1g, w1l,
                         b_e1[l][:, None, 0::2], b_e1[l][:, None, 1::2],
                         w_e2[l].astype(bf), b_e2[l][:, None, :], moe_tile)
        out = _final(x1, moe_tiles, g2[:b, None, :], final_g[None], moe_tile)
    return out
```

```python
import functools

import numpy as np
import jax
import jax.numpy as jnp
from jax import lax
from jax.experimental import pallas as pl
from jax.experimental.pallas import tpu as pltpu

HEAD_DIM = 64
KV_HEADS = 2
A_HEADS = 8
B_HEADS = 8
GROUP = A_HEADS // KV_HEADS
GRID_W = 64
WINDOW = 128
ROPE_HALF = HEAD_DIM // 2
ROPE_QUARTER = ROPE_HALF // 2
ROPE_THETA = 10000.0
N_EXPERTS = 32
TOP_K = 4
SWIGLU_ALPHA = 1.702
SWIGLU_LIMIT = 7.0
RMS_EPS = 1e-6
ATTN_SCALE = HEAD_DIM ** -0.5
LOG2E = 1.4426950408889634
NEG_INF = -1e30
KV_W = KV_HEADS * HEAD_DIM
Q_A = A_HEADS * HEAD_DIM
Q_B = B_HEADS * HEAD_DIM
V_ROWS = HEAD_DIM + 16

SUBLANES = 8
LANES = 128
TOKEN_TILE = 256
MERGE_TILE = 512
MOE_TILE = 4096
MOE_CHUNK = 256
MOE_VMEM_LIMIT = 60 * 1024 * 1024
ROW_UNROLL = 8
SPLIT_BLOCK = 256
VMEM_LIMIT = 56 * 1024 * 1024

_NT = (((1,), (1,)), ((), ()))


def _dot_nt(a, b, precision=None):
    return lax.dot_general(a, b, _NT, preferred_element_type=jnp.float32, precision=precision)


def _dot(a, b, precision=None):
    return jnp.dot(a, b, preferred_element_type=jnp.float32, precision=precision)


def _params(sem):
    return pltpu.CompilerParams(dimension_semantics=sem, vmem_limit_bytes=VMEM_LIMIT)


def _adaln_kernel(cond_ref, w_ref, b_ref, o_ref):
    cond = cond_ref[...]
    act = cond * jax.nn.sigmoid(cond)
    o_ref[...] = _dot(act, w_ref[...], precision=lax.Precision.HIGHEST) + b_ref[...]


def _adaln(cond, w, b):
    rows, d = cond.shape
    n = w.shape[1]
    tn = 1024
    return pl.pallas_call(
        _adaln_kernel,
        grid=(n // tn,),
        in_specs=[pl.BlockSpec((rows, d), lambda j: (0, 0)),
                  pl.BlockSpec((d, tn), lambda j: (0, j)),
                  pl.BlockSpec((1, tn), lambda j: (0, j))],
        out_specs=pl.BlockSpec((rows, tn), lambda j: (0, j)),
        out_shape=jax.ShapeDtypeStruct((rows, n), jnp.float32),
        compiler_params=_params(("arbitrary",)),
        name="adaln",
    )(cond, w, b.reshape(1, n))


def _rope_t(xh, cos, sin):
    q = ROPE_QUARTER
    a0, a1, b0, b1 = xh[0:q], xh[q:2 * q], xh[2 * q:3 * q], xh[3 * q:4 * q]
    cr, cc = cos[0:q], cos[q:2 * q]
    sr, sc = sin[0:q], sin[q:2 * q]
    return jnp.concatenate([a0 * cr - a1 * sr, a1 * cr + a0 * sr,
                            b0 * cc - b1 * sc, b1 * cc + b0 * sc], axis=0)


def _head_norm_t(xh, g):
    ms = jnp.mean(xh * xh, axis=0, keepdims=True)
    return xh * lax.rsqrt(ms + RMS_EPS) * g


def _mod_norm(x, g, shift, scale):
    ms = jnp.mean(x * x, axis=-1, keepdims=True)
    return x * lax.rsqrt(ms + RMS_EPS) * g * (1.0 + scale) + shift


def _in_proj_kernel(n_lat, x_ref, ctx_ref, mod_ref, g_ref, wq_ref, wk_ref, wv_ref, wg_ref,
                    qg_ref, kg_ref, cos_ref, sin_ref,
                    qa_ref, qb_ref, ka_ref, kb_ref, va_ref, vb_ref, gate_ref, h_ref):
    i = pl.program_id(1)
    shift = mod_ref[0, 0, 0:1, :]
    scale = mod_ref[0, 0, 1:2, :]

    @pl.when(i < n_lat)
    def _():
        h_ref[...] = _mod_norm(x_ref[0], g_ref[...], shift, scale).astype(jnp.bfloat16)

    @pl.when(i >= n_lat)
    def _():
        h_ref[...] = _mod_norm(ctx_ref[0], g_ref[...], shift, scale).astype(jnp.bfloat16)

    h = h_ref[...]
    cos = cos_ref[...]
    sin = sin_ref[...]
    t = h.shape[0]

    kt = _dot_nt(wk_ref[...], h)
    kg = kg_ref[...]
    ka = [_rope_t(_head_norm_t(kt[j * HEAD_DIM:(j + 1) * HEAD_DIM], kg), cos, sin) for j in range(KV_HEADS)]
    kb = [_rope_t(kt[KV_W + j * HEAD_DIM:KV_W + (j + 1) * HEAD_DIM], cos, sin) for j in range(KV_HEADS)]
    ka_ref[0, 0] = jnp.concatenate(ka, axis=0).T.astype(jnp.bfloat16)
    kb_ref[0, 0] = jnp.concatenate(kb, axis=0).T.astype(jnp.bfloat16)
    vt = _dot_nt(wv_ref[...], h)
    ones = jnp.ones((V_ROWS - HEAD_DIM, t), jnp.float32)
    for br, v_ref in enumerate((va_ref, vb_ref)):
        rows = []
        for j in range(KV_HEADS):
            rows += [vt[br * KV_W + j * HEAD_DIM:br * KV_W + (j + 1) * HEAD_DIM], ones]
        v_ref[0, 0] = jnp.concatenate(rows, axis=0).astype(jnp.bfloat16)

    @pl.when(i < n_lat)
    def _():
        qg = qg_ref[...]
        for half in range(2):
            qt = _dot_nt(wq_ref[half * Q_A:(half + 1) * Q_A, :], h)
            for hd in range(A_HEADS):
                xh = qt[hd * HEAD_DIM:(hd + 1) * HEAD_DIM]
                if half == 0:
                    out = _rope_t(_head_norm_t(xh, qg), cos, sin)
                    qa_ref[0, hd * HEAD_DIM:(hd + 1) * HEAD_DIM, :] = out.astype(jnp.bfloat16)
                else:
                    out = _rope_t(xh * (ATTN_SCALE * LOG2E), cos, sin)
                    qb_ref[0, hd * HEAD_DIM:(hd + 1) * HEAD_DIM, :] = out.astype(jnp.bfloat16)
        rows = 512
        for c in range(wg_ref.shape[0] // rows):
            gt = _dot_nt(wg_ref[c * rows:(c + 1) * rows, :], h)
            gate_ref[0, c * rows:(c + 1) * rows, :] = jax.nn.sigmoid(gt).astype(jnp.bfloat16)


def _in_proj(x, ctx, mod, norm_g, wq_t, wk_t, wv_t, wg_t, qg, kg, cos, sin):
    b, s, d = x.shape
    n_ctx = ctx.shape[1]
    t = TOKEN_TILE
    n_lat = s // t
    n_tiles = n_lat + n_ctx // t
    last = n_lat - 1
    full = lambda shape: pl.BlockSpec(shape, lambda bi, i: (0,) * len(shape))
    q_spec = pl.BlockSpec((1, Q_A, t), lambda bi, i: (bi, 0, jnp.minimum(i, last)))
    k_spec = pl.BlockSpec((1, 1, t, KV_W), lambda bi, i: (bi, i, 0, 0))
    v_spec = pl.BlockSpec((1, 1, KV_HEADS * V_ROWS, t), lambda bi, i: (bi, i, 0, 0))
    bf = jnp.bfloat16
    return pl.pallas_call(
        functools.partial(_in_proj_kernel, n_lat),
        grid=(b, n_tiles),
        in_specs=[pl.BlockSpec((1, t, d), lambda bi, i: (bi, jnp.minimum(i, last), 0)),
                  pl.BlockSpec((1, t, d), lambda bi, i: (bi, jnp.maximum(i - n_lat, 0), 0)),
                  pl.BlockSpec((1, 1, 2, d), lambda bi, i: (bi, i // n_lat, 0, 0)),
                  full((1, d)), full(wq_t.shape), full(wk_t.shape), full(wv_t.shape), full(wg_t.shape),
                  full((HEAD_DIM, t)), full((HEAD_DIM, t)),
                  pl.BlockSpec((ROPE_HALF, t), lambda bi, i: (0, i)),
                  pl.BlockSpec((ROPE_HALF, t), lambda bi, i: (0, i))],
        out_specs=[q_spec, q_spec, k_spec, k_spec, v_spec, v_spec,
                   pl.BlockSpec((1, 2 * d, t), lambda bi, i: (bi, 0, jnp.minimum(i, last)))],
        out_shape=[jax.ShapeDtypeStruct((b, Q_A, s), bf), jax.ShapeDtypeStruct((b, Q_B, s), bf),
                   jax.ShapeDtypeStruct((b, n_tiles, t, KV_W), bf), jax.ShapeDtypeStruct((b, n_tiles, t, KV_W), bf),
                   jax.ShapeDtypeStruct((b, n_tiles, KV_HEADS * V_ROWS, t), bf),
                   jax.ShapeDtypeStruct((b, n_tiles, KV_HEADS * V_ROWS, t), bf),
                   jax.ShapeDtypeStruct((b, 2 * d, s), bf)],
        scratch_shapes=[pltpu.VMEM((t, d), bf)],
        compiler_params=_params(("arbitrary", "arbitrary")),
        name="in_proj",
    )(x, ctx, mod, norm_g, wq_t, wk_t, wv_t, wg_t, qg, kg, cos, sin)


def _attn_kernel(windowed, tq, n_lat, n_tiles, unroll, sink_ref, q_ref, k_ref, v_ref, y_ref, s_ref):
    qi = pl.program_id(1)
    t = TOKEN_TILE
    n = GROUP * tq
    heads = [[kvh * GROUP + g for g in range(GROUP)] for kvh in range(KV_HEADS)]
    qpad = []
    for kvh in range(KV_HEADS):
        qcat = jnp.concatenate([q_ref[0, hd * HEAD_DIM:(hd + 1) * HEAD_DIM, :] for hd in heads[kvh]], axis=1)
        zeros = jnp.zeros_like(qcat)
        qpad.append(jnp.concatenate([qcat, zeros] if kvh == 0 else [zeros, qcat], axis=0))

    def scores(slot, kb, m8, lo=0, rows=t, mask=None):
        kblk = k_ref[0, kb, lo:lo + rows, :]
        out = []
        for kvh in range(KV_HEADS):
            s = _dot(kblk, qpad[kvh])
            if mask is not None:
                s = jnp.where(mask, s, NEG_INF)
            s_ref[kvh, slot, 0:rows, :] = s
            out.append(jnp.maximum(m8[kvh], jnp.max(s.reshape(rows // 8, 8, n), axis=0)))
        return tuple(out)

    def weigh(slot, kb, m, accs, lo=0, rows=t):
        out = []
        for kvh in range(KV_HEADS):
            p = jnp.exp2(s_ref[kvh, slot, 0:rows, :] - m[kvh]).astype(jnp.bfloat16)
            vblk = v_ref[0, kb, kvh * V_ROWS:(kvh + 1) * V_ROWS, lo:lo + rows]
            out.append(accs[kvh] + _dot(vblk, p))
        return tuple(out)

    acc0 = jnp.zeros((V_ROWS, n), jnp.float32)
    if windowed:
        sink = [jnp.concatenate([jnp.full((1, tq), sink_ref[hd] * LOG2E, jnp.float32) for hd in heads[kvh]], axis=1)
                for kvh in range(KV_HEADS)]
        assert tq == t and WINDOW * 2 == t
        c_half = lax.rem(lax.broadcasted_iota(jnp.int32, (WINDOW, n), 1), tq)
        r_half = lax.broadcasted_iota(jnp.int32, (WINDOW, n), 0)
        c_full = lax.rem(lax.broadcasted_iota(jnp.int32, (t, n), 1), tq)
        r_full = lax.broadcasted_iota(jnp.int32, (t, n), 0)
        off_prev = jnp.where(qi >= 1, 0, 2 * t)
        off_next = jnp.where(qi + 1 < n_lat, 0, 2 * t)
        pieces = [
            (jnp.maximum(qi - 1, 0), WINDOW, WINDOW, r_half >= c_half + off_prev),
            (qi, 0, t, jnp.abs(r_full - c_full) <= WINDOW),
            (jnp.minimum(qi + 1, n_lat - 1), 0, WINDOW, c_half - r_half >= WINDOW + off_next),
        ] + [(kb, 0, t, None) for kb in range(n_lat, n_tiles)]
        m8 = tuple(jnp.broadcast_to(sink[kvh], (8, n)) for kvh in range(KV_HEADS))
        for slot, (kb, lo, rows, ok) in enumerate(pieces):
            m8 = scores(slot, kb, m8, lo, rows, ok)
        m = [jnp.max(m8[kvh], axis=0, keepdims=True) for kvh in range(KV_HEADS)]
        row = lax.broadcasted_iota(jnp.int32, (V_ROWS, n), 0)
        accs = tuple(jnp.where(row == HEAD_DIM, jnp.exp2(sink[kvh] - m[kvh]), 0.0) for kvh in range(KV_HEADS))
        for slot, (kb, lo, rows, _) in enumerate(pieces):
            accs = weigh(slot, kb, m, accs, lo, rows)
    else:
        m8 = tuple(jnp.full((8, n), -jnp.inf, jnp.float32) for _ in range(KV_HEADS))
        m8 = lax.fori_loop(0, n_tiles, lambda kb, c: scores(kb, kb, c), m8, unroll=unroll)
        m = [jnp.max(m8[kvh], axis=0, keepdims=True) for kvh in range(KV_HEADS)]
        accs = lax.fori_loop(0, n_tiles, lambda kb, c: weigh(kb, kb, m, c), (acc0, acc0), unroll=unroll)
    for kvh in range(KV_HEADS):
        acc = accs[kvh]
        out = acc[0:HEAD_DIM] / acc[HEAD_DIM:HEAD_DIM + 1]
        for g, hd in enumerate(heads[kvh]):
            y_ref[0, hd * HEAD_DIM:(hd + 1) * HEAD_DIM, :] = out[:, g * tq:(g + 1) * tq].astype(jnp.bfloat16)


def _attn_global_kernel(tq, n_tiles, n_q, unroll, q_ref, k_ref, v_ref, y_ref, s0_ref, s1_ref, m0_ref, m1_ref):
    j = pl.program_id(1)
    t = TOKEN_TILE
    n = GROUP * tq
    heads = [[kvh * GROUP + g for g in range(GROUP)] for kvh in range(KV_HEADS)]
    bufs = ((s0_ref, m0_ref), (s1_ref, m1_ref))

    def padded_q():
        qpad = []
        for kvh in range(KV_HEADS):
            qcat = jnp.concatenate([q_ref[0, hd * HEAD_DIM:(hd + 1) * HEAD_DIM, :] for hd in heads[kvh]], axis=1)
            zeros = jnp.zeros_like(qcat)
            qpad.append(jnp.concatenate([qcat, zeros] if kvh == 0 else [zeros, qcat], axis=0))
        return qpad

    def scores(s_ref, kb, m8, qpad):
        kblk = k_ref[0, kb]
        out = []
        for kvh in range(KV_HEADS):
            s = _dot(kblk, qpad[kvh])
            s_ref[kvh, kb] = s
            out.append(jnp.maximum(m8[kvh], jnp.max(s.reshape(t // 8, 8, n), axis=0)))
        return tuple(out)

    def weigh(s_ref, kb, m, accs):
        out = []
        for kvh in range(KV_HEADS):
            p = jnp.exp2(s_ref[kvh, kb] - m[kvh]).astype(jnp.bfloat16)
            vblk = v_ref[0, kb, kvh * V_ROWS:(kvh + 1) * V_ROWS, :]
            out.append(accs[kvh] + _dot(vblk, p))
        return tuple(out)

    def keep_maxima(m_ref, m8):
        for kvh in range(KV_HEADS):
            m_ref[kvh] = m8[kvh]

    def maxima(m_ref):
        return [jnp.max(m_ref[kvh], axis=0, keepdims=True) for kvh in range(KV_HEADS)]

    def write(accs):
        for kvh in range(KV_HEADS):
            out = accs[kvh][0:HEAD_DIM] / accs[kvh][HEAD_DIM:HEAD_DIM + 1]
            for g, hd in enumerate(heads[kvh]):
                y_ref[0, hd * HEAD_DIM:(hd + 1) * HEAD_DIM, :] = out[:, g * tq:(g + 1) * tq].astype(jnp.bfloat16)

    m8_0 = tuple(jnp.full((8, n), -jnp.inf, jnp.float32) for _ in range(KV_HEADS))
    acc_0 = tuple(jnp.zeros((V_ROWS, n), jnp.float32) for _ in range(KV_HEADS))

    @pl.when(j == 0)
    def _():
        s_w, m_w = bufs[0]
        qpad = padded_q()
        keep_maxima(m_w, lax.fori_loop(0, n_tiles, lambda kb, c: scores(s_w, kb, c, qpad), m8_0, unroll=unroll))

    for parity in range(2):
        (s_w, m_w), (s_r, m_r) = bufs[parity], bufs[1 - parity]

        @pl.when((j > 0) & (j < n_q) & (lax.rem(j, 2) == parity))
        def _():
            qpad = padded_q()
            m = maxima(m_r)
            m8, accs = lax.fori_loop(
                0, n_tiles, lambda kb, c: (scores(s_w, kb, c[0], qpad), weigh(s_r, kb, m, c[1])),
                (m8_0, acc_0), unroll=unroll)
            keep_maxima(m_w, m8)
            write(accs)

    @pl.when(j == n_q)
    def _():
        s_r, m_r = bufs[(n_q - 1) % 2]
        m = maxima(m_r)
        write(lax.fori_loop(0, n_tiles, lambda kb, c: weigh(s_r, kb, m, c), acc_0, unroll=unroll))


def _attention_global(tq, unroll, q_t, k, v_t):
    b, hq, s = q_t.shape
    n_tiles = k.shape[1]
    n_q = s // tq
    n = GROUP * tq
    return pl.pallas_call(
        functools.partial(_attn_global_kernel, tq, n_tiles, n_q, unroll),
        grid=(b, n_q + 1),
        in_specs=[pl.BlockSpec((1, hq, tq), lambda bi, j: (bi, 0, jnp.minimum(j, n_q - 1))),
                  pl.BlockSpec((1,) + k.shape[1:], lambda bi, j: (bi, 0, 0, 0)),
                  pl.BlockSpec((1,) + v_t.shape[1:], lambda bi, j: (bi, 0, 0, 0))],
        out_specs=pl.BlockSpec((1, hq, tq), lambda bi, j: (bi, 0, jnp.maximum(j - 1, 0))),
        out_shape=jax.ShapeDtypeStruct((b, hq, s), jnp.bfloat16),
        scratch_shapes=[pltpu.VMEM((KV_HEADS, n_tiles, TOKEN_TILE, n), jnp.float32)] * 2
        + [pltpu.VMEM((KV_HEADS, 8, n), jnp.float32)] * 2,
        compiler_params=_params(("arbitrary", "arbitrary")),
        name="attn_global",
    )(q_t, k, v_t)


def _attention(windowed, tq, unroll, sink, q_t, k, v_t):
    b, hq, s = q_t.shape
    n_tiles = k.shape[1]
    n_lat = s // TOKEN_TILE
    slots = tq // TOKEN_TILE + 2 + (n_tiles - n_lat) if windowed else n_tiles
    grid_spec = pltpu.PrefetchScalarGridSpec(
        num_scalar_prefetch=1,
        grid=(b, s // tq),
        in_specs=[pl.BlockSpec((1, hq, tq), lambda bi, qi, sk: (bi, 0, qi)),
                  pl.BlockSpec((1,) + k.shape[1:], lambda bi, qi, sk: (bi, 0, 0, 0)),
                  pl.BlockSpec((1,) + v_t.shape[1:], lambda bi, qi, sk: (bi, 0, 0, 0))],
        out_specs=pl.BlockSpec((1, hq, tq), lambda bi, qi, sk: (bi, 0, qi)),
        scratch_shapes=[pltpu.VMEM((KV_HEADS, slots, TOKEN_TILE, GROUP * tq), jnp.float32)],
    )
    return pl.pallas_call(
        functools.partial(_attn_kernel, windowed, tq, n_lat, n_tiles, unroll),
        grid_spec=grid_spec,
        out_shape=jax.ShapeDtypeStruct((b, hq, s), jnp.bfloat16),
        compiler_params=_params(("arbitrary", "arbitrary")),
        name="attn_window" if windowed else "attn_global",
    )(sink, q_t, k, v_t)


def _to_token_tiles(ref, row0, rows, val):
    for j in range(SUBLANES):
        ref[pl.ds(row0 * SUBLANES + j, rows, stride=SUBLANES), :] = val[:, j * LANES:(j + 1) * LANES]


def _from_token_tiles(ref, row0, rows):
    return jnp.concatenate([ref[pl.ds(row0 * SUBLANES + j, rows, stride=SUBLANES), :] for j in range(SUBLANES)],
                           axis=1)


def _merge_kernel(x_ref, ya_ref, yb_ref, gate_ref, wa_ref, wb_ref, wo_ref, g1_ref, mod_ref, g2_ref,
                  wr_ref, br_ref, x1_ref, h2_ref, idx_ref, wt_ref):
    d = x_ref.shape[-1]
    t = TOKEN_TILE
    for sub in range(x_ref.shape[1] // t):
        tok = slice(sub * t, (sub + 1) * t)
        za = _dot(wa_ref[...], ya_ref[0, :, tok])
        zb = _dot(wb_ref[...], yb_ref[0, :, tok])
        mt = (gate_ref[0, 0:d, tok].astype(jnp.float32) * za
              + gate_ref[0, d:2 * d, tok].astype(jnp.float32) * zb)
        o = _dot(mt.T.astype(jnp.bfloat16), wo_ref[...])
        x1 = x_ref[0, tok, :] + g1_ref[0] * o
        x1_ref[0, tok, :] = x1
        h2 = _mod_norm(x1, g2_ref[...], mod_ref[0, 0:1, :], mod_ref[0, 1:2, :])
        _to_token_tiles(h2_ref, sub * t, t, h2)
        logits = _dot_nt(wr_ref[...], h2, precision=lax.Precision.HIGHEST) + br_ref[...]
        n_e = logits.shape[0]
        row = lax.broadcasted_iota(jnp.int32, logits.shape, 0)
        work = logits
        top_v, top_i = [], []
        for _ in range(TOP_K):
            mk = jnp.max(work, axis=0, keepdims=True)
            ik = jnp.min(jnp.where(work == mk, row, n_e), axis=0, keepdims=True)
            top_v.append(mk)
            top_i.append(ik)
            work = jnp.where(row == ik, -jnp.inf, work)
        ex = [jnp.exp(v - top_v[0]) for v in top_v]
        den = ex[0] + ex[1] + ex[2] + ex[3]
        idx_ref[0, :, tok] = jnp.concatenate(top_i, axis=0)
        wt_ref[0, :, tok] = jnp.concatenate([e / den for e in ex], axis=0)


def _merge(x, ya_t, yb_t, gate_t, wa_t, wb_t, wo, g1, mod2, norm2_g, wr_t, br):
    b, s, d = x.shape
    t = MERGE_TILE
    per = s // t
    assert d == SUBLANES * LANES
    full = lambda shape: pl.BlockSpec(shape, lambda bi, i: (0,) * len(shape))
    tok = pl.BlockSpec((1, t, d), lambda bi, i: (bi, i, 0))
    col = lambda rows: pl.BlockSpec((1, rows, t), lambda bi, i: (bi, 0, i))
    return pl.pallas_call(
        _merge_kernel,
        grid=(b, per),
        in_specs=[tok, col(Q_A), col(Q_B), col(2 * d),
                  full(wa_t.shape), full(wb_t.shape), full(wo.shape),
                  pl.BlockSpec((1, 1, d), lambda bi, i: (bi, 0, 0)),
                  pl.BlockSpec((1, 2, d), lambda bi, i: (bi, 0, 0)),
                  full((1, d)), full(wr_t.shape), full(br.shape)],
        out_specs=[tok, pl.BlockSpec((t * SUBLANES, LANES), lambda bi, i: (bi * per + i, 0)),
                   col(TOP_K), col(TOP_K)],
        out_shape=[jax.ShapeDtypeStruct((b, s, d), jnp.float32),
                   jax.ShapeDtypeStruct((b * s * SUBLANES, LANES), jnp.float32),
                   jax.ShapeDtypeStruct((b, TOP_K, s), jnp.int32), jax.ShapeDtypeStruct((b, TOP_K, s), jnp.float32)],
        compiler_params=_params(("arbitrary", "arbitrary")),
        name="merge_router",
    )(x, ya_t, yb_t, gate_t, wa_t, wb_t, wo, g1, mod2, norm2_g, wr_t, br)


def _w1_split_kernel(w_ref, p_ref, g_ref, l_ref):
    half = SPLIT_BLOCK // 2
    for jb in range(w_ref.shape[-1] // SPLIT_BLOCK):
        blk = w_ref[0, :, jb * SPLIT_BLOCK:(jb + 1) * SPLIT_BLOCK].astype(jnp.bfloat16)
        r = _dot(blk, p_ref[...])
        g_ref[0, :, jb * half:(jb + 1) * half] = r[:, :half].astype(jnp.bfloat16)
        l_ref[0, :, jb * half:(jb + 1) * half] = r[:, half:].astype(jnp.bfloat16)


def _w1_split(w1):
    n_e, d, d2 = w1.shape
    rows = 512
    half = SPLIT_BLOCK // 2
    perm = np.zeros((SPLIT_BLOCK, SPLIT_BLOCK), np.float32)
    for j in range(half):
        perm[2 * j, j] = 1.0
        perm[2 * j + 1, half + j] = 1.0
    out = pl.BlockSpec((1, rows, d2 // 2), lambda e, r: (e, r, 0))
    return pl.pallas_call(
        _w1_split_kernel,
        grid=(n_e, d // rows),
        in_specs=[pl.BlockSpec((1, rows, d2), lambda e, r: (e, r, 0)),
                  pl.BlockSpec((SPLIT_BLOCK, SPLIT_BLOCK), lambda e, r: (0, 0))],
        out_specs=[out, out],
        out_shape=[jax.ShapeDtypeStruct((n_e, d, d2 // 2), jnp.bfloat16)] * 2,
        compiler_params=_params(("arbitrary", "arbitrary")),
        name="w1_split",
    )(w1, jnp.asarray(perm, jnp.bfloat16))


def _moe_kernel(tile, offs_ref, tok_ref, wt_ref, h_ref, w1g_ref, w1l_ref, b1g_ref, b1l_ref, w2_ref, b2_ref,
                acc_ref, xg_ref, y_ref, pend_ref):
    ti = pl.program_id(0)
    e = pl.program_id(1)
    n_e = pl.num_programs(1)
    ch = xg_ref.shape[0] // SUBLANES
    off = ti * (n_e + 1)

    def tile_rows(r):
        return pl.ds(pl.multiple_of(r * SUBLANES, SUBLANES), SUBLANES)

    def static_rows(r):
        return slice(r * SUBLANES, (r + 1) * SUBLANES)

    def expert_mlp(xg):
        glu = jnp.minimum(_dot(xg, w1g_ref[0]) + b1g_ref[0], SWIGLU_LIMIT)
        lin = jnp.clip(_dot(xg, w1l_ref[0]) + b1l_ref[0], -SWIGLU_LIMIT, SWIGLU_LIMIT)
        act = glu * jax.nn.sigmoid(SWIGLU_ALPHA * glu) * (lin + 1.0)
        return _dot(act.astype(jnp.bfloat16), w2_ref[0]) + b2_ref[0]

    def gather_rows(dst_ref, base, n):
        def group(i, _):
            r0 = pl.multiple_of(i * ROW_UNROLL, ROW_UNROLL)
            for u in range(ROW_UNROLL):
                dst_ref[tile_rows(r0 + u), :] = h_ref[tile_rows(tok_ref[0, 0, base + r0 + u]), :]
            return 0

        lax.fori_loop(0, (n + ROW_UNROLL - 1) // ROW_UNROLL, group, 0)

    def scatter_rows(src_ref, base, n):
        def group(i, _):
            r0 = pl.multiple_of(i * ROW_UNROLL, ROW_UNROLL)
            toks = [tok_ref[0, 0, base + r0 + u] for u in range(ROW_UNROLL)]
            new = [acc_ref[tile_rows(toks[u]), :] + wt_ref[0, 0, base + r0 + u] * src_ref[tile_rows(r0 + u), :]
                   for u in range(ROW_UNROLL)]
            for u in range(ROW_UNROLL):
                acc_ref[tile_rows(toks[u]), :] = new[u]
            return 0

        full_groups = n // ROW_UNROLL
        lax.fori_loop(0, full_groups, group, 0)

        def row(r, _):
            tok = tok_ref[0, 0, base + r]
            acc_ref[tile_rows(tok), :] = acc_ref[tile_rows(tok), :] + wt_ref[0, 0, base + r] * src_ref[tile_rows(r), :]
            return 0

        lax.fori_loop(full_groups * ROW_UNROLL, n, row, 0)

    start = offs_ref[off + e]
    count = offs_ref[off + e + 1] - start
    n_blocks = jnp.maximum((count + ch - 1) // ch, 1)
    start_next = offs_ref[off + jnp.minimum(e + 1, n_e - 1)]

    @pl.when((ti == 0) & (e == 0))
    def _():
        for ref in (xg_ref, y_ref):
            ref[...] = jnp.zeros_like(ref)

    @pl.when(e == 0)
    def _():
        acc_ref[...] = jnp.zeros_like(acc_ref)
        pend_ref[0] = 0
        pend_ref[1] = 0
        gather_rows(xg_ref, start, jnp.minimum(count, ch))

    def block(k, pending):
        base_prev, n_prev = pending
        xg = _from_token_tiles(xg_ref, 0, ch).astype(jnp.bfloat16)

        base_next = jnp.where(k + 1 < n_blocks, start + (k + 1) * ch, start_next)
        for r in range(ch):
            xg_ref[static_rows(r), :] = h_ref[tile_rows(tok_ref[0, 0, base_next + r]), :]

        for g in range(ch // ROW_UNROLL):
            toks, new = [], []
            for u in range(ROW_UNROLL):
                r = g * ROW_UNROLL + u
                valid = r < n_prev
                tok = jnp.where(valid, tok_ref[0, 0, base_prev + r], tile + u)
                w = jnp.where(valid, wt_ref[0, 0, base_prev + r], 0.0)
                toks.append(tok)
                new.append(acc_ref[tile_rows(tok), :] + w * y_ref[static_rows(r), :])
            for u in range(ROW_UNROLL):
                acc_ref[tile_rows(toks[u]), :] = new[u]

        _to_token_tiles(y_ref, 0, ch, expert_mlp(xg))
        return start + k * ch, jnp.minimum(ch, count - k * ch)

    base_prev, n_prev = lax.fori_loop(0, n_blocks, block, (pend_ref[0], pend_ref[1]))
    pend_ref[0] = base_prev
    pend_ref[1] = n_prev

    @pl.when(e == n_e - 1)
    def _():
        scatter_rows(y_ref, base_prev, n_prev)


def _moe(offs, tok_sorted, wt_sorted, h2_tiles, w1g, w1l, b1g, b1l, w2, b2, tile):
    n_tok = h2_tiles.shape[0] // SUBLANES
    n_e, d, d_e = w1g.shape
    n_t = n_tok // tile
    a = tok_sorted.shape[-1]
    assert d == SUBLANES * LANES and MOE_CHUNK % ROW_UNROLL == 0
    smem = lambda: pl.BlockSpec((1, 1, a), lambda ti, e, o: (ti, 0, 0), memory_space=pltpu.SMEM)
    ex = lambda shape: pl.BlockSpec((1,) + shape, lambda ti, e, o: (e, 0, 0))
    out_rows = (tile + ROW_UNROLL) * SUBLANES
    rows = pltpu.VMEM((MOE_CHUNK * SUBLANES, LANES), jnp.float32)
    once = pl.Buffered(1)
    grid_spec = pltpu.PrefetchScalarGridSpec(
        num_scalar_prefetch=1,
        grid=(n_t, n_e),
        in_specs=[smem(), smem(),
                  pl.BlockSpec((tile * SUBLANES, LANES), lambda ti, e, o: (ti, 0), pipeline_mode=once),
                  ex((d, d_e)), ex((d, d_e)), ex((1, d_e)), ex((1, d_e)), ex((d_e, d)), ex((1, d))],
        out_specs=pl.BlockSpec((out_rows, LANES), lambda ti, e, o: (ti, 0), pipeline_mode=once),
        scratch_shapes=[rows, rows, pltpu.SMEM((2,), jnp.int32)],
    )
    return pl.pallas_call(
        functools.partial(_moe_kernel, tile),
        grid_spec=grid_spec,
        out_shape=jax.ShapeDtypeStruct((n_t * out_rows, LANES), jnp.float32),
        compiler_params=pltpu.CompilerParams(dimension_semantics=("arbitrary", "arbitrary"),
                                             vmem_limit_bytes=MOE_VMEM_LIMIT),
        name="moe",
    )(offs, tok_sorted, wt_sorted, h2_tiles, w1g, w1l, b1g, b1l, w2, b2)


def _final_kernel(sub, x_ref, m_ref, g2_ref, g_ref, o_ref):
    t = x_ref.shape[1]
    row0 = lax.rem(pl.program_id(1), sub) * t
    x = x_ref[0] + g2_ref[0] * _from_token_tiles(m_ref, row0, t)
    ms = jnp.mean(x * x, axis=-1, keepdims=True)
    o_ref[0] = x * lax.rsqrt(ms + RMS_EPS) * g_ref[...]


def _final(x1, moe_tiles, g2, final_g, moe_tile):
    b, s, d = x1.shape
    t = min(MERGE_TILE, moe_tile)
    per = s // t
    sub = moe_tile // t
    moe_rows = (moe_tile + ROW_UNROLL) * SUBLANES
    tok = pl.BlockSpec((1, t, d), lambda bi, i: (bi, i, 0))
    return pl.pallas_call(
        functools.partial(_final_kernel, sub),
        grid=(b, per),
        in_specs=[tok, pl.BlockSpec((moe_rows, LANES), lambda bi, i: (bi * (per // sub) + i // sub, 0)),
                  pl.BlockSpec((1, 1, d), lambda bi, i: (bi, 0, 0)),
                  pl.BlockSpec((1, d), lambda bi, i: (0, 0))],
        out_specs=tok,
        out_shape=jax.ShapeDtypeStruct((b, s, d), jnp.float32),
        compiler_params=_params(("arbitrary", "arbitrary")),
        name="final_norm",
    )(x1, moe_tiles, g2, final_g)


def _rope_tables(s, n_ctx):
    inv = ROPE_THETA ** (-(jnp.arange(ROPE_QUARTER, dtype=jnp.float32) * 2.0 / ROPE_HALF))
    pos = jnp.arange(s)
    ang_r = (pos // GRID_W).astype(jnp.float32)[None, :] * inv[:, None]
    ang_c = (pos % GRID_W).astype(jnp.float32)[None, :] * inv[:, None]
    ang = jnp.concatenate([ang_r, ang_c], axis=0)
    cos = jnp.concatenate([jnp.cos(ang), jnp.ones((ROPE_HALF, n_ctx), jnp.float32)], axis=1)
    sin = jnp.concatenate([jnp.sin(ang), jnp.zeros((ROPE_HALF, n_ctx), jnp.float32)], axis=1)
    return cos, sin


def _route_lists(top_idx, top_w, tile):
    b, k, s = top_idx.shape
    n_t = b * s // tile
    per = s // tile
    e_flat = top_idx.reshape(b, k, per, tile).transpose(0, 2, 1, 3).reshape(n_t, k * tile)
    w_flat = top_w.reshape(b, k, per, tile).transpose(0, 2, 1, 3).reshape(n_t, k * tile)
    tok = jnp.tile(jnp.arange(tile, dtype=jnp.int32), k)[None, :].repeat(n_t, axis=0)
    e_sorted, tok_sorted, w_sorted = lax.sort((e_flat, tok, w_flat), dimension=1, num_keys=2)
    counts = jnp.sum(e_flat[:, :, None] == jnp.arange(N_EXPERTS, dtype=jnp.int32)[None, None, :], axis=1,
                     dtype=jnp.int32)
    offs = jnp.concatenate([jnp.zeros((n_t, 1), jnp.int32), jnp.cumsum(counts, axis=1, dtype=jnp.int32)], axis=1)
    pad = ((0, 0), (0, MOE_CHUNK))
    return offs.reshape(-1), jnp.pad(tok_sorted, pad)[:, None, :], jnp.pad(w_sorted, pad)[:, None, :]


def kernel(x, c, ctx, c_ctx, w_mod, b_mod, norm1_g, norm2_g, w_in, q_norm_g, k_norm_g, sink, w_br_a, w_br_b,
           w_o, w_router, b_router, w_e1, b_e1, w_e2, b_e2, final_g):
    b, s, d = x.shape
    n_ctx = ctx.shape[1]
    assert w_mod.shape[0] == 1
    bf = jnp.bfloat16
    cos, sin = _rope_tables(s, n_ctx)
    t = TOKEN_TILE
    moe_tile = min(MOE_TILE, s)
    for l in range(1):
        rows = ((b + 1 + 7) // 8) * 8
        cond = jnp.zeros((rows, d), jnp.float32).at[:b].set(c).at[b].set(c_ctx)
        mods = _adaln(cond, w_mod[l], b_mod[l])
        sh1, sc1, g1, sh2, sc2, g2 = jnp.split(mods, 6, axis=-1)
        mod1 = jnp.stack([jnp.stack([sh1[:b], sc1[:b]], axis=1),
                          jnp.broadcast_to(jnp.stack([sh1[b], sc1[b]], axis=0)[None], (b, 2, d))], axis=1)
        mod2 = jnp.stack([sh2[:b], sc2[:b]], axis=1)

        w = w_in[l]
        kv = w[:, :4 * KV_W]
        wk_t = jnp.concatenate([kv[:, 0:KV_W], kv[:, 2 * KV_W:3 * KV_W]], axis=1).T.astype(bf)
        wv_t = jnp.concatenate([kv[:, KV_W:2 * KV_W], kv[:, 3 * KV_W:4 * KV_W]], axis=1).T.astype(bf)
        wq_t = w[:, 4 * KV_W:4 * KV_W + Q_A + Q_B].T.astype(bf)
        wg_t = w[:, 4 * KV_W + Q_A + Q_B:].T.astype(bf)
        qg = jnp.broadcast_to((q_norm_g[l] * (ATTN_SCALE * LOG2E))[:, None], (HEAD_DIM, t))
        kg = jnp.broadcast_to(k_norm_g[l][:, None], (HEAD_DIM, t))

        qa_t, qb_t, ka, kb, va_t, vb_t, gate_t = _in_proj(
            x, ctx, mod1, norm1_g[l][None], wq_t, wk_t, wv_t, wg_t, qg, kg, cos, sin)

        ya_t = _attention_global(128, 2, qa_t, ka, va_t)
        yb_t = _attention(True, 256, 1, sink[l], qb_t, kb, vb_t)

        x1, h2, top_idx, top_w = _merge(
            x, ya_t, yb_t, gate_t, w_br_a[l].T.astype(bf), w_br_b[l].T.astype(bf), w_o[l].astype(bf),
            g1[:b, None, :], mod2, norm2_g[l][None], w_router[l].T, b_router[l][:, None])

        offs, tok_sorted, w_sorted = _route_lists(top_idx, top_w, moe_tile)
        w1g, w1l = _w1_split(w_e1[l])
        moe_tiles = _moe(offs, tok_sorted, w_sorted, h2, w1g, w1l,
                         b_e1[l][:, None, 0::2], b_e1[l][:, None, 1::2],
                         w_e2[l].astype(bf), b_e2[l][:, None, :], moe_tile)
        out = _final(x1, moe_tiles, g2[:b, None, :], final_g[None], moe_tile)
    return out
```

```python
import functools

import numpy as np
import jax
import jax.numpy as jnp
from jax import lax
from jax.experimental import pallas as pl
from jax.experimental.pallas import tpu as pltpu

HEAD_DIM = 64
KV_HEADS = 2
A_HEADS = 8
B_HEADS = 8
GROUP = A_HEADS // KV_HEADS
GRID_W = 64
WINDOW = 128
ROPE_HALF = HEAD_DIM // 2
ROPE_QUARTER = ROPE_HALF // 2
ROPE_THETA = 10000.0
N_EXPERTS = 32
TOP_K = 4
SWIGLU_ALPHA = 1.702
SWIGLU_LIMIT = 7.0
RMS_EPS = 1e-6
ATTN_SCALE = HEAD_DIM ** -0.5
LOG2E = 1.4426950408889634
NEG_INF = -1e30
KV_W = KV_HEADS * HEAD_DIM
Q_A = A_HEADS * HEAD_DIM
Q_B = B_HEADS * HEAD_DIM
V_ROWS = HEAD_DIM + 16

SUBLANES = 8
LANES = 128
TOKEN_TILE = 256
MERGE_TILE = 512
MOE_TILE = 4096
MOE_CHUNK = 256
MOE_VMEM_LIMIT = 60 * 1024 * 1024
ROW_UNROLL = 8
SPLIT_BLOCK = 256
VMEM_LIMIT = 56 * 1024 * 1024

_NT = (((1,), (1,)), ((), ()))


def _dot_nt(a, b, precision=None):
    return lax.dot_general(a, b, _NT, preferred_element_type=jnp.float32, precision=precision)


def _dot(a, b, precision=None):
    return jnp.dot(a, b, preferred_element_type=jnp.float32, precision=precision)


def _params(sem):
    return pltpu.CompilerParams(dimension_semantics=sem, vmem_limit_bytes=VMEM_LIMIT)


def _adaln_kernel(cond_ref, w_ref, b_ref, o_ref):
    cond = cond_ref[...]
    act = cond * jax.nn.sigmoid(cond)
    o_ref[...] = _dot(act, w_ref[...], precision=lax.Precision.HIGHEST) + b_ref[...]


def _adaln(cond, w, b):
    rows, d = cond.shape
    n = w.shape[1]
    tn = 1024
    return pl.pallas_call(
        _adaln_kernel,
        grid=(n // tn,),
        in_specs=[pl.BlockSpec((rows, d), lambda j: (0, 0)),
                  pl.BlockSpec((d, tn), lambda j: (0, j)),
                  pl.BlockSpec((1, tn), lambda j: (0, j))],
        out_specs=pl.BlockSpec((rows, tn), lambda j: (0, j)),
        out_shape=jax.ShapeDtypeStruct((rows, n), jnp.float32),
        compiler_params=_params(("arbitrary",)),
        name="adaln",
    )(cond, w, b.reshape(1, n))


def _rope_t(xh, cos, sin):
    q = ROPE_QUARTER
    a0, a1, b0, b1 = xh[0:q], xh[q:2 * q], xh[2 * q:3 * q], xh[3 * q:4 * q]
    cr, cc = cos[0:q], cos[q:2 * q]
    sr, sc = sin[0:q], sin[q:2 * q]
    return jnp.concatenate([a0 * cr - a1 * sr, a1 * cr + a0 * sr,
                            b0 * cc - b1 * sc, b1 * cc + b0 * sc], axis=0)


def _head_norm_t(xh, g):
    ms = jnp.mean(xh * xh, axis=0, keepdims=True)
    return xh * lax.rsqrt(ms + RMS_EPS) * g


def _mod_norm(x, g, shift, scale):
    ms = jnp.mean(x * x, axis=-1, keepdims=True)
    return x * lax.rsqrt(ms + RMS_EPS) * g * (1.0 + scale) + shift


def _in_proj_kernel(n_lat, x_ref, ctx_ref, mod_ref, g_ref, wq_ref, wk_ref, wv_ref, wg_ref,
                    qg_ref, kg_ref, cos_ref, sin_ref,
                    qa_ref, qb_ref, ka_ref, kb_ref, va_ref, vb_ref, gate_ref, h_ref):
    i = pl.program_id(1)
    shift = mod_ref[0, 0, 0:1, :]
    scale = mod_ref[0, 0, 1:2, :]

    @pl.when(i < n_lat)
    def _():
        h_ref[...] = _mod_norm(x_ref[0], g_ref[...], shift, scale).astype(jnp.bfloat16)

    @pl.when(i >= n_lat)
    def _():
        h_ref[...] = _mod_norm(ctx_ref[0], g_ref[...], shift, scale).astype(jnp.bfloat16)

    h = h_ref[...]
    cos = cos_ref[...]
    sin = sin_ref[...]
    t = h.shape[0]

    kt = _dot_nt(wk_ref[...], h)
    kg = kg_ref[...]
    ka = [_rope_t(_head_norm_t(kt[j * HEAD_DIM:(j + 1) * HEAD_DIM], kg), cos, sin) for j in range(KV_HEADS)]
    kb = [_rope_t(kt[KV_W + j * HEAD_DIM:KV_W + (j + 1) * HEAD_DIM], cos, sin) for j in range(KV_HEADS)]
    ka_ref[0, 0] = jnp.concatenate(ka, axis=0).T.astype(jnp.bfloat16)
    kb_ref[0, 0] = jnp.concatenate(kb, axis=0).T.astype(jnp.bfloat16)
    vt = _dot_nt(wv_ref[...], h)
    ones = jnp.ones((V_ROWS - HEAD_DIM, t), jnp.float32)
    for br, v_ref in enumerate((va_ref, vb_ref)):
        rows = []
        for j in range(KV_HEADS):
            rows += [vt[br * KV_W + j * HEAD_DIM:br * KV_W + (j + 1) * HEAD_DIM], ones]
        v_ref[0, 0] = jnp.concatenate(rows, axis=0).astype(jnp.bfloat16)

    @pl.when(i < n_lat)
    def _():
        qg = qg_ref[...]
        for half in range(2):
            qt = _dot_nt(wq_ref[half * Q_A:(half + 1) * Q_A, :], h)
            for hd in range(A_HEADS):
                xh = qt[hd * HEAD_DIM:(hd + 1) * HEAD_DIM]
                if half == 0:
                    out = _rope_t(_head_norm_t(xh, qg), cos, sin)
                    qa_ref[0, hd * HEAD_DIM:(hd + 1) * HEAD_DIM, :] = out.astype(jnp.bfloat16)
                else:
                    out = _rope_t(xh * (ATTN_SCALE * LOG2E), cos, sin)
                    qb_ref[0, hd * HEAD_DIM:(hd + 1) * HEAD_DIM, :] = out.astype(jnp.bfloat16)
        rows = 512
        for c in range(wg_ref.shape[0] // rows):
            gt = _dot_nt(wg_ref[c * rows:(c + 1) * rows, :], h)
            gate_ref[0, c * rows:(c + 1) * rows, :] = jax.nn.sigmoid(gt).astype(jnp.bfloat16)


def _in_proj(x, ctx, mod, norm_g, wq_t, wk_t, wv_t, wg_t, qg, kg, cos, sin):
    b, s, d = x.shape
    n_ctx = ctx.shape[1]
    t = TOKEN_TILE
    n_lat = s // t
    n_tiles = n_lat + n_ctx // t
    last = n_lat - 1
    full = lambda shape: pl.BlockSpec(shape, lambda bi, i: (0,) * len(shape))
    q_spec = pl.BlockSpec((1, Q_A, t), lambda bi, i: (bi, 0, jnp.minimum(i, last)))
    k_spec = pl.BlockSpec((1, 1, t, KV_W), lambda bi, i: (bi, i, 0, 0))
    v_spec = pl.BlockSpec((1, 1, KV_HEADS * V_ROWS, t), lambda bi, i: (bi, i, 0, 0))
    bf = jnp.bfloat16
    return pl.pallas_call(
        functools.partial(_in_proj_kernel, n_lat),
        grid=(b, n_tiles),
        in_specs=[pl.BlockSpec((1, t, d), lambda bi, i: (bi, jnp.minimum(i, last), 0)),
                  pl.BlockSpec((1, t, d), lambda bi, i: (bi, jnp.maximum(i - n_lat, 0), 0)),
                  pl.BlockSpec((1, 1, 2, d), lambda bi, i: (bi, i // n_lat, 0, 0)),
                  full((1, d)), full(wq_t.shape), full(wk_t.shape), full(wv_t.shape), full(wg_t.shape),
                  full((HEAD_DIM, t)), full((HEAD_DIM, t)),
                  pl.BlockSpec((ROPE_HALF, t), lambda bi, i: (0, i)),
                  pl.BlockSpec((ROPE_HALF, t), lambda bi, i: (0, i))],
        out_specs=[q_spec, q_spec, k_spec, k_spec, v_spec, v_spec,
                   pl.BlockSpec((1, 2 * d, t), lambda bi, i: (bi, 0, jnp.minimum(i, last)))],
        out_shape=[jax.ShapeDtypeStruct((b, Q_A, s), bf), jax.ShapeDtypeStruct((b, Q_B, s), bf),
                   jax.ShapeDtypeStruct((b, n_tiles, t, KV_W), bf), jax.ShapeDtypeStruct((b, n_tiles, t, KV_W), bf),
                   jax.ShapeDtypeStruct((b, n_tiles, KV_HEADS * V_ROWS, t), bf),
                   jax.ShapeDtypeStruct((b, n_tiles, KV_HEADS * V_ROWS, t), bf),
                   jax.ShapeDtypeStruct((b, 2 * d, s), bf)],
        scratch_shapes=[pltpu.VMEM((t, d), bf)],
        compiler_params=_params(("arbitrary", "arbitrary")),
        name="in_proj",
    )(x, ctx, mod, norm_g, wq_t, wk_t, wv_t, wg_t, qg, kg, cos, sin)


def _attn_window_kernel(subs, n_lat, n_tiles, sink_ref, q_ref, k_ref, v_ref, y_ref, s_ref):
    t = tq = TOKEN_TILE
    n = GROUP * tq
    assert WINDOW * 2 == t
    heads = [[kvh * GROUP + g for g in range(GROUP)] for kvh in range(KV_HEADS)]
    n_slots = s_ref.shape[1] // subs
    sink = [jnp.concatenate([jnp.full((1, tq), sink_ref[hd] * LOG2E, jnp.float32) for hd in heads[kvh]], axis=1)
            for kvh in range(KV_HEADS)]
    c_half = lax.rem(lax.broadcasted_iota(jnp.int32, (WINDOW, n), 1), tq)
    r_half = lax.broadcasted_iota(jnp.int32, (WINDOW, n), 0)
    c_full = lax.rem(lax.broadcasted_iota(jnp.int32, (t, n), 1), tq)
    r_full = lax.broadcasted_iota(jnp.int32, (t, n), 0)
    acc_row = lax.broadcasted_iota(jnp.int32, (V_ROWS, n), 0)

    for sub in range(subs):
        qi = pl.program_id(1) * subs + sub
        cols = slice(sub * tq, (sub + 1) * tq)
        qpad = []
        for kvh in range(KV_HEADS):
            qcat = jnp.concatenate([q_ref[0, hd * HEAD_DIM:(hd + 1) * HEAD_DIM, cols] for hd in heads[kvh]], axis=1)
            zeros = jnp.zeros_like(qcat)
            qpad.append(jnp.concatenate([qcat, zeros] if kvh == 0 else [zeros, qcat], axis=0))

        off_prev = jnp.where(qi >= 1, 0, 2 * t)
        off_next = jnp.where(qi + 1 < n_lat, 0, 2 * t)
        pieces = [
            (jnp.maximum(qi - 1, 0), WINDOW, WINDOW, r_half >= c_half + off_prev),
            (qi, 0, t, jnp.abs(r_full - c_full) <= WINDOW),
            (jnp.minimum(qi + 1, n_lat - 1), 0, WINDOW, c_half - r_half >= WINDOW + off_next),
        ] + [(kb, 0, t, None) for kb in range(n_lat, n_tiles)]

        m8 = [jnp.broadcast_to(sink[kvh], (8, n)) for kvh in range(KV_HEADS)]
        for slot, (kb, lo, rows, ok) in enumerate(pieces):
            kblk = k_ref[0, kb, lo:lo + rows, :]
            for kvh in range(KV_HEADS):
                s = _dot(kblk, qpad[kvh])
                if ok is not None:
                    s = jnp.where(ok, s, NEG_INF)
                s_ref[kvh, sub * n_slots + slot, 0:rows, :] = s
                m8[kvh] = jnp.maximum(m8[kvh], jnp.max(s.reshape(rows // 8, 8, n), axis=0))
        m = [jnp.max(m8[kvh], axis=0, keepdims=True) for kvh in range(KV_HEADS)]

        accs = [jnp.where(acc_row == HEAD_DIM, jnp.exp2(sink[kvh] - m[kvh]), 0.0) for kvh in range(KV_HEADS)]
        for slot, (kb, lo, rows, _) in enumerate(pieces):
            for kvh in range(KV_HEADS):
                p = jnp.exp2(s_ref[kvh, sub * n_slots + slot, 0:rows, :] - m[kvh]).astype(jnp.bfloat16)
                vblk = v_ref[0, kb, kvh * V_ROWS:(kvh + 1) * V_ROWS, lo:lo + rows]
                accs[kvh] = accs[kvh] + _dot(vblk, p)
        for kvh in range(KV_HEADS):
            out = accs[kvh][0:HEAD_DIM] / accs[kvh][HEAD_DIM:HEAD_DIM + 1]
            for g, hd in enumerate(heads[kvh]):
                y_ref[0, hd * HEAD_DIM:(hd + 1) * HEAD_DIM, cols] = out[:, g * tq:(g + 1) * tq].astype(jnp.bfloat16)


def _attn_global_kernel(tq, n_tiles, n_q, unroll, q_ref, k_ref, v_ref, y_ref, s0_ref, s1_ref, m0_ref, m1_ref):
    j = pl.program_id(1)
    t = TOKEN_TILE
    n = GROUP * tq
    heads = [[kvh * GROUP + g for g in range(GROUP)] for kvh in range(KV_HEADS)]
    bufs = ((s0_ref, m0_ref), (s1_ref, m1_ref))

    def padded_q():
        qpad = []
        for kvh in range(KV_HEADS):
            qcat = jnp.concatenate([q_ref[0, hd * HEAD_DIM:(hd + 1) * HEAD_DIM, :] for hd in heads[kvh]], axis=1)
            zeros = jnp.zeros_like(qcat)
            qpad.append(jnp.concatenate([qcat, zeros] if kvh == 0 else [zeros, qcat], axis=0))
        return qpad

    def scores(s_ref, kb, m8, qpad):
        kblk = k_ref[0, kb]
        out = []
        for kvh in range(KV_HEADS):
            s = _dot(kblk, qpad[kvh])
            s_ref[kvh, kb] = s
            out.append(jnp.maximum(m8[kvh], jnp.max(s.reshape(t // 8, 8, n), axis=0)))
        return tuple(out)

    def weigh(s_ref, kb, m, accs):
        out = []
        for kvh in range(KV_HEADS):
            p = jnp.exp2(s_ref[kvh, kb] - m[kvh]).astype(jnp.bfloat16)
            vblk = v_ref[0, kb, kvh * V_ROWS:(kvh + 1) * V_ROWS, :]
            out.append(accs[kvh] + _dot(vblk, p))
        return tuple(out)

    def keep_maxima(m_ref, m8):
        for kvh in range(KV_HEADS):
            m_ref[kvh] = m8[kvh]

    def maxima(m_ref):
        return [jnp.max(m_ref[kvh], axis=0, keepdims=True) for kvh in range(KV_HEADS)]

    def write(accs):
        for kvh in range(KV_HEADS):
            out = accs[kvh][0:HEAD_DIM] / accs[kvh][HEAD_DIM:HEAD_DIM + 1]
            for g, hd in enumerate(heads[kvh]):
                y_ref[0, hd * HEAD_DIM:(hd + 1) * HEAD_DIM, :] = out[:, g * tq:(g + 1) * tq].astype(jnp.bfloat16)

    m8_0 = tuple(jnp.full((8, n), -jnp.inf, jnp.float32) for _ in range(KV_HEADS))
    acc_0 = tuple(jnp.zeros((V_ROWS, n), jnp.float32) for _ in range(KV_HEADS))

    @pl.when(j == 0)
    def _():
        s_w, m_w = bufs[0]
        qpad = padded_q()
        keep_maxima(m_w, lax.fori_loop(0, n_tiles, lambda kb, c: scores(s_w, kb, c, qpad), m8_0, unroll=unroll))

    for parity in range(2):
        (s_w, m_w), (s_r, m_r) = bufs[parity], bufs[1 - parity]

        @pl.when((j > 0) & (j < n_q) & (lax.rem(j, 2) == parity))
        def _():
            qpad = padded_q()
            m = maxima(m_r)
            m8, accs = lax.fori_loop(
                0, n_tiles, lambda kb, c: (scores(s_w, kb, c[0], qpad), weigh(s_r, kb, m, c[1])),
                (m8_0, acc_0), unroll=unroll)
            keep_maxima(m_w, m8)
            write(accs)

    @pl.when(j == n_q)
    def _():
        s_r, m_r = bufs[(n_q - 1) % 2]
        m = maxima(m_r)
        write(lax.fori_loop(0, n_tiles, lambda kb, c: weigh(s_r, kb, m, c), acc_0, unroll=unroll))


def _attention_global(tq, unroll, q_t, k, v_t):
    b, hq, s = q_t.shape
    n_tiles = k.shape[1]
    n_q = s // tq
    n = GROUP * tq
    return pl.pallas_call(
        functools.partial(_attn_global_kernel, tq, n_tiles, n_q, unroll),
        grid=(b, n_q + 1),
        in_specs=[pl.BlockSpec((1, hq, tq), lambda bi, j: (bi, 0, jnp.minimum(j, n_q - 1))),
                  pl.BlockSpec((1,) + k.shape[1:], lambda bi, j: (bi, 0, 0, 0)),
                  pl.BlockSpec((1,) + v_t.shape[1:], lambda bi, j: (bi, 0, 0, 0))],
        out_specs=pl.BlockSpec((1, hq, tq), lambda bi, j: (bi, 0, jnp.maximum(j - 1, 0))),
        out_shape=jax.ShapeDtypeStruct((b, hq, s), jnp.bfloat16),
        scratch_shapes=[pltpu.VMEM((KV_HEADS, n_tiles, TOKEN_TILE, n), jnp.float32)] * 2
        + [pltpu.VMEM((KV_HEADS, 8, n), jnp.float32)] * 2,
        compiler_params=_params(("arbitrary", "arbitrary")),
        name="attn_global",
    )(q_t, k, v_t)


def _attention_window(sink, q_t, k, v_t):
    b, hq, s = q_t.shape
    n_tiles = k.shape[1]
    t = TOKEN_TILE
    n_lat = s // t
    subs = 2 if n_lat % 2 == 0 else 1
    slots = 3 + (n_tiles - n_lat)
    grid_spec = pltpu.PrefetchScalarGridSpec(
        num_scalar_prefetch=1,
        grid=(b, n_lat // subs),
        in_specs=[pl.BlockSpec((1, hq, subs * t), lambda bi, qi, sk: (bi, 0, qi)),
                  pl.BlockSpec((1,) + k.shape[1:], lambda bi, qi, sk: (bi, 0, 0, 0)),
                  pl.BlockSpec((1,) + v_t.shape[1:], lambda bi, qi, sk: (bi, 0, 0, 0))],
        out_specs=pl.BlockSpec((1, hq, subs * t), lambda bi, qi, sk: (bi, 0, qi)),
        scratch_shapes=[pltpu.VMEM((KV_HEADS, subs * slots, t, GROUP * t), jnp.float32)],
    )
    return pl.pallas_call(
        functools.partial(_attn_window_kernel, subs, n_lat, n_tiles),
        grid_spec=grid_spec,
        out_shape=jax.ShapeDtypeStruct((b, hq, s), jnp.bfloat16),
        compiler_params=_params(("arbitrary", "arbitrary")),
        name="attn_window",
    )(sink, q_t, k, v_t)


def _to_token_tiles(ref, row0, rows, val):
    for j in range(SUBLANES):
        ref[pl.ds(row0 * SUBLANES + j, rows, stride=SUBLANES), :] = val[:, j * LANES:(j + 1) * LANES]


def _from_token_tiles(ref, row0, rows):
    return jnp.concatenate([ref[pl.ds(row0 * SUBLANES + j, rows, stride=SUBLANES), :] for j in range(SUBLANES)],
                           axis=1)


def _merge_kernel(x_ref, ya_ref, yb_ref, gate_ref, wa_ref, wb_ref, wo_ref, g1_ref, mod_ref, g2_ref,
                  wr_ref, br_ref, x1_ref, h2_ref, idx_ref, wt_ref):
    d = x_ref.shape[-1]
    t = TOKEN_TILE
    for sub in range(x_ref.shape[1] // t):
        tok = slice(sub * t, (sub + 1) * t)
        za = _dot(wa_ref[...], ya_ref[0, :, tok])
        zb = _dot(wb_ref[...], yb_ref[0, :, tok])
        mt = (gate_ref[0, 0:d, tok].astype(jnp.float32) * za
              + gate_ref[0, d:2 * d, tok].astype(jnp.float32) * zb)
        o = _dot(mt.T.astype(jnp.bfloat16), wo_ref[...])
        x1 = x_ref[0, tok, :] + g1_ref[0] * o
        x1_ref[0, tok, :] = x1
        h2 = _mod_norm(x1, g2_ref[...], mod_ref[0, 0:1, :], mod_ref[0, 1:2, :])
        _to_token_tiles(h2_ref, sub * t, t, h2)
        logits = _dot_nt(wr_ref[...], h2, precision=lax.Precision.HIGHEST) + br_ref[...]
        n_e = logits.shape[0]
        row = lax.broadcasted_iota(jnp.int32, logits.shape, 0)
        work = logits
        top_v, top_i = [], []
        for _ in range(TOP_K):
            mk = jnp.max(work, axis=0, keepdims=True)
            ik = jnp.min(jnp.where(work == mk, row, n_e), axis=0, keepdims=True)
            top_v.append(mk)
            top_i.append(ik)
            work = jnp.where(row == ik, -jnp.inf, work)
        ex = [jnp.exp(v - top_v[0]) for v in top_v]
        den = ex[0] + ex[1] + ex[2] + ex[3]
        idx_ref[0, :, tok] = jnp.concatenate(top_i, axis=0)
        wt_ref[0, :, tok] = jnp.concatenate([e / den for e in ex], axis=0)


def _merge(x, ya_t, yb_t, gate_t, wa_t, wb_t, wo, g1, mod2, norm2_g, wr_t, br):
    b, s, d = x.shape
    t = MERGE_TILE
    per = s // t
    assert d == SUBLANES * LANES
    full = lambda shape: pl.BlockSpec(shape, lambda bi, i: (0,) * len(shape))
    tok = pl.BlockSpec((1, t, d), lambda bi, i: (bi, i, 0))
    col = lambda rows: pl.BlockSpec((1, rows, t), lambda bi, i: (bi, 0, i))
    return pl.pallas_call(
        _merge_kernel,
        grid=(b, per),
        in_specs=[tok, col(Q_A), col(Q_B), col(2 * d),
                  full(wa_t.shape), full(wb_t.shape), full(wo.shape),
                  pl.BlockSpec((1, 1, d), lambda bi, i: (bi, 0, 0)),
                  pl.BlockSpec((1, 2, d), lambda bi, i: (bi, 0, 0)),
                  full((1, d)), full(wr_t.shape), full(br.shape)],
        out_specs=[tok, pl.BlockSpec((t * SUBLANES, LANES), lambda bi, i: (bi * per + i, 0)),
                   col(TOP_K), col(TOP_K)],
        out_shape=[jax.ShapeDtypeStruct((b, s, d), jnp.float32),
                   jax.ShapeDtypeStruct((b * s * SUBLANES, LANES), jnp.float32),
                   jax.ShapeDtypeStruct((b, TOP_K, s), jnp.int32), jax.ShapeDtypeStruct((b, TOP_K, s), jnp.float32)],
        compiler_params=_params(("arbitrary", "arbitrary")),
        name="merge_router",
    )(x, ya_t, yb_t, gate_t, wa_t, wb_t, wo, g1, mod2, norm2_g, wr_t, br)


def _w1_split_kernel(w_ref, p_ref, g_ref, l_ref):
    half = SPLIT_BLOCK // 2
    for jb in range(w_ref.shape[-1] // SPLIT_BLOCK):
        blk = w_ref[0, :, jb * SPLIT_BLOCK:(jb + 1) * SPLIT_BLOCK].astype(jnp.bfloat16)
        r = _dot(blk, p_ref[...])
        g_ref[0, :, jb * half:(jb + 1) * half] = r[:, :half].astype(jnp.bfloat16)
        l_ref[0, :, jb * half:(jb + 1) * half] = r[:, half:].astype(jnp.bfloat16)


def _w1_split(w1):
    n_e, d, d2 = w1.shape
    rows = 512
    half = SPLIT_BLOCK // 2
    perm = np.zeros((SPLIT_BLOCK, SPLIT_BLOCK), np.float32)
    for j in range(half):
        perm[2 * j, j] = 1.0
        perm[2 * j + 1, half + j] = 1.0
    out = pl.BlockSpec((1, rows, d2 // 2), lambda e, r: (e, r, 0))
    return pl.pallas_call(
        _w1_split_kernel,
        grid=(n_e, d // rows),
        in_specs=[pl.BlockSpec((1, rows, d2), lambda e, r: (e, r, 0)),
                  pl.BlockSpec((SPLIT_BLOCK, SPLIT_BLOCK), lambda e, r: (0, 0))],
        out_specs=[out, out],
        out_shape=[jax.ShapeDtypeStruct((n_e, d, d2 // 2), jnp.bfloat16)] * 2,
        compiler_params=_params(("arbitrary", "arbitrary")),
        name="w1_split",
    )(w1, jnp.asarray(perm, jnp.bfloat16))


def _moe_kernel(tile, offs_ref, tok_ref, wt_ref, h_ref, w1g_ref, w1l_ref, b1g_ref, b1l_ref, w2_ref, b2_ref,
                acc_ref, xg_ref, y_ref, pend_ref):
    ti = pl.program_id(0)
    e = pl.program_id(1)
    n_e = pl.num_programs(1)
    ch = xg_ref.shape[0] // SUBLANES
    off = ti * (n_e + 1)

    def tile_rows(r):
        return pl.ds(pl.multiple_of(r * SUBLANES, SUBLANES), SUBLANES)

    def static_rows(r):
        return slice(r * SUBLANES, (r + 1) * SUBLANES)

    def expert_mlp(xg):
        glu = jnp.minimum(_dot(xg, w1g_ref[0]) + b1g_ref[0], SWIGLU_LIMIT)
        lin = jnp.clip(_dot(xg, w1l_ref[0]) + b1l_ref[0], -SWIGLU_LIMIT, SWIGLU_LIMIT)
        act = glu * jax.nn.sigmoid(SWIGLU_ALPHA * glu) * (lin + 1.0)
        return _dot(act.astype(jnp.bfloat16), w2_ref[0]) + b2_ref[0]

    def gather_rows(dst_ref, base, n):
        def group(i, _):
            r0 = pl.multiple_of(i * ROW_UNROLL, ROW_UNROLL)
            for u in range(ROW_UNROLL):
                dst_ref[tile_rows(r0 + u), :] = h_ref[tile_rows(tok_ref[0, 0, base + r0 + u]), :]
            return 0

        lax.fori_loop(0, (n + ROW_UNROLL - 1) // ROW_UNROLL, group, 0)

    def scatter_rows(src_ref, base, n):
        def group(i, _):
            r0 = pl.multiple_of(i * ROW_UNROLL, ROW_UNROLL)
            toks = [tok_ref[0, 0, base + r0 + u] for u in range(ROW_UNROLL)]
            new = [acc_ref[tile_rows(toks[u]), :] + wt_ref[0, 0, base + r0 + u] * src_ref[tile_rows(r0 + u), :]
                   for u in range(ROW_UNROLL)]
            for u in range(ROW_UNROLL):
                acc_ref[tile_rows(toks[u]), :] = new[u]
            return 0

        full_groups = n // ROW_UNROLL
        lax.fori_loop(0, full_groups, group, 0)

        def row(r, _):
            tok = tok_ref[0, 0, base + r]
            acc_ref[tile_rows(tok), :] = acc_ref[tile_rows(tok), :] + wt_ref[0, 0, base + r] * src_ref[tile_rows(r), :]
            return 0

        lax.fori_loop(full_groups * ROW_UNROLL, n, row, 0)

    start = offs_ref[off + e]
    count = offs_ref[off + e + 1] - start
    n_blocks = jnp.maximum((count + ch - 1) // ch, 1)
    start_next = offs_ref[off + jnp.minimum(e + 1, n_e - 1)]

    @pl.when((ti == 0) & (e == 0))
    def _():
        for ref in (xg_ref, y_ref):
            ref[...] = jnp.zeros_like(ref)

    @pl.when(e == 0)
    def _():
        acc_ref[...] = jnp.zeros_like(acc_ref)
        pend_ref[0] = 0
        pend_ref[1] = 0
        gather_rows(xg_ref, start, jnp.minimum(count, ch))

    def block_body(rows, k, base_prev, n_prev):
        xg = _from_token_tiles(xg_ref, 0, rows).astype(jnp.bfloat16)

        base_next = jnp.where(k + 1 < n_blocks, start + (k + 1) * ch, start_next)
        for r in range(ch):
            xg_ref[static_rows(r), :] = h_ref[tile_rows(tok_ref[0, 0, base_next + r]), :]

        for g in range(ch // ROW_UNROLL):
            toks, new = [], []
            for u in range(ROW_UNROLL):
                r = g * ROW_UNROLL + u
                valid = r < n_prev
                tok = jnp.where(valid, tok_ref[0, 0, base_prev + r], tile + u)
                w = jnp.where(valid, wt_ref[0, 0, base_prev + r], 0.0)
                toks.append(tok)
                new.append(acc_ref[tile_rows(tok), :] + w * y_ref[static_rows(r), :])
            for u in range(ROW_UNROLL):
                acc_ref[tile_rows(toks[u]), :] = new[u]

        _to_token_tiles(y_ref, 0, rows, expert_mlp(xg))

    def block(k, pending):
        base_prev, n_prev = pending
        n = jnp.minimum(ch, count - k * ch)
        small = n <= ch // 2
        pl.when(small)(functools.partial(block_body, ch // 2, k, base_prev, n_prev))
        pl.when(jnp.logical_not(small))(functools.partial(block_body, ch, k, base_prev, n_prev))
        return start + k * ch, n

    base_prev, n_prev = lax.fori_loop(0, n_blocks, block, (pend_ref[0], pend_ref[1]))
    pend_ref[0] = base_prev
    pend_ref[1] = n_prev

    @pl.when(e == n_e - 1)
    def _():
        scatter_rows(y_ref, base_prev, n_prev)


def _moe(offs, tok_sorted, wt_sorted, h2_tiles, w1g, w1l, b1g, b1l, w2, b2, tile):
    n_tok = h2_tiles.shape[0] // SUBLANES
    n_e, d, d_e = w1g.shape
    n_t = n_tok // tile
    a = tok_sorted.shape[-1]
    assert d == SUBLANES * LANES and MOE_CHUNK % ROW_UNROLL == 0
    smem = lambda: pl.BlockSpec((1, 1, a), lambda ti, e, o: (ti, 0, 0), memory_space=pltpu.SMEM)
    ex = lambda shape: pl.BlockSpec((1,) + shape, lambda ti, e, o: (e, 0, 0))
    out_rows = (tile + ROW_UNROLL) * SUBLANES
    rows = pltpu.VMEM((MOE_CHUNK * SUBLANES, LANES), jnp.float32)
    once = pl.Buffered(1)
    grid_spec = pltpu.PrefetchScalarGridSpec(
        num_scalar_prefetch=1,
        grid=(n_t, n_e),
        in_specs=[smem(), smem(),
                  pl.BlockSpec((tile * SUBLANES, LANES), lambda ti, e, o: (ti, 0), pipeline_mode=once),
                  ex((d, d_e)), ex((d, d_e)), ex((1, d_e)), ex((1, d_e)), ex((d_e, d)), ex((1, d))],
        out_specs=pl.BlockSpec((out_rows, LANES), lambda ti, e, o: (ti, 0), pipeline_mode=once),
        scratch_shapes=[rows, rows, pltpu.SMEM((2,), jnp.int32)],
    )
    return pl.pallas_call(
        functools.partial(_moe_kernel, tile),
        grid_spec=grid_spec,
        out_shape=jax.ShapeDtypeStruct((n_t * out_rows, LANES), jnp.float32),
        compiler_params=pltpu.CompilerParams(dimension_semantics=("arbitrary", "arbitrary"),
                                             vmem_limit_bytes=MOE_VMEM_LIMIT),
        name="moe",
    )(offs, tok_sorted, wt_sorted, h2_tiles, w1g, w1l, b1g, b1l, w2, b2)


def _final_kernel(sub, x_ref, m_ref, g2_ref, g_ref, o_ref):
    t = x_ref.shape[1]
    row0 = lax.rem(pl.program_id(1), sub) * t
    x = x_ref[0] + g2_ref[0] * _from_token_tiles(m_ref, row0, t)
    ms = jnp.mean(x * x, axis=-1, keepdims=True)
    o_ref[0] = x * lax.rsqrt(ms + RMS_EPS) * g_ref[...]


def _final(x1, moe_tiles, g2, final_g, moe_tile):
    b, s, d = x1.shape
    t = min(MERGE_TILE, moe_tile)
    per = s // t
    sub = moe_tile // t
    moe_rows = (moe_tile + ROW_UNROLL) * SUBLANES
    tok = pl.BlockSpec((1, t, d), lambda bi, i: (bi, i, 0))
    return pl.pallas_call(
        functools.partial(_final_kernel, sub),
        grid=(b, per),
        in_specs=[tok, pl.BlockSpec((moe_rows, LANES), lambda bi, i: (bi * (per // sub) + i // sub, 0)),
                  pl.BlockSpec((1, 1, d), lambda bi, i: (bi, 0, 0)),
                  pl.BlockSpec((1, d), lambda bi, i: (0, 0))],
        out_specs=tok,
        out_shape=jax.ShapeDtypeStruct((b, s, d), jnp.float32),
        compiler_params=_params(("arbitrary", "arbitrary")),
        name="final_norm",
    )(x1, moe_tiles, g2, final_g)


def _rope_tables(s, n_ctx):
    inv = ROPE_THETA ** (-(jnp.arange(ROPE_QUARTER, dtype=jnp.float32) * 2.0 / ROPE_HALF))
    pos = jnp.arange(s)
    ang_r = (pos // GRID_W).astype(jnp.float32)[None, :] * inv[:, None]
    ang_c = (pos % GRID_W).astype(jnp.float32)[None, :] * inv[:, None]
    ang = jnp.concatenate([ang_r, ang_c], axis=0)
    cos = jnp.concatenate([jnp.cos(ang), jnp.ones((ROPE_HALF, n_ctx), jnp.float32)], axis=1)
    sin = jnp.concatenate([jnp.sin(ang), jnp.zeros((ROPE_HALF, n_ctx), jnp.float32)], axis=1)
    return cos, sin


def _route_lists(top_idx, top_w, tile):
    b, k, s = top_idx.shape
    n_t = b * s // tile
    per = s // tile
    e_flat = top_idx.reshape(b, k, per, tile).transpose(0, 2, 1, 3).reshape(n_t, k * tile)
    w_flat = top_w.reshape(b, k, per, tile).transpose(0, 2, 1, 3).reshape(n_t, k * tile)
    tok = jnp.tile(jnp.arange(tile, dtype=jnp.int32), k)[None, :]
    key_sorted, w_sorted = lax.sort((e_flat * tile + tok, w_flat), dimension=1, num_keys=1)
    tok_sorted = lax.rem(key_sorted, tile)
    counts = jnp.sum(e_flat[:, :, None] == jnp.arange(N_EXPERTS, dtype=jnp.int32)[None, None, :], axis=1,
                     dtype=jnp.int32)
    offs = jnp.concatenate([jnp.zeros((n_t, 1), jnp.int32), jnp.cumsum(counts, axis=1, dtype=jnp.int32)], axis=1)
    pad = ((0, 0), (0, MOE_CHUNK))
    return offs.reshape(-1), jnp.pad(tok_sorted, pad)[:, None, :], jnp.pad(w_sorted, pad)[:, None, :]


def kernel(x, c, ctx, c_ctx, w_mod, b_mod, norm1_g, norm2_g, w_in, q_norm_g, k_norm_g, sink, w_br_a, w_br_b,
           w_o, w_router, b_router, w_e1, b_e1, w_e2, b_e2, final_g):
    b, s, d = x.shape
    n_ctx = ctx.shape[1]
    assert w_mod.shape[0] == 1
    bf = jnp.bfloat16
    cos, sin = _rope_tables(s, n_ctx)
    t = TOKEN_TILE
    moe_tile = min(MOE_TILE, s)
    for l in range(1):
        rows = ((b + 1 + 7) // 8) * 8
        cond = jnp.zeros((rows, d), jnp.float32).at[:b].set(c).at[b].set(c_ctx)
        mods = _adaln(cond, w_mod[l], b_mod[l])
        sh1, sc1, g1, sh2, sc2, g2 = jnp.split(mods, 6, axis=-1)
        mod1 = jnp.stack([jnp.stack([sh1[:b], sc1[:b]], axis=1),
                          jnp.broadcast_to(jnp.stack([sh1[b], sc1[b]], axis=0)[None], (b, 2, d))], axis=1)
        mod2 = jnp.stack([sh2[:b], sc2[:b]], axis=1)

        w = w_in[l]
        kv = w[:, :4 * KV_W]
        wk_t = jnp.concatenate([kv[:, 0:KV_W], kv[:, 2 * KV_W:3 * KV_W]], axis=1).T.astype(bf)
        wv_t = jnp.concatenate([kv[:, KV_W:2 * KV_W], kv[:, 3 * KV_W:4 * KV_W]], axis=1).T.astype(bf)
        wq_t = w[:, 4 * KV_W:4 * KV_W + Q_A + Q_B].T.astype(bf)
        wg_t = w[:, 4 * KV_W + Q_A + Q_B:].T.astype(bf)
        qg = jnp.broadcast_to((q_norm_g[l] * (ATTN_SCALE * LOG2E))[:, None], (HEAD_DIM, t))
        kg = jnp.broadcast_to(k_norm_g[l][:, None], (HEAD_DIM, t))

        qa_t, qb_t, ka, kb, va_t, vb_t, gate_t = _in_proj(
            x, ctx, mod1, norm1_g[l][None], wq_t, wk_t, wv_t, wg_t, qg, kg, cos, sin)

        ya_t = _attention_global(128, 2, qa_t, ka, va_t)
        yb_t = _attention_window(sink[l], qb_t, kb, vb_t)

        x1, h2, top_idx, top_w = _merge(
            x, ya_t, yb_t, gate_t, w_br_a[l].T.astype(bf), w_br_b[l].T.astype(bf), w_o[l].astype(bf),
            g1[:b, None, :], mod2, norm2_g[l][None], w_router[l].T, b_router[l][:, None])

        offs, tok_sorted, w_sorted = _route_lists(top_idx, top_w, moe_tile)
        w1g, w1l = _w1_split(w_e1[l])
        moe_tiles = _moe(offs, tok_sorted, w_sorted, h2, w1g, w1l,
                         b_e1[l][:, None, 0::2], b_e1[l][:, None, 1::2],
                         w_e2[l].astype(bf), b_e2[l][:, None, :], moe_tile)
        out = _final(x1, moe_tiles, g2[:b, None, :], final_g[None], moe_tile)
    return out
```

```python
import functools

import numpy as np
import jax
import jax.numpy as jnp
from jax import lax
from jax.experimental import pallas as pl
from jax.experimental.pallas import tpu as pltpu

HEAD_DIM = 64
KV_HEADS = 2
A_HEADS = 8
B_HEADS = 8
GROUP = A_HEADS // KV_HEADS
GRID_W = 64
WINDOW = 128
ROPE_HALF = HEAD_DIM // 2
ROPE_QUARTER = ROPE_HALF // 2
ROPE_THETA = 10000.0
N_EXPERTS = 32
TOP_K = 4
SWIGLU_ALPHA = 1.702
SWIGLU_LIMIT = 7.0
RMS_EPS = 1e-6
ATTN_SCALE = HEAD_DIM ** -0.5
LOG2E = 1.4426950408889634
NEG_INF = -1e30
KV_W = KV_HEADS * HEAD_DIM
Q_A = A_HEADS * HEAD_DIM
Q_B = B_HEADS * HEAD_DIM
V_ROWS = HEAD_DIM + 16

SUBLANES = 8
LANES = 128
TOKEN_TILE = 256
MERGE_TILE = 512
MOE_TILE = 4096
MOE_CHUNK = 256
MOE_VMEM_LIMIT = 60 * 1024 * 1024
ROW_UNROLL = 8
SPLIT_BLOCK = 256
VMEM_LIMIT = 56 * 1024 * 1024

_NT = (((1,), (1,)), ((), ()))


def _dot_nt(a, b, precision=None):
    return lax.dot_general(a, b, _NT, preferred_element_type=jnp.float32, precision=precision)


def _dot(a, b, precision=None):
    return jnp.dot(a, b, preferred_element_type=jnp.float32, precision=precision)


def _params(sem):
    return pltpu.CompilerParams(dimension_semantics=sem, vmem_limit_bytes=VMEM_LIMIT)


def _adaln_kernel(cond_ref, w_ref, b_ref, o_ref):
    cond = cond_ref[...]
    act = cond * jax.nn.sigmoid(cond)
    o_ref[...] = _dot(act, w_ref[...], precision=lax.Precision.HIGHEST) + b_ref[...]


def _adaln(cond, w, b):
    rows, d = cond.shape
    n = w.shape[1]
    tn = 1024
    return pl.pallas_call(
        _adaln_kernel,
        grid=(n // tn,),
        in_specs=[pl.BlockSpec((rows, d), lambda j: (0, 0)),
                  pl.BlockSpec((d, tn), lambda j: (0, j)),
                  pl.BlockSpec((1, tn), lambda j: (0, j))],
        out_specs=pl.BlockSpec((rows, tn), lambda j: (0, j)),
        out_shape=jax.ShapeDtypeStruct((rows, n), jnp.float32),
        compiler_params=_params(("arbitrary",)),
        name="adaln",
    )(cond, w, b.reshape(1, n))


def _rope_t(xh, cos, sin):
    q = ROPE_QUARTER
    a0, a1, b0, b1 = xh[0:q], xh[q:2 * q], xh[2 * q:3 * q], xh[3 * q:4 * q]
    cr, cc = cos[0:q], cos[q:2 * q]
    sr, sc = sin[0:q], sin[q:2 * q]
    return jnp.concatenate([a0 * cr - a1 * sr, a1 * cr + a0 * sr,
                            b0 * cc - b1 * sc, b1 * cc + b0 * sc], axis=0)


def _head_norm_t(xh, g):
    ms = jnp.mean(xh * xh, axis=0, keepdims=True)
    return xh * lax.rsqrt(ms + RMS_EPS) * g


def _mod_norm(x, g, shift, scale):
    ms = jnp.mean(x * x, axis=-1, keepdims=True)
    return x * lax.rsqrt(ms + RMS_EPS) * g * (1.0 + scale) + shift


def _in_proj_kernel(n_lat, x_ref, ctx_ref, mod_ref, g_ref, wq_ref, wk_ref, wv_ref, wg_ref,
                    qg_ref, kg_ref, cos_ref, sin_ref,
                    qa_ref, qb_ref, ka_ref, kb_ref, va_ref, vb_ref, gate_ref, h_ref):
    i = pl.program_id(1)
    shift = mod_ref[0, 0, 0:1, :]
    scale = mod_ref[0, 0, 1:2, :]

    @pl.when(i < n_lat)
    def _():
        h_ref[...] = _mod_norm(x_ref[0], g_ref[...], shift, scale).astype(jnp.bfloat16)

    @pl.when(i >= n_lat)
    def _():
        h_ref[...] = _mod_norm(ctx_ref[0], g_ref[...], shift, scale).astype(jnp.bfloat16)

    h = h_ref[...]
    cos = cos_ref[...]
    sin = sin_ref[...]
    t = h.shape[0]

    kt = _dot_nt(wk_ref[...], h)
    kg = kg_ref[...]
    ka = [_rope_t(_head_norm_t(kt[j * HEAD_DIM:(j + 1) * HEAD_DIM], kg), cos, sin) for j in range(KV_HEADS)]
    kb = [_rope_t(kt[KV_W + j * HEAD_DIM:KV_W + (j + 1) * HEAD_DIM], cos, sin) for j in range(KV_HEADS)]
    ka_ref[0, 0] = jnp.concatenate(ka, axis=0).T.astype(jnp.bfloat16)
    kb_ref[0, 0] = jnp.concatenate(kb, axis=0).T.astype(jnp.bfloat16)
    vt = _dot_nt(wv_ref[...], h)
    ones = jnp.ones((V_ROWS - HEAD_DIM, t), jnp.float32)
    for br, v_ref in enumerate((va_ref, vb_ref)):
        rows = []
        for j in range(KV_HEADS):
            rows += [vt[br * KV_W + j * HEAD_DIM:br * KV_W + (j + 1) * HEAD_DIM], ones]
        v_ref[0, 0] = jnp.concatenate(rows, axis=0).astype(jnp.bfloat16)

    @pl.when(i < n_lat)
    def _():
        qg = qg_ref[...]
        for half in range(2):
            qt = _dot_nt(wq_ref[half * Q_A:(half + 1) * Q_A, :], h)
            for hd in range(A_HEADS):
                xh = qt[hd * HEAD_DIM:(hd + 1) * HEAD_DIM]
                if half == 0:
                    out = _rope_t(_head_norm_t(xh, qg), cos, sin)
                    qa_ref[0, hd * HEAD_DIM:(hd + 1) * HEAD_DIM, :] = out.astype(jnp.bfloat16)
                else:
                    out = _rope_t(xh * (ATTN_SCALE * LOG2E), cos, sin)
                    qb_ref[0, hd * HEAD_DIM:(hd + 1) * HEAD_DIM, :] = out.astype(jnp.bfloat16)
        rows = 512
        for c in range(wg_ref.shape[0] // rows):
            gt = _dot_nt(wg_ref[c * rows:(c + 1) * rows, :], h)
            gate_ref[0, c * rows:(c + 1) * rows, :] = jax.nn.sigmoid(gt).astype(jnp.bfloat16)


def _in_proj(x, ctx, mod, norm_g, wq_t, wk_t, wv_t, wg_t, qg, kg, cos, sin):
    b, s, d = x.shape
    n_ctx = ctx.shape[1]
    t = TOKEN_TILE
    n_lat = s // t
    n_tiles = n_lat + n_ctx // t
    last = n_lat - 1
    full = lambda shape: pl.BlockSpec(shape, lambda bi, i: (0,) * len(shape))
    q_spec = pl.BlockSpec((1, Q_A, t), lambda bi, i: (bi, 0, jnp.minimum(i, last)))
    k_spec = pl.BlockSpec((1, 1, t, KV_W), lambda bi, i: (bi, i, 0, 0))
    v_spec = pl.BlockSpec((1, 1, KV_HEADS * V_ROWS, t), lambda bi, i: (bi, i, 0, 0))
    bf = jnp.bfloat16
    return pl.pallas_call(
        functools.partial(_in_proj_kernel, n_lat),
        grid=(b, n_tiles),
        in_specs=[pl.BlockSpec((1, t, d), lambda bi, i: (bi, jnp.minimum(i, last), 0)),
                  pl.BlockSpec((1, t, d), lambda bi, i: (bi, jnp.maximum(i - n_lat, 0), 0)),
                  pl.BlockSpec((1, 1, 2, d), lambda bi, i: (bi, i // n_lat, 0, 0)),
                  full((1, d)), full(wq_t.shape), full(wk_t.shape), full(wv_t.shape), full(wg_t.shape),
                  full((HEAD_DIM, t)), full((HEAD_DIM, t)),
                  pl.BlockSpec((ROPE_HALF, t), lambda bi, i: (0, i)),
                  pl.BlockSpec((ROPE_HALF, t), lambda bi, i: (0, i))],
        out_specs=[q_spec, q_spec, k_spec, k_spec, v_spec, v_spec,
                   pl.BlockSpec((1, 2 * d, t), lambda bi, i: (bi, 0, jnp.minimum(i, last)))],
        out_shape=[jax.ShapeDtypeStruct((b, Q_A, s), bf), jax.ShapeDtypeStruct((b, Q_B, s), bf),
                   jax.ShapeDtypeStruct((b, n_tiles, t, KV_W), bf), jax.ShapeDtypeStruct((b, n_tiles, t, KV_W), bf),
                   jax.ShapeDtypeStruct((b, n_tiles, KV_HEADS * V_ROWS, t), bf),
                   jax.ShapeDtypeStruct((b, n_tiles, KV_HEADS * V_ROWS, t), bf),
                   jax.ShapeDtypeStruct((b, 2 * d, s), bf)],
        scratch_shapes=[pltpu.VMEM((t, d), bf)],
        compiler_params=_params(("arbitrary", "arbitrary")),
        name="in_proj",
    )(x, ctx, mod, norm_g, wq_t, wk_t, wv_t, wg_t, qg, kg, cos, sin)


def _attn_window_kernel(subs, n_lat, n_tiles, sink_ref, q_ref, k_ref, v_ref, y_ref, s_ref):
    t = tq = TOKEN_TILE
    n = GROUP * tq
    assert WINDOW * 2 == t
    heads = [[kvh * GROUP + g for g in range(GROUP)] for kvh in range(KV_HEADS)]
    n_slots = s_ref.shape[1] // subs
    sink = [jnp.concatenate([jnp.full((1, tq), sink_ref[hd] * LOG2E, jnp.float32) for hd in heads[kvh]], axis=1)
            for kvh in range(KV_HEADS)]
    c_half = lax.rem(lax.broadcasted_iota(jnp.int32, (WINDOW, n), 1), tq)
    r_half = lax.broadcasted_iota(jnp.int32, (WINDOW, n), 0)
    c_full = lax.rem(lax.broadcasted_iota(jnp.int32, (t, n), 1), tq)
    r_full = lax.broadcasted_iota(jnp.int32, (t, n), 0)
    acc_row = lax.broadcasted_iota(jnp.int32, (V_ROWS, n), 0)

    for sub in range(subs):
        qi = pl.program_id(1) * subs + sub
        cols = slice(sub * tq, (sub + 1) * tq)
        qpad = []
        for kvh in range(KV_HEADS):
            qcat = jnp.concatenate([q_ref[0, hd * HEAD_DIM:(hd + 1) * HEAD_DIM, cols] for hd in heads[kvh]], axis=1)
            zeros = jnp.zeros_like(qcat)
            qpad.append(jnp.concatenate([qcat, zeros] if kvh == 0 else [zeros, qcat], axis=0))

        off_prev = jnp.where(qi >= 1, 0, 2 * t)
        off_next = jnp.where(qi + 1 < n_lat, 0, 2 * t)
        pieces = [
            (jnp.maximum(qi - 1, 0), WINDOW, WINDOW, r_half >= c_half + off_prev),
            (qi, 0, t, jnp.abs(r_full - c_full) <= WINDOW),
            (jnp.minimum(qi + 1, n_lat - 1), 0, WINDOW, c_half - r_half >= WINDOW + off_next),
        ] + [(kb, 0, t, None) for kb in range(n_lat, n_tiles)]

        m8 = [jnp.broadcast_to(sink[kvh], (8, n)) for kvh in range(KV_HEADS)]
        for slot, (kb, lo, rows, ok) in enumerate(pieces):
            kblk = k_ref[0, kb, lo:lo + rows, :]
            for kvh in range(KV_HEADS):
                s = _dot(kblk, qpad[kvh])
                if ok is not None:
                    s = jnp.where(ok, s, NEG_INF)
                s_ref[kvh, sub * n_slots + slot, 0:rows, :] = s
                m8[kvh] = jnp.maximum(m8[kvh], jnp.max(s.reshape(rows // 8, 8, n), axis=0))
        m = [jnp.max(m8[kvh], axis=0, keepdims=True) for kvh in range(KV_HEADS)]

        accs = [jnp.where(acc_row == HEAD_DIM, jnp.exp2(sink[kvh] - m[kvh]), 0.0) for kvh in range(KV_HEADS)]
        for slot, (kb, lo, rows, _) in enumerate(pieces):
            for kvh in range(KV_HEADS):
                p = jnp.exp2(s_ref[kvh, sub * n_slots + slot, 0:rows, :] - m[kvh]).astype(jnp.bfloat16)
                vblk = v_ref[0, kb, kvh * V_ROWS:(kvh + 1) * V_ROWS, lo:lo + rows]
                accs[kvh] = accs[kvh] + _dot(vblk, p)
        for kvh in range(KV_HEADS):
            out = accs[kvh][0:HEAD_DIM] / accs[kvh][HEAD_DIM:HEAD_DIM + 1]
            for g, hd in enumerate(heads[kvh]):
                y_ref[0, hd * HEAD_DIM:(hd + 1) * HEAD_DIM, cols] = out[:, g * tq:(g + 1) * tq].astype(jnp.bfloat16)


def _attn_global_kernel(tq, n_tiles, n_q, unroll, q_ref, k_ref, v_ref, y_ref, s0_ref, s1_ref, m0_ref, m1_ref):
    j = pl.program_id(1)
    t = TOKEN_TILE
    n = GROUP * tq
    heads = [[kvh * GROUP + g for g in range(GROUP)] for kvh in range(KV_HEADS)]
    bufs = ((s0_ref, m0_ref), (s1_ref, m1_ref))

    def padded_q():
        qpad = []
        for kvh in range(KV_HEADS):
            qcat = jnp.concatenate([q_ref[0, hd * HEAD_DIM:(hd + 1) * HEAD_DIM, :] for hd in heads[kvh]], axis=1)
            zeros = jnp.zeros_like(qcat)
            qpad.append(jnp.concatenate([qcat, zeros] if kvh == 0 else [zeros, qcat], axis=0))
        return qpad

    def scores(s_ref, kb, m8, qpad):
        kblk = k_ref[0, kb]
        out = []
        for kvh in range(KV_HEADS):
            s = _dot(kblk, qpad[kvh])
            s_ref[kvh, kb] = s
            out.append(jnp.maximum(m8[kvh], jnp.max(s.reshape(t // 8, 8, n), axis=0)))
        return tuple(out)

    def weigh(s_ref, kb, m, accs):
        out = []
        for kvh in range(KV_HEADS):
            p = jnp.exp2(s_ref[kvh, kb] - m[kvh]).astype(jnp.bfloat16)
            vblk = v_ref[0, kb, kvh * V_ROWS:(kvh + 1) * V_ROWS, :]
            out.append(accs[kvh] + _dot(vblk, p))
        return tuple(out)

    def keep_maxima(m_ref, m8):
        for kvh in range(KV_HEADS):
            m_ref[kvh] = m8[kvh]

    def maxima(m_ref):
        return [jnp.max(m_ref[kvh], axis=0, keepdims=True) for kvh in range(KV_HEADS)]

    def write(accs):
        for kvh in range(KV_HEADS):
            out = accs[kvh][0:HEAD_DIM] / accs[kvh][HEAD_DIM:HEAD_DIM + 1]
            for g, hd in enumerate(heads[kvh]):
                y_ref[0, hd * HEAD_DIM:(hd + 1) * HEAD_DIM, :] = out[:, g * tq:(g + 1) * tq].astype(jnp.bfloat16)

    m8_0 = tuple(jnp.full((8, n), -jnp.inf, jnp.float32) for _ in range(KV_HEADS))
    acc_0 = tuple(jnp.zeros((V_ROWS, n), jnp.float32) for _ in range(KV_HEADS))

    @pl.when(j == 0)
    def _():
        s_w, m_w = bufs[0]
        qpad = padded_q()
        keep_maxima(m_w, lax.fori_loop(0, n_tiles, lambda kb, c: scores(s_w, kb, c, qpad), m8_0, unroll=unroll))

    for parity in range(2):
        (s_w, m_w), (s_r, m_r) = bufs[parity], bufs[1 - parity]

        @pl.when((j > 0) & (j < n_q) & (lax.rem(j, 2) == parity))
        def _():
            qpad = padded_q()
            m = maxima(m_r)
            m8, accs = lax.fori_loop(
                0, n_tiles, lambda kb, c: (scores(s_w, kb, c[0], qpad), weigh(s_r, kb, m, c[1])),
                (m8_0, acc_0), unroll=unroll)
            keep_maxima(m_w, m8)
            write(accs)

    @pl.when(j == n_q)
    def _():
        s_r, m_r = bufs[(n_q - 1) % 2]
        m = maxima(m_r)
        write(lax.fori_loop(0, n_tiles, lambda kb, c: weigh(s_r, kb, m, c), acc_0, unroll=unroll))


def _attention_global(tq, unroll, q_t, k, v_t):
    b, hq, s = q_t.shape
    n_tiles = k.shape[1]
    n_q = s // tq
    n = GROUP * tq
    return pl.pallas_call(
        functools.partial(_attn_global_kernel, tq, n_tiles, n_q, unroll),
        grid=(b, n_q + 1),
        in_specs=[pl.BlockSpec((1, hq, tq), lambda bi, j: (bi, 0, jnp.minimum(j, n_q - 1))),
                  pl.BlockSpec((1,) + k.shape[1:], lambda bi, j: (bi, 0, 0, 0)),
                  pl.BlockSpec((1,) + v_t.shape[1:], lambda bi, j: (bi, 0, 0, 0))],
        out_specs=pl.BlockSpec((1, hq, tq), lambda bi, j: (bi, 0, jnp.maximum(j - 1, 0))),
        out_shape=jax.ShapeDtypeStruct((b, hq, s), jnp.bfloat16),
        scratch_shapes=[pltpu.VMEM((KV_HEADS, n_tiles, TOKEN_TILE, n), jnp.float32)] * 2
        + [pltpu.VMEM((KV_HEADS, 8, n), jnp.float32)] * 2,
        compiler_params=_params(("arbitrary", "arbitrary")),
        name="attn_global",
    )(q_t, k, v_t)


def _attention_window(sink, q_t, k, v_t):
    b, hq, s = q_t.shape
    n_tiles = k.shape[1]
    t = TOKEN_TILE
    n_lat = s // t
    subs = 2 if n_lat % 2 == 0 else 1
    slots = 3 + (n_tiles - n_lat)
    grid_spec = pltpu.PrefetchScalarGridSpec(
        num_scalar_prefetch=1,
        grid=(b, n_lat // subs),
        in_specs=[pl.BlockSpec((1, hq, subs * t), lambda bi, qi, sk: (bi, 0, qi)),
                  pl.BlockSpec((1,) + k.shape[1:], lambda bi, qi, sk: (bi, 0, 0, 0)),
                  pl.BlockSpec((1,) + v_t.shape[1:], lambda bi, qi, sk: (bi, 0, 0, 0))],
        out_specs=pl.BlockSpec((1, hq, subs * t), lambda bi, qi, sk: (bi, 0, qi)),
        scratch_shapes=[pltpu.VMEM((KV_HEADS, subs * slots, t, GROUP * t), jnp.float32)],
    )
    return pl.pallas_call(
        functools.partial(_attn_window_kernel, subs, n_lat, n_tiles),
        grid_spec=grid_spec,
        out_shape=jax.ShapeDtypeStruct((b, hq, s), jnp.bfloat16),
        compiler_params=_params(("arbitrary", "arbitrary")),
        name="attn_window",
    )(sink, q_t, k, v_t)


def _to_token_tiles(ref, row0, rows, val):
    for j in range(SUBLANES):
        ref[pl.ds(row0 * SUBLANES + j, rows, stride=SUBLANES), :] = val[:, j * LANES:(j + 1) * LANES]


def _from_token_tiles(ref, row0, rows):
    return jnp.concatenate([ref[pl.ds(row0 * SUBLANES + j, rows, stride=SUBLANES), :] for j in range(SUBLANES)],
                           axis=1)


def _merge_kernel(x_ref, ya_ref, yb_ref, gate_ref, wa_ref, wb_ref, wo_ref, g1_ref, mod_ref, g2_ref,
                  wr_ref, br_ref, x1_ref, h2_ref, idx_ref, wt_ref):
    d = x_ref.shape[-1]
    t = TOKEN_TILE
    for sub in range(x_ref.shape[1] // t):
        tok = slice(sub * t, (sub + 1) * t)
        za = _dot(wa_ref[...], ya_ref[0, :, tok])
        zb = _dot(wb_ref[...], yb_ref[0, :, tok])
        mt = (gate_ref[0, 0:d, tok].astype(jnp.float32) * za
              + gate_ref[0, d:2 * d, tok].astype(jnp.float32) * zb)
        o = _dot(mt.T.astype(jnp.bfloat16), wo_ref[...])
        x1 = x_ref[0, tok, :] + g1_ref[0] * o
        x1_ref[0, tok, :] = x1
        h2 = _mod_norm(x1, g2_ref[...], mod_ref[0, 0:1, :], mod_ref[0, 1:2, :])
        _to_token_tiles(h2_ref, sub * t, t, h2)
        logits = _dot_nt(wr_ref[...], h2, precision=lax.Precision.HIGHEST) + br_ref[...]
        n_e = logits.shape[0]
        row = lax.broadcasted_iota(jnp.int32, logits.shape, 0)
        work = logits
        top_v, top_i = [], []
        for _ in range(TOP_K):
            mk = jnp.max(work, axis=0, keepdims=True)
            ik = jnp.min(jnp.where(work == mk, row, n_e), axis=0, keepdims=True)
            top_v.append(mk)
            top_i.append(ik)
            work = jnp.where(row == ik, -jnp.inf, work)
        ex = [jnp.exp(v - top_v[0]) for v in top_v]
        den = ex[0] + ex[1] + ex[2] + ex[3]
        idx_ref[0, :, tok] = jnp.concatenate(top_i, axis=0)
        wt_ref[0, :, tok] = jnp.concatenate([e / den for e in ex], axis=0)


def _merge(x, ya_t, yb_t, gate_t, wa_t, wb_t, wo, g1, mod2, norm2_g, wr_t, br):
    b, s, d = x.shape
    t = MERGE_TILE
    per = s // t
    assert d == SUBLANES * LANES
    full = lambda shape: pl.BlockSpec(shape, lambda bi, i: (0,) * len(shape))
    tok = pl.BlockSpec((1, t, d), lambda bi, i: (bi, i, 0))
    col = lambda rows: pl.BlockSpec((1, rows, t), lambda bi, i: (bi, 0, i))
    return pl.pallas_call(
        _merge_kernel,
        grid=(b, per),
        in_specs=[tok, col(Q_A), col(Q_B), col(2 * d),
                  full(wa_t.shape), full(wb_t.shape), full(wo.shape),
                  pl.BlockSpec((1, 1, d), lambda bi, i: (bi, 0, 0)),
                  pl.BlockSpec((1, 2, d), lambda bi, i: (bi, 0, 0)),
                  full((1, d)), full(wr_t.shape), full(br.shape)],
        out_specs=[tok, pl.BlockSpec((t * SUBLANES, LANES), lambda bi, i: (bi * per + i, 0)),
                   col(TOP_K), col(TOP_K)],
        out_shape=[jax.ShapeDtypeStruct((b, s, d), jnp.float32),
                   jax.ShapeDtypeStruct((b * s * SUBLANES, LANES), jnp.float32),
                   jax.ShapeDtypeStruct((b, TOP_K, s), jnp.int32), jax.ShapeDtypeStruct((b, TOP_K, s), jnp.float32)],
        compiler_params=_params(("arbitrary", "arbitrary")),
        name="merge_router",
    )(x, ya_t, yb_t, gate_t, wa_t, wb_t, wo, g1, mod2, norm2_g, wr_t, br)


def _w1_split_kernel(w_ref, p_ref, g_ref, l_ref):
    half = SPLIT_BLOCK // 2
    for jb in range(w_ref.shape[-1] // SPLIT_BLOCK):
        blk = w_ref[0, :, jb * SPLIT_BLOCK:(jb + 1) * SPLIT_BLOCK].astype(jnp.bfloat16)
        r = _dot(blk, p_ref[...])
        g_ref[0, :, jb * half:(jb + 1) * half] = r[:, :half].astype(jnp.bfloat16)
        l_ref[0, :, jb * half:(jb + 1) * half] = r[:, half:].astype(jnp.bfloat16)


def _w1_split(w1):
    n_e, d, d2 = w1.shape
    rows = 512
    half = SPLIT_BLOCK // 2
    perm = np.zeros((SPLIT_BLOCK, SPLIT_BLOCK), np.float32)
    for j in range(half):
        perm[2 * j, j] = 1.0
        perm[2 * j + 1, half + j] = 1.0
    out = pl.BlockSpec((1, rows, d2 // 2), lambda e, r: (e, r, 0))
    return pl.pallas_call(
        _w1_split_kernel,
        grid=(n_e, d // rows),
        in_specs=[pl.BlockSpec((1, rows, d2), lambda e, r: (e, r, 0)),
                  pl.BlockSpec((SPLIT_BLOCK, SPLIT_BLOCK), lambda e, r: (0, 0))],
        out_specs=[out, out],
        out_shape=[jax.ShapeDtypeStruct((n_e, d, d2 // 2), jnp.bfloat16)] * 2,
        compiler_params=_params(("arbitrary", "arbitrary")),
        name="w1_split",
    )(w1, jnp.asarray(perm, jnp.bfloat16))


def _moe_kernel(tile, offs_ref, tok_ref, wt_ref, h_ref, w1g_ref, w1l_ref, b1g_ref, b1l_ref, w2_ref, b2_ref,
                acc_ref, xg_ref, y_ref, pend_ref):
    ti = pl.program_id(0)
    e = pl.program_id(1)
    n_e = pl.num_programs(1)
    ch = xg_ref.shape[0] // SUBLANES
    off = ti * (n_e + 1)

    def tile_rows(r):
        return pl.ds(pl.multiple_of(r * SUBLANES, SUBLANES), SUBLANES)

    def static_rows(r):
        return slice(r * SUBLANES, (r + 1) * SUBLANES)

    def expert_mlp(xg):
        glu = jnp.minimum(_dot(xg, w1g_ref[0]) + b1g_ref[0], SWIGLU_LIMIT)
        lin = jnp.clip(_dot(xg, w1l_ref[0]) + b1l_ref[0], -SWIGLU_LIMIT, SWIGLU_LIMIT)
        act = glu * jax.nn.sigmoid(SWIGLU_ALPHA * glu) * (lin + 1.0)
        return _dot(act.astype(jnp.bfloat16), w2_ref[0]) + b2_ref[0]

    def gather_rows(dst_ref, base, n):
        def group(i, _):
            r0 = pl.multiple_of(i * ROW_UNROLL, ROW_UNROLL)
            for u in range(ROW_UNROLL):
                dst_ref[tile_rows(r0 + u), :] = h_ref[tile_rows(tok_ref[0, 0, base + r0 + u]), :]
            return 0

        lax.fori_loop(0, (n + ROW_UNROLL - 1) // ROW_UNROLL, group, 0)

    def scatter_rows(src_ref, base, n):
        def group(i, _):
            r0 = pl.multiple_of(i * ROW_UNROLL, ROW_UNROLL)
            toks = [tok_ref[0, 0, base + r0 + u] for u in range(ROW_UNROLL)]
            new = [acc_ref[tile_rows(toks[u]), :] + wt_ref[0, 0, base + r0 + u] * src_ref[tile_rows(r0 + u), :]
                   for u in range(ROW_UNROLL)]
            for u in range(ROW_UNROLL):
                acc_ref[tile_rows(toks[u]), :] = new[u]
            return 0

        full_groups = n // ROW_UNROLL
        lax.fori_loop(0, full_groups, group, 0)

        def row(r, _):
            tok = tok_ref[0, 0, base + r]
            acc_ref[tile_rows(tok), :] = acc_ref[tile_rows(tok), :] + wt_ref[0, 0, base + r] * src_ref[tile_rows(r), :]
            return 0

        lax.fori_loop(full_groups * ROW_UNROLL, n, row, 0)

    start = offs_ref[off + e]
    count = offs_ref[off + e + 1] - start
    n_blocks = jnp.maximum((count + ch - 1) // ch, 1)
    start_next = offs_ref[off + jnp.minimum(e + 1, n_e - 1)]

    @pl.when((ti == 0) & (e == 0))
    def _():
        for ref in (xg_ref, y_ref):
            ref[...] = jnp.zeros_like(ref)

    @pl.when(e == 0)
    def _():
        acc_ref[...] = jnp.zeros_like(acc_ref)
        pend_ref[0] = 0
        pend_ref[1] = 0
        gather_rows(xg_ref, start, jnp.minimum(count, ch))

    def block_body(rows, k, base_prev, n_prev):
        xg = _from_token_tiles(xg_ref, 0, rows).astype(jnp.bfloat16)

        base_next = jnp.where(k + 1 < n_blocks, start + (k + 1) * ch, start_next)
        for r in range(ch):
            xg_ref[static_rows(r), :] = h_ref[tile_rows(tok_ref[0, 0, base_next + r]), :]

        for g in range(ch // ROW_UNROLL):
            toks, new = [], []
            for u in range(ROW_UNROLL):
                r = g * ROW_UNROLL + u
                valid = r < n_prev
                tok = jnp.where(valid, tok_ref[0, 0, base_prev + r], tile + u)
                w = jnp.where(valid, wt_ref[0, 0, base_prev + r], 0.0)
                toks.append(tok)
                new.append(acc_ref[tile_rows(tok), :] + w * y_ref[static_rows(r), :])
            for u in range(ROW_UNROLL):
                acc_ref[tile_rows(toks[u]), :] = new[u]

        _to_token_tiles(y_ref, 0, rows, expert_mlp(xg))

    def block(k, pending):
        base_prev, n_prev = pending
        n = jnp.minimum(ch, count - k * ch)
        small = n <= ch // 2
        pl.when(small)(functools.partial(block_body, ch // 2, k, base_prev, n_prev))
        pl.when(jnp.logical_not(small))(functools.partial(block_body, ch, k, base_prev, n_prev))
        return start + k * ch, n

    base_prev, n_prev = lax.fori_loop(0, n_blocks, block, (pend_ref[0], pend_ref[1]))
    pend_ref[0] = base_prev
    pend_ref[1] = n_prev

    @pl.when(e == n_e - 1)
    def _():
        scatter_rows(y_ref, base_prev, n_prev)


def _moe(offs, tok_sorted, wt_sorted, h2_tiles, w1g, w1l, b1g, b1l, w2, b2, tile):
    n_tok = h2_tiles.shape[0] // SUBLANES
    n_e, d, d_e = w1g.shape
    n_t = n_tok // tile
    a = tok_sorted.shape[-1]
    assert d == SUBLANES * LANES and MOE_CHUNK % ROW_UNROLL == 0
    smem = lambda: pl.BlockSpec((1, 1, a), lambda ti, e, o: (ti, 0, 0), memory_space=pltpu.SMEM)
    ex = lambda shape: pl.BlockSpec((1,) + shape, lambda ti, e, o: (e, 0, 0))
    out_rows = (tile + ROW_UNROLL) * SUBLANES
    rows = pltpu.VMEM((MOE_CHUNK * SUBLANES, LANES), jnp.float32)
    once = pl.Buffered(1)
    grid_spec = pltpu.PrefetchScalarGridSpec(
        num_scalar_prefetch=1,
        grid=(n_t, n_e),
        in_specs=[smem(), smem(),
                  pl.BlockSpec((tile * SUBLANES, LANES), lambda ti, e, o: (ti, 0), pipeline_mode=once),
                  ex((d, d_e)), ex((d, d_e)), ex((1, d_e)), ex((1, d_e)), ex((d_e, d)), ex((1, d))],
        out_specs=pl.BlockSpec((out_rows, LANES), lambda ti, e, o: (ti, 0), pipeline_mode=once),
        scratch_shapes=[rows, rows, pltpu.SMEM((2,), jnp.int32)],
    )
    return pl.pallas_call(
        functools.partial(_moe_kernel, tile),
        grid_spec=grid_spec,
        out_shape=jax.ShapeDtypeStruct((n_t * out_rows, LANES), jnp.float32),
        compiler_params=pltpu.CompilerParams(dimension_semantics=("arbitrary", "arbitrary"),
                                             vmem_limit_bytes=MOE_VMEM_LIMIT),
        name="moe",
    )(offs, tok_sorted, wt_sorted, h2_tiles, w1g, w1l, b1g, b1l, w2, b2)


def _final_kernel(sub, x_ref, m_ref, g2_ref, g_ref, o_ref):
    t = x_ref.shape[1]
    row0 = lax.rem(pl.program_id(1), sub) * t
    x = x_ref[0] + g2_ref[0] * _from_token_tiles(m_ref, row0, t)
    ms = jnp.mean(x * x, axis=-1, keepdims=True)
    o_ref[0] = x * lax.rsqrt(ms + RMS_EPS) * g_ref[...]


def _final(x1, moe_tiles, g2, final_g, moe_tile):
    b, s, d = x1.shape
    t = min(MERGE_TILE, moe_tile)
    per = s // t
    sub = moe_tile // t
    moe_rows = (moe_tile + ROW_UNROLL) * SUBLANES
    tok = pl.BlockSpec((1, t, d), lambda bi, i: (bi, i, 0))
    return pl.pallas_call(
        functools.partial(_final_kernel, sub),
        grid=(b, per),
        in_specs=[tok, pl.BlockSpec((moe_rows, LANES), lambda bi, i: (bi * (per // sub) + i // sub, 0)),
                  pl.BlockSpec((1, 1, d), lambda bi, i: (bi, 0, 0)),
                  pl.BlockSpec((1, d), lambda bi, i: (0, 0))],
        out_specs=tok,
        out_shape=jax.ShapeDtypeStruct((b, s, d), jnp.float32),
        compiler_params=_params(("arbitrary", "arbitrary")),
        name="final_norm",
    )(x1, moe_tiles, g2, final_g)


def _rope_tables(s, n_ctx):
    inv = ROPE_THETA ** (-(jnp.arange(ROPE_QUARTER, dtype=jnp.float32) * 2.0 / ROPE_HALF))
    pos = jnp.arange(s)
    ang_r = (pos // GRID_W).astype(jnp.float32)[None, :] * inv[:, None]
    ang_c = (pos % GRID_W).astype(jnp.float32)[None, :] * inv[:, None]
    ang = jnp.concatenate([ang_r, ang_c], axis=0)
    cos = jnp.concatenate([jnp.cos(ang), jnp.ones((ROPE_HALF, n_ctx), jnp.float32)], axis=1)
    sin = jnp.concatenate([jnp.sin(ang), jnp.zeros((ROPE_HALF, n_ctx), jnp.float32)], axis=1)
    return cos, sin


def _route_lists(top_idx, top_w, tile):
    b, k, s = top_idx.shape
    n_t = b * s // tile
    per = s // tile
    e_flat = top_idx.reshape(b, k, per, tile).transpose(0, 2, 1, 3).reshape(n_t, k * tile)
    w_flat = top_w.reshape(b, k, per, tile).transpose(0, 2, 1, 3).reshape(n_t, k * tile)
    tok = jnp.tile(jnp.arange(tile, dtype=jnp.int32), k)[None, :]
    key_sorted, w_sorted = lax.sort((e_flat * tile + tok, w_flat), dimension=1, num_keys=1)
    tok_sorted = lax.rem(key_sorted, tile)
    counts = jnp.sum(e_flat[:, :, None] == jnp.arange(N_EXPERTS, dtype=jnp.int32)[None, None, :], axis=1,
                     dtype=jnp.int32)
    offs = jnp.concatenate([jnp.zeros((n_t, 1), jnp.int32), jnp.cumsum(counts, axis=1, dtype=jnp.int32)], axis=1)
    pad = ((0, 0), (0, MOE_CHUNK))
    return offs.reshape(-1), jnp.pad(tok_sorted, pad)[:, None, :], jnp.pad(w_sorted, pad)[:, None, :]


def kernel(x, c, ctx, c_ctx, w_mod, b_mod, norm1_g, norm2_g, w_in, q_norm_g, k_norm_g, sink, w_br_a, w_br_b,
           w_o, w_router, b_router, w_e1, b_e1, w_e2, b_e2, final_g):
    b, s, d = x.shape
    n_ctx = ctx.shape[1]
    assert w_mod.shape[0] == 1
    bf = jnp.bfloat16
    cos, sin = _rope_tables(s, n_ctx)
    t = TOKEN_TILE
    moe_tile = min(MOE_TILE, s)
    for l in range(1):
        rows = ((b + 1 + 7) // 8) * 8
        cond = jnp.zeros((rows, d), jnp.float32).at[:b].set(c).at[b].set(c_ctx)
        mods = _adaln(cond, w_mod[l], b_mod[l])
        sh1, sc1, g1, sh2, sc2, g2 = jnp.split(mods, 6, axis=-1)
        mod1 = jnp.stack([jnp.stack([sh1[:b], sc1[:b]], axis=1),
                          jnp.broadcast_to(jnp.stack([sh1[b], sc1[b]], axis=0)[None], (b, 2, d))], axis=1)
        mod2 = jnp.stack([sh2[:b], sc2[:b]], axis=1)

        w = w_in[l]
        kv = w[:, :4 * KV_W]
        wk_t = jnp.concatenate([kv[:, 0:KV_W], kv[:, 2 * KV_W:3 * KV_W]], axis=1).T.astype(bf)
        wv_t = jnp.concatenate([kv[:, KV_W:2 * KV_W], kv[:, 3 * KV_W:4 * KV_W]], axis=1).T.astype(bf)
        wq_t = w[:, 4 * KV_W:4 * KV_W + Q_A + Q_B].T.astype(bf)
        wg_t = w[:, 4 * KV_W + Q_A + Q_B:].T.astype(bf)
        qg = jnp.broadcast_to((q_norm_g[l] * (ATTN_SCALE * LOG2E))[:, None], (HEAD_DIM, t))
        kg = jnp.broadcast_to(k_norm_g[l][:, None], (HEAD_DIM, t))

        qa_t, qb_t, ka, kb, va_t, vb_t, gate_t = _in_proj(
            x, ctx, mod1, norm1_g[l][None], wq_t, wk_t, wv_t, wg_t, qg, kg, cos, sin)

        ya_t = _attention_global(128, True, qa_t, ka, va_t)
        yb_t = _attention_window(sink[l], qb_t, kb, vb_t)

        x1, h2, top_idx, top_w = _merge(
            x, ya_t, yb_t, gate_t, w_br_a[l].T.astype(bf), w_br_b[l].T.astype(bf), w_o[l].astype(bf),
            g1[:b, None, :], mod2, norm2_g[l][None], w_router[l].T, b_router[l][:, None])

        offs, tok_sorted, w_sorted = _route_lists(top_idx, top_w, moe_tile)
        w1g, w1l = _w1_split(w_e1[l])
        moe_tiles = _moe(offs, tok_sorted, w_sorted, h2, w1g, w1l,
                         b_e1[l][:, None, 0::2], b_e1[l][:, None, 1::2],
                         w_e2[l].astype(bf), b_e2[l][:, None, :], moe_tile)
        out = _final(x1, moe_tiles, g2[:b, None, :], final_g[None], moe_tile)
    return out
```

```python
import functools

import numpy as np
import jax
import jax.numpy as jnp
from jax import lax
from jax.experimental import pallas as pl
from jax.experimental.pallas import tpu as pltpu

HEAD_DIM = 64
KV_HEADS = 2
A_HEADS = 8
B_HEADS = 8
GROUP = A_HEADS // KV_HEADS
GRID_W = 64
WINDOW = 128
ROPE_HALF = HEAD_DIM // 2
ROPE_QUARTER = ROPE_HALF // 2
ROPE_THETA = 10000.0
N_EXPERTS = 32
TOP_K = 4
SWIGLU_ALPHA = 1.702
SWIGLU_LIMIT = 7.0
RMS_EPS = 1e-6
ATTN_SCALE = HEAD_DIM ** -0.5
LOG2E = 1.4426950408889634
NEG_INF = -1e30
KV_W = KV_HEADS * HEAD_DIM
Q_A = A_HEADS * HEAD_DIM
Q_B = B_HEADS * HEAD_DIM
V_ROWS = HEAD_DIM + 16

SUBLANES = 8
LANES = 128
TOKEN_TILE = 256
IN_PROJ_SUBS = 2
MERGE_TILE = 512
MOE_TILE = 4096
MOE_CHUNK = 256
MOE_VMEM_LIMIT = 60 * 1024 * 1024
ROW_UNROLL = 8
SPLIT_BLOCK = 256
VMEM_LIMIT = 56 * 1024 * 1024

_NT = (((1,), (1,)), ((), ()))


def _dot_nt(a, b, precision=None):
    return lax.dot_general(a, b, _NT, preferred_element_type=jnp.float32, precision=precision)


def _dot(a, b, precision=None):
    return jnp.dot(a, b, preferred_element_type=jnp.float32, precision=precision)


def _params(sem):
    return pltpu.CompilerParams(dimension_semantics=sem, vmem_limit_bytes=VMEM_LIMIT)


def _adaln_kernel(cond_ref, w_ref, b_ref, o_ref):
    cond = cond_ref[...]
    act = cond * jax.nn.sigmoid(cond)
    o_ref[...] = _dot(act, w_ref[...], precision=lax.Precision.HIGHEST) + b_ref[...]


def _adaln(cond, w, b):
    rows, d = cond.shape
    n = w.shape[1]
    tn = 1024
    return pl.pallas_call(
        _adaln_kernel,
        grid=(n // tn,),
        in_specs=[pl.BlockSpec((rows, d), lambda j: (0, 0)),
                  pl.BlockSpec((d, tn), lambda j: (0, j)),
                  pl.BlockSpec((1, tn), lambda j: (0, j))],
        out_specs=pl.BlockSpec((rows, tn), lambda j: (0, j)),
        out_shape=jax.ShapeDtypeStruct((rows, n), jnp.float32),
        compiler_params=_params(("arbitrary",)),
        name="adaln",
    )(cond, w, b.reshape(1, n))


def _rope_t(xh, cos, sin):
    q = ROPE_QUARTER
    a0, a1, b0, b1 = xh[0:q], xh[q:2 * q], xh[2 * q:3 * q], xh[3 * q:4 * q]
    cr, cc = cos[0:q], cos[q:2 * q]
    sr, sc = sin[0:q], sin[q:2 * q]
    return jnp.concatenate([a0 * cr - a1 * sr, a1 * cr + a0 * sr,
                            b0 * cc - b1 * sc, b1 * cc + b0 * sc], axis=0)


def _head_norm_t(xh, g):
    ms = jnp.mean(xh * xh, axis=0, keepdims=True)
    return xh * lax.rsqrt(ms + RMS_EPS) * g


def _mod_norm(x, g, shift, scale):
    ms = jnp.mean(x * x, axis=-1, keepdims=True)
    return x * lax.rsqrt(ms + RMS_EPS) * g * (1.0 + scale) + shift


def _in_proj_kernel(n_lat, x_ref, ctx_ref, mod_ref, g_ref, wq_ref, wk_ref, wv_ref, wg_ref,
                    qg_ref, kg_ref, cos_ref, sin_ref,
                    qa_ref, qb_ref, ka_ref, kb_ref, va_ref, vb_ref, gate_ref, h_ref):
    i = pl.program_id(1)
    t = TOKEN_TILE
    shift = mod_ref[0, 0, 0:1, :]
    scale = mod_ref[0, 0, 1:2, :]
    is_lat = i < n_lat

    for sub in range(IN_PROJ_SUBS):
        @pl.when(is_lat)
        def _(sub=sub):
            h_ref[sub] = _mod_norm(x_ref[0, sub * t:(sub + 1) * t, :], g_ref[...], shift, scale).astype(jnp.bfloat16)

        @pl.when(jnp.logical_not(is_lat))
        def _(sub=sub):
            if sub == 0:
                h_ref[sub] = _mod_norm(ctx_ref[0], g_ref[...], shift, scale).astype(jnp.bfloat16)
            else:
                h_ref[sub] = jnp.zeros(h_ref.shape[1:], jnp.bfloat16)

    kg = kg_ref[...]
    ones = jnp.ones((V_ROWS - HEAD_DIM, t), jnp.float32)
    for sub in range(IN_PROJ_SUBS):
        h = h_ref[sub]
        cos = cos_ref[:, sub * t:(sub + 1) * t]
        sin = sin_ref[:, sub * t:(sub + 1) * t]
        kt = _dot_nt(wk_ref[...], h)
        ka = [_rope_t(_head_norm_t(kt[j * HEAD_DIM:(j + 1) * HEAD_DIM], kg), cos, sin) for j in range(KV_HEADS)]
        kb = [_rope_t(kt[KV_W + j * HEAD_DIM:KV_W + (j + 1) * HEAD_DIM], cos, sin) for j in range(KV_HEADS)]
        ka_ref[0, sub] = jnp.concatenate(ka, axis=0).T.astype(jnp.bfloat16)
        kb_ref[0, sub] = jnp.concatenate(kb, axis=0).T.astype(jnp.bfloat16)
        vt = _dot_nt(wv_ref[...], h)
        for br, v_ref in enumerate((va_ref, vb_ref)):
            rows = []
            for j in range(KV_HEADS):
                rows += [vt[br * KV_W + j * HEAD_DIM:br * KV_W + (j + 1) * HEAD_DIM], ones]
            v_ref[0, sub] = jnp.concatenate(rows, axis=0).astype(jnp.bfloat16)

    @pl.when(is_lat)
    def _():
        qg = qg_ref[...]
        for sub in range(IN_PROJ_SUBS):
            h = h_ref[sub]
            tok = slice(sub * t, (sub + 1) * t)
            cos = cos_ref[:, tok]
            sin = sin_ref[:, tok]
            for half in range(2):
                qt = _dot_nt(wq_ref[half * Q_A:(half + 1) * Q_A, :], h)
                for hd in range(A_HEADS):
                    xh = qt[hd * HEAD_DIM:(hd + 1) * HEAD_DIM]
                    if half == 0:
                        out = _rope_t(_head_norm_t(xh, qg), cos, sin)
                        qa_ref[0, hd * HEAD_DIM:(hd + 1) * HEAD_DIM, tok] = out.astype(jnp.bfloat16)
                    else:
                        out = _rope_t(xh * (ATTN_SCALE * LOG2E), cos, sin)
                        qb_ref[0, hd * HEAD_DIM:(hd + 1) * HEAD_DIM, tok] = out.astype(jnp.bfloat16)
            rows = 512
            for c in range(wg_ref.shape[0] // rows):
                gt = _dot_nt(wg_ref[c * rows:(c + 1) * rows, :], h)
                gate_ref[0, c * rows:(c + 1) * rows, tok] = jax.nn.sigmoid(gt).astype(jnp.bfloat16)


def _in_proj(x, ctx, mod, norm_g, wq_t, wk_t, wv_t, wg_t, qg, kg, cos, sin):
    b, s, d = x.shape
    t = TOKEN_TILE
    subs = IN_PROJ_SUBS
    st = subs * t
    assert ctx.shape[1] == t and s % st == 0
    n_lat = s // st
    n_tiles = (n_lat + 1) * subs
    last = n_lat - 1
    full = lambda shape: pl.BlockSpec(shape, lambda bi, i: (0,) * len(shape))
    q_spec = pl.BlockSpec((1, Q_A, st), lambda bi, i: (bi, 0, jnp.minimum(i, last)))
    k_spec = pl.BlockSpec((1, subs, t, KV_W), lambda bi, i: (bi, i, 0, 0))
    v_spec = pl.BlockSpec((1, subs, KV_HEADS * V_ROWS, t), lambda bi, i: (bi, i, 0, 0))
    bf = jnp.bfloat16
    return pl.pallas_call(
        functools.partial(_in_proj_kernel, n_lat),
        grid=(b, n_lat + 1),
        in_specs=[pl.BlockSpec((1, st, d), lambda bi, i: (bi, jnp.minimum(i, last), 0)),
                  pl.BlockSpec((1, t, d), lambda bi, i: (bi, 0, 0)),
                  pl.BlockSpec((1, 1, 2, d), lambda bi, i: (bi, i // n_lat, 0, 0)),
                  full((1, d)), full(wq_t.shape), full(wk_t.shape), full(wv_t.shape), full(wg_t.shape),
                  full((HEAD_DIM, t)), full((HEAD_DIM, t)),
                  pl.BlockSpec((ROPE_HALF, st), lambda bi, i: (0, i)),
                  pl.BlockSpec((ROPE_HALF, st), lambda bi, i: (0, i))],
        out_specs=[q_spec, q_spec, k_spec, k_spec, v_spec, v_spec,
                   pl.BlockSpec((1, 2 * d, st), lambda bi, i: (bi, 0, jnp.minimum(i, last)))],
        out_shape=[jax.ShapeDtypeStruct((b, Q_A, s), bf), jax.ShapeDtypeStruct((b, Q_B, s), bf),
                   jax.ShapeDtypeStruct((b, n_tiles, t, KV_W), bf), jax.ShapeDtypeStruct((b, n_tiles, t, KV_W), bf),
                   jax.ShapeDtypeStruct((b, n_tiles, KV_HEADS * V_ROWS, t), bf),
                   jax.ShapeDtypeStruct((b, n_tiles, KV_HEADS * V_ROWS, t), bf),
                   jax.ShapeDtypeStruct((b, 2 * d, s), bf)],
        scratch_shapes=[pltpu.VMEM((subs, t, d), bf)],
        compiler_params=_params(("arbitrary", "arbitrary")),
        name="in_proj",
    )(x, ctx, mod, norm_g, wq_t, wk_t, wv_t, wg_t, qg, kg, cos, sin)


def _attn_window_kernel(subs, n_lat, n_tiles, sink_ref, q_ref, k_ref, v_ref, y_ref, s_ref):
    t = tq = TOKEN_TILE
    n = GROUP * tq
    assert WINDOW * 2 == t
    heads = [[kvh * GROUP + g for g in range(GROUP)] for kvh in range(KV_HEADS)]
    n_slots = s_ref.shape[1] // subs
    sink = [jnp.concatenate([jnp.full((1, tq), sink_ref[hd] * LOG2E, jnp.float32) for hd in heads[kvh]], axis=1)
            for kvh in range(KV_HEADS)]
    c_half = lax.rem(lax.broadcasted_iota(jnp.int32, (WINDOW, n), 1), tq)
    r_half = lax.broadcasted_iota(jnp.int32, (WINDOW, n), 0)
    c_full = lax.rem(lax.broadcasted_iota(jnp.int32, (t, n), 1), tq)
    r_full = lax.broadcasted_iota(jnp.int32, (t, n), 0)
    acc_row = lax.broadcasted_iota(jnp.int32, (V_ROWS, n), 0)

    for sub in range(subs):
        qi = pl.program_id(1) * subs + sub
        cols = slice(sub * tq, (sub + 1) * tq)
        qpad = []
        for kvh in range(KV_HEADS):
            qcat = jnp.concatenate([q_ref[0, hd * HEAD_DIM:(hd + 1) * HEAD_DIM, cols] for hd in heads[kvh]], axis=1)
            zeros = jnp.zeros_like(qcat)
            qpad.append(jnp.concatenate([qcat, zeros] if kvh == 0 else [zeros, qcat], axis=0))

        off_prev = jnp.where(qi >= 1, 0, 2 * t)
        off_next = jnp.where(qi + 1 < n_lat, 0, 2 * t)
        pieces = [
            (jnp.maximum(qi - 1, 0), WINDOW, WINDOW, r_half >= c_half + off_prev),
            (qi, 0, t, jnp.abs(r_full - c_full) <= WINDOW),
            (jnp.minimum(qi + 1, n_lat - 1), 0, WINDOW, c_half - r_half >= WINDOW + off_next),
        ] + [(kb, 0, t, None) for kb in range(n_lat, n_tiles)]

        m8 = [jnp.broadcast_to(sink[kvh], (8, n)) for kvh in range(KV_HEADS)]
        for slot, (kb, lo, rows, ok) in enumerate(pieces):
            kblk = k_ref[0, kb, lo:lo + rows, :]
            for kvh in range(KV_HEADS):
                s = _dot(kblk, qpad[kvh])
                if ok is not None:
                    s = jnp.where(ok, s, NEG_INF)
                s_ref[kvh, sub * n_slots + slot, 0:rows, :] = s
                m8[kvh] = jnp.maximum(m8[kvh], jnp.max(s.reshape(rows // 8, 8, n), axis=0))
        m = [jnp.max(m8[kvh], axis=0, keepdims=True) for kvh in range(KV_HEADS)]

        accs = [jnp.where(acc_row == HEAD_DIM, jnp.exp2(sink[kvh] - m[kvh]), 0.0) for kvh in range(KV_HEADS)]
        for slot, (kb, lo, rows, _) in enumerate(pieces):
            for kvh in range(KV_HEADS):
                p = jnp.exp2(s_ref[kvh, sub * n_slots + slot, 0:rows, :] - m[kvh]).astype(jnp.bfloat16)
                vblk = v_ref[0, kb, kvh * V_ROWS:(kvh + 1) * V_ROWS, lo:lo + rows]
                accs[kvh] = accs[kvh] + _dot(vblk, p)
        for kvh in range(KV_HEADS):
            out = accs[kvh][0:HEAD_DIM] / accs[kvh][HEAD_DIM:HEAD_DIM + 1]
            for g, hd in enumerate(heads[kvh]):
                y_ref[0, hd * HEAD_DIM:(hd + 1) * HEAD_DIM, cols] = out[:, g * tq:(g + 1) * tq].astype(jnp.bfloat16)


def _attn_global_kernel(tq, n_tiles, n_q, unroll, q_ref, k_ref, v_ref, y_ref, s0_ref, s1_ref, m0_ref, m1_ref):
    j = pl.program_id(1)
    t = TOKEN_TILE
    n = GROUP * tq
    heads = [[kvh * GROUP + g for g in range(GROUP)] for kvh in range(KV_HEADS)]
    bufs = ((s0_ref, m0_ref), (s1_ref, m1_ref))

    def padded_q():
        qpad = []
        for kvh in range(KV_HEADS):
            qcat = jnp.concatenate([q_ref[0, hd * HEAD_DIM:(hd + 1) * HEAD_DIM, :] for hd in heads[kvh]], axis=1)
            zeros = jnp.zeros_like(qcat)
            qpad.append(jnp.concatenate([qcat, zeros] if kvh == 0 else [zeros, qcat], axis=0))
        return qpad

    def scores(s_ref, kb, m8, qpad):
        kblk = k_ref[0, kb]
        out = []
        for kvh in range(KV_HEADS):
            s = _dot(kblk, qpad[kvh])
            s_ref[kvh, kb] = s
            out.append(jnp.maximum(m8[kvh], jnp.max(s.reshape(t // 8, 8, n), axis=0)))
        return tuple(out)

    def weigh(s_ref, kb, m, accs):
        out = []
        for kvh in range(KV_HEADS):
            p = jnp.exp2(s_ref[kvh, kb] - m[kvh]).astype(jnp.bfloat16)
            vblk = v_ref[0, kb, kvh * V_ROWS:(kvh + 1) * V_ROWS, :]
            out.append(accs[kvh] + _dot(vblk, p))
        return tuple(out)

    def keep_maxima(m_ref, m8):
        for kvh in range(KV_HEADS):
            m_ref[kvh] = m8[kvh]

    def maxima(m_ref):
        return [jnp.max(m_ref[kvh], axis=0, keepdims=True) for kvh in range(KV_HEADS)]

    def write(accs):
        for kvh in range(KV_HEADS):
            out = accs[kvh][0:HEAD_DIM] / accs[kvh][HEAD_DIM:HEAD_DIM + 1]
            for g, hd in enumerate(heads[kvh]):
                y_ref[0, hd * HEAD_DIM:(hd + 1) * HEAD_DIM, :] = out[:, g * tq:(g + 1) * tq].astype(jnp.bfloat16)

    m8_0 = tuple(jnp.full((8, n), -jnp.inf, jnp.float32) for _ in range(KV_HEADS))
    acc_0 = tuple(jnp.zeros((V_ROWS, n), jnp.float32) for _ in range(KV_HEADS))

    @pl.when(j == 0)
    def _():
        s_w, m_w = bufs[0]
        qpad = padded_q()
        keep_maxima(m_w, lax.fori_loop(0, n_tiles, lambda kb, c: scores(s_w, kb, c, qpad), m8_0, unroll=unroll))

    for parity in range(2):
        (s_w, m_w), (s_r, m_r) = bufs[parity], bufs[1 - parity]

        @pl.when((j > 0) & (j < n_q) & (lax.rem(j, 2) == parity))
        def _():
            qpad = padded_q()
            m = maxima(m_r)
            m8, accs = lax.fori_loop(
                0, n_tiles, lambda kb, c: (scores(s_w, kb, c[0], qpad), weigh(s_r, kb, m, c[1])),
                (m8_0, acc_0), unroll=unroll)
            keep_maxima(m_w, m8)
            write(accs)

    @pl.when(j == n_q)
    def _():
        s_r, m_r = bufs[(n_q - 1) % 2]
        m = maxima(m_r)
        write(lax.fori_loop(0, n_tiles, lambda kb, c: weigh(s_r, kb, m, c), acc_0, unroll=unroll))


def _attention_global(tq, unroll, n_tiles, q_t, k, v_t):
    b, hq, s = q_t.shape
    n_q = s // tq
    n = GROUP * tq
    return pl.pallas_call(
        functools.partial(_attn_global_kernel, tq, n_tiles, n_q, unroll),
        grid=(b, n_q + 1),
        in_specs=[pl.BlockSpec((1, hq, tq), lambda bi, j: (bi, 0, jnp.minimum(j, n_q - 1))),
                  pl.BlockSpec((1,) + k.shape[1:], lambda bi, j: (bi, 0, 0, 0)),
                  pl.BlockSpec((1,) + v_t.shape[1:], lambda bi, j: (bi, 0, 0, 0))],
        out_specs=pl.BlockSpec((1, hq, tq), lambda bi, j: (bi, 0, jnp.maximum(j - 1, 0))),
        out_shape=jax.ShapeDtypeStruct((b, hq, s), jnp.bfloat16),
        scratch_shapes=[pltpu.VMEM((KV_HEADS, n_tiles, TOKEN_TILE, n), jnp.float32)] * 2
        + [pltpu.VMEM((KV_HEADS, 8, n), jnp.float32)] * 2,
        compiler_params=_params(("arbitrary", "arbitrary")),
        name="attn_global",
    )(q_t, k, v_t)


def _attention_window(n_tiles, sink, q_t, k, v_t):
    b, hq, s = q_t.shape
    t = TOKEN_TILE
    n_lat = s // t
    subs = 2 if n_lat % 2 == 0 else 1
    slots = 3 + (n_tiles - n_lat)
    grid_spec = pltpu.PrefetchScalarGridSpec(
        num_scalar_prefetch=1,
        grid=(b, n_lat // subs),
        in_specs=[pl.BlockSpec((1, hq, subs * t), lambda bi, qi, sk: (bi, 0, qi)),
                  pl.BlockSpec((1,) + k.shape[1:], lambda bi, qi, sk: (bi, 0, 0, 0)),
                  pl.BlockSpec((1,) + v_t.shape[1:], lambda bi, qi, sk: (bi, 0, 0, 0))],
        out_specs=pl.BlockSpec((1, hq, subs * t), lambda bi, qi, sk: (bi, 0, qi)),
        scratch_shapes=[pltpu.VMEM((KV_HEADS, subs * slots, t, GROUP * t), jnp.float32)],
    )
    return pl.pallas_call(
        functools.partial(_attn_window_kernel, subs, n_lat, n_tiles),
        grid_spec=grid_spec,
        out_shape=jax.ShapeDtypeStruct((b, hq, s), jnp.bfloat16),
        compiler_params=_params(("arbitrary", "arbitrary")),
        name="attn_window",
    )(sink, q_t, k, v_t)


def _to_token_tiles(ref, row0, rows, val):
    for j in range(SUBLANES):
        ref[pl.ds(row0 * SUBLANES + j, rows, stride=SUBLANES), :] = val[:, j * LANES:(j + 1) * LANES]


def _from_token_tiles(ref, row0, rows):
    return jnp.concatenate([ref[pl.ds(row0 * SUBLANES + j, rows, stride=SUBLANES), :] for j in range(SUBLANES)],
                           axis=1)


def _merge_kernel(x_ref, ya_ref, yb_ref, gate_ref, wa_ref, wb_ref, wo_ref, g1_ref, mod_ref, g2_ref,
                  wr_ref, br_ref, x1_ref, h2_ref, idx_ref, wt_ref):
    d = x_ref.shape[-1]
    t = TOKEN_TILE
    for sub in range(x_ref.shape[1] // t):
        tok = slice(sub * t, (sub + 1) * t)
        za = _dot(wa_ref[...], ya_ref[0, :, tok])
        zb = _dot(wb_ref[...], yb_ref[0, :, tok])
        mt = (gate_ref[0, 0:d, tok].astype(jnp.float32) * za
              + gate_ref[0, d:2 * d, tok].astype(jnp.float32) * zb)
        o = _dot(mt.T.astype(jnp.bfloat16), wo_ref[...])
        x1 = x_ref[0, tok, :] + g1_ref[0] * o
        x1_ref[0, tok, :] = x1
        h2 = _mod_norm(x1, g2_ref[...], mod_ref[0, 0:1, :], mod_ref[0, 1:2, :])
        _to_token_tiles(h2_ref, sub * t, t, h2)
        logits = _dot_nt(wr_ref[...], h2, precision=lax.Precision.HIGHEST) + br_ref[...]
        n_e = logits.shape[0]
        row = lax.broadcasted_iota(jnp.int32, logits.shape, 0)
        work = logits
        top_v, top_i = [], []
        for _ in range(TOP_K):
            mk = jnp.max(work, axis=0, keepdims=True)
            ik = jnp.min(jnp.where(work == mk, row, n_e), axis=0, keepdims=True)
            top_v.append(mk)
            top_i.append(ik)
            work = jnp.where(row == ik, -jnp.inf, work)
        ex = [jnp.exp(v - top_v[0]) for v in top_v]
        den = ex[0] + ex[1] + ex[2] + ex[3]
        idx_ref[0, :, tok] = jnp.concatenate(top_i, axis=0)
        wt_ref[0, :, tok] = jnp.concatenate([e / den for e in ex], axis=0)


def _merge(x, ya_t, yb_t, gate_t, wa_t, wb_t, wo, g1, mod2, norm2_g, wr_t, br):
    b, s, d = x.shape
    t = MERGE_TILE
    per = s // t
    assert d == SUBLANES * LANES
    full = lambda shape: pl.BlockSpec(shape, lambda bi, i: (0,) * len(shape))
    tok = pl.BlockSpec((1, t, d), lambda bi, i: (bi, i, 0))
    col = lambda rows: pl.BlockSpec((1, rows, t), lambda bi, i: (bi, 0, i))
    return pl.pallas_call(
        _merge_kernel,
        grid=(b, per),
        in_specs=[tok, col(Q_A), col(Q_B), col(2 * d),
                  full(wa_t.shape), full(wb_t.shape), full(wo.shape),
                  pl.BlockSpec((1, 1, d), lambda bi, i: (bi, 0, 0)),
                  pl.BlockSpec((1, 2, d), lambda bi, i: (bi, 0, 0)),
                  full((1, d)), full(wr_t.shape), full(br.shape)],
        out_specs=[tok, pl.BlockSpec((t * SUBLANES, LANES), lambda bi, i: (bi * per + i, 0)),
                   col(TOP_K), col(TOP_K)],
        out_shape=[jax.ShapeDtypeStruct((b, s, d), jnp.float32),
                   jax.ShapeDtypeStruct((b * s * SUBLANES, LANES), jnp.float32),
                   jax.ShapeDtypeStruct((b, TOP_K, s), jnp.int32), jax.ShapeDtypeStruct((b, TOP_K, s), jnp.float32)],
        compiler_params=_params(("arbitrary", "arbitrary")),
        name="merge_router",
    )(x, ya_t, yb_t, gate_t, wa_t, wb_t, wo, g1, mod2, norm2_g, wr_t, br)


def _w1_split_kernel(w_ref, p_ref, g_ref, l_ref):
    half = SPLIT_BLOCK // 2
    for jb in range(w_ref.shape[-1] // SPLIT_BLOCK):
        blk = w_ref[0, :, jb * SPLIT_BLOCK:(jb + 1) * SPLIT_BLOCK].astype(jnp.bfloat16)
        r = _dot(blk, p_ref[...])
        g_ref[0, :, jb * half:(jb + 1) * half] = r[:, :half].astype(jnp.bfloat16)
        l_ref[0, :, jb * half:(jb + 1) * half] = r[:, half:].astype(jnp.bfloat16)


def _w1_split(w1):
    n_e, d, d2 = w1.shape
    rows = 512
    half = SPLIT_BLOCK // 2
    perm = np.zeros((SPLIT_BLOCK, SPLIT_BLOCK), np.float32)
    for j in range(half):
        perm[2 * j, j] = 1.0
        perm[2 * j + 1, half + j] = 1.0
    out = pl.BlockSpec((1, rows, d2 // 2), lambda e, r: (e, r, 0))
    return pl.pallas_call(
        _w1_split_kernel,
        grid=(n_e, d // rows),
        in_specs=[pl.BlockSpec((1, rows, d2), lambda e, r: (e, r, 0)),
                  pl.BlockSpec((SPLIT_BLOCK, SPLIT_BLOCK), lambda e, r: (0, 0))],
        out_specs=[out, out],
        out_shape=[jax.ShapeDtypeStruct((n_e, d, d2 // 2), jnp.bfloat16)] * 2,
        compiler_params=_params(("arbitrary", "arbitrary")),
        name="w1_split",
    )(w1, jnp.asarray(perm, jnp.bfloat16))


def _moe_kernel(tile, offs_ref, tok_ref, wt_ref, h_ref, w1g_ref, w1l_ref, b1g_ref, b1l_ref, w2_ref, b2_ref,
                acc_ref, xg_ref, y_ref, pend_ref):
    ti = pl.program_id(0)
    e = pl.program_id(1)
    n_e = pl.num_programs(1)
    ch = xg_ref.shape[0] // SUBLANES
    off = ti * (n_e + 1)

    def tile_rows(r):
        return pl.ds(pl.multiple_of(r * SUBLANES, SUBLANES), SUBLANES)

    def static_rows(r):
        return slice(r * SUBLANES, (r + 1) * SUBLANES)

    def expert_mlp(xg):
        glu = jnp.minimum(_dot(xg, w1g_ref[0]) + b1g_ref[0], SWIGLU_LIMIT)
        lin = jnp.clip(_dot(xg, w1l_ref[0]) + b1l_ref[0], -SWIGLU_LIMIT, SWIGLU_LIMIT)
        act = glu * jax.nn.sigmoid(SWIGLU_ALPHA * glu) * (lin + 1.0)
        return _dot(act.astype(jnp.bfloat16), w2_ref[0]) + b2_ref[0]

    def gather_rows(dst_ref, base, n):
        def group(i, _):
            r0 = pl.multiple_of(i * ROW_UNROLL, ROW_UNROLL)
            for u in range(ROW_UNROLL):
                dst_ref[tile_rows(r0 + u), :] = h_ref[tile_rows(tok_ref[0, 0, base + r0 + u]), :]
            return 0

        lax.fori_loop(0, (n + ROW_UNROLL - 1) // ROW_UNROLL, group, 0)

    def scatter_rows(src_ref, base, n):
        def group(i, _):
            r0 = pl.multiple_of(i * ROW_UNROLL, ROW_UNROLL)
            toks = [tok_ref[0, 0, base + r0 + u] for u in range(ROW_UNROLL)]
            new = [acc_ref[tile_rows(toks[u]), :] + wt_ref[0, 0, base + r0 + u] * src_ref[tile_rows(r0 + u), :]
                   for u in range(ROW_UNROLL)]
            for u in range(ROW_UNROLL):
                acc_ref[tile_rows(toks[u]), :] = new[u]
            return 0

        full_groups = n // ROW_UNROLL
        lax.fori_loop(0, full_groups, group, 0)

        def row(r, _):
            tok = tok_ref[0, 0, base + r]
            acc_ref[tile_rows(tok), :] = acc_ref[tile_rows(tok), :] + wt_ref[0, 0, base + r] * src_ref[tile_rows(r), :]
            return 0

        lax.fori_loop(full_groups * ROW_UNROLL, n, row, 0)

    start = offs_ref[off + e]
    count = offs_ref[off + e + 1] - start
    n_blocks = jnp.maximum((count + ch - 1) // ch, 1)
    start_next = offs_ref[off + jnp.minimum(e + 1, n_e - 1)]

    @pl.when((ti == 0) & (e == 0))
    def _():
        for ref in (xg_ref, y_ref):
            ref[...] = jnp.zeros_like(ref)

    @pl.when(e == 0)
    def _():
        acc_ref[...] = jnp.zeros_like(acc_ref)
        pend_ref[0] = 0
        pend_ref[1] = 0
        gather_rows(xg_ref, start, jnp.minimum(count, ch))

    def block_body(rows, k, base_prev, n_prev):
        xg = _from_token_tiles(xg_ref, 0, rows).astype(jnp.bfloat16)

        base_next = jnp.where(k + 1 < n_blocks, start + (k + 1) * ch, start_next)
        for r in range(ch):
            xg_ref[static_rows(r), :] = h_ref[tile_rows(tok_ref[0, 0, base_next + r]), :]

        for g in range(ch // ROW_UNROLL):
            toks, new = [], []
            for u in range(ROW_UNROLL):
                r = g * ROW_UNROLL + u
                valid = r < n_prev
                tok = jnp.where(valid, tok_ref[0, 0, base_prev + r], tile + u)
                w = jnp.where(valid, wt_ref[0, 0, base_prev + r], 0.0)
                toks.append(tok)
                new.append(acc_ref[tile_rows(tok), :] + w * y_ref[static_rows(r), :])
            for u in range(ROW_UNROLL):
                acc_ref[tile_rows(toks[u]), :] = new[u]

        _to_token_tiles(y_ref, 0, rows, expert_mlp(xg))

    def block(k, pending):
        base_prev, n_prev = pending
        n = jnp.minimum(ch, count - k * ch)
        small = n <= ch // 2
        pl.when(small)(functools.partial(block_body, ch // 2, k, base_prev, n_prev))
        pl.when(jnp.logical_not(small))(functools.partial(block_body, ch, k, base_prev, n_prev))
        return start + k * ch, n

    base_prev, n_prev = lax.fori_loop(0, n_blocks, block, (pend_ref[0], pend_ref[1]))
    pend_ref[0] = base_prev
    pend_ref[1] = n_prev

    @pl.when(e == n_e - 1)
    def _():
        scatter_rows(y_ref, base_prev, n_prev)


def _moe(offs, tok_sorted, wt_sorted, h2_tiles, w1g, w1l, b1g, b1l, w2, b2, tile):
    n_tok = h2_tiles.shape[0] // SUBLANES
    n_e, d, d_e = w1g.shape
    n_t = n_tok // tile
    a = tok_sorted.shape[-1]
    assert d == SUBLANES * LANES and MOE_CHUNK % ROW_UNROLL == 0
    smem = lambda: pl.BlockSpec((1, 1, a), lambda ti, e, o: (ti, 0, 0), memory_space=pltpu.SMEM)
    ex = lambda shape: pl.BlockSpec((1,) + shape, lambda ti, e, o: (e, 0, 0))
    out_rows = (tile + ROW_UNROLL) * SUBLANES
    rows = pltpu.VMEM((MOE_CHUNK * SUBLANES, LANES), jnp.float32)
    once = pl.Buffered(1)
    grid_spec = pltpu.PrefetchScalarGridSpec(
        num_scalar_prefetch=1,
        grid=(n_t, n_e),
        in_specs=[smem(), smem(),
                  pl.BlockSpec((tile * SUBLANES, LANES), lambda ti, e, o: (ti, 0), pipeline_mode=once),
                  ex((d, d_e)), ex((d, d_e)), ex((1, d_e)), ex((1, d_e)), ex((d_e, d)), ex((1, d))],
        out_specs=pl.BlockSpec((out_rows, LANES), lambda ti, e, o: (ti, 0), pipeline_mode=once),
        scratch_shapes=[rows, rows, pltpu.SMEM((2,), jnp.int32)],
    )
    return pl.pallas_call(
        functools.partial(_moe_kernel, tile),
        grid_spec=grid_spec,
        out_shape=jax.ShapeDtypeStruct((n_t * out_rows, LANES), jnp.float32),
        compiler_params=pltpu.CompilerParams(dimension_semantics=("arbitrary", "arbitrary"),
                                             vmem_limit_bytes=MOE_VMEM_LIMIT),
        name="moe",
    )(offs, tok_sorted, wt_sorted, h2_tiles, w1g, w1l, b1g, b1l, w2, b2)


def _final_kernel(sub, x_ref, m_ref, g2_ref, g_ref, o_ref):
    t = x_ref.shape[1]
    row0 = lax.rem(pl.program_id(1), sub) * t
    x = x_ref[0] + g2_ref[0] * _from_token_tiles(m_ref, row0, t)
    ms = jnp.mean(x * x, axis=-1, keepdims=True)
    o_ref[0] = x * lax.rsqrt(ms + RMS_EPS) * g_ref[...]


def _final(x1, moe_tiles, g2, final_g, moe_tile):
    b, s, d = x1.shape
    t = min(MERGE_TILE, moe_tile)
    per = s // t
    sub = moe_tile // t
    moe_rows = (moe_tile + ROW_UNROLL) * SUBLANES
    tok = pl.BlockSpec((1, t, d), lambda bi, i: (bi, i, 0))
    return pl.pallas_call(
        functools.partial(_final_kernel, sub),
        grid=(b, per),
        in_specs=[tok, pl.BlockSpec((moe_rows, LANES), lambda bi, i: (bi * (per // sub) + i // sub, 0)),
                  pl.BlockSpec((1, 1, d), lambda bi, i: (bi, 0, 0)),
                  pl.BlockSpec((1, d), lambda bi, i: (0, 0))],
        out_specs=tok,
        out_shape=jax.ShapeDtypeStruct((b, s, d), jnp.float32),
        compiler_params=_params(("arbitrary", "arbitrary")),
        name="final_norm",
    )(x1, moe_tiles, g2, final_g)


def _rope_tables(s, n_ctx):
    inv = ROPE_THETA ** (-(jnp.arange(ROPE_QUARTER, dtype=jnp.float32) * 2.0 / ROPE_HALF))
    pos = jnp.arange(s)
    ang_r = (pos // GRID_W).astype(jnp.float32)[None, :] * inv[:, None]
    ang_c = (pos % GRID_W).astype(jnp.float32)[None, :] * inv[:, None]
    ang = jnp.concatenate([ang_r, ang_c], axis=0)
    cos = jnp.concatenate([jnp.cos(ang), jnp.ones((ROPE_HALF, n_ctx), jnp.float32)], axis=1)
    sin = jnp.concatenate([jnp.sin(ang), jnp.zeros((ROPE_HALF, n_ctx), jnp.float32)], axis=1)
    return cos, sin


def _route_lists(top_idx, top_w, tile):
    b, k, s = top_idx.shape
    n_t = b * s // tile
    per = s // tile
    e_flat = top_idx.reshape(b, k, per, tile).transpose(0, 2, 1, 3).reshape(n_t, k * tile)
    w_flat = top_w.reshape(b, k, per, tile).transpose(0, 2, 1, 3).reshape(n_t, k * tile)
    tok = jnp.tile(jnp.arange(tile, dtype=jnp.int32), k)[None, :]
    key_sorted, w_sorted = lax.sort((e_flat * tile + tok, w_flat), dimension=1, num_keys=1)
    tok_sorted = lax.rem(key_sorted, tile)
    counts = jnp.sum(e_flat[:, :, None] == jnp.arange(N_EXPERTS, dtype=jnp.int32)[None, None, :], axis=1,
                     dtype=jnp.int32)
    offs = jnp.concatenate([jnp.zeros((n_t, 1), jnp.int32), jnp.cumsum(counts, axis=1, dtype=jnp.int32)], axis=1)
    pad = ((0, 0), (0, MOE_CHUNK))
    return offs.reshape(-1), jnp.pad(tok_sorted, pad)[:, None, :], jnp.pad(w_sorted, pad)[:, None, :]


def kernel(x, c, ctx, c_ctx, w_mod, b_mod, norm1_g, norm2_g, w_in, q_norm_g, k_norm_g, sink, w_br_a, w_br_b,
           w_o, w_router, b_router, w_e1, b_e1, w_e2, b_e2, final_g):
    b, s, d = x.shape
    n_ctx = ctx.shape[1]
    assert w_mod.shape[0] == 1
    bf = jnp.bfloat16
    t = TOKEN_TILE
    cos, sin = _rope_tables(s, IN_PROJ_SUBS * t)
    moe_tile = min(MOE_TILE, s)
    for l in range(1):
        rows = ((b + 1 + 7) // 8) * 8
        cond = jnp.zeros((rows, d), jnp.float32).at[:b].set(c).at[b].set(c_ctx)
        mods = _adaln(cond, w_mod[l], b_mod[l])
        sh1, sc1, g1, sh2, sc2, g2 = jnp.split(mods, 6, axis=-1)
        mod1 = jnp.stack([jnp.stack([sh1[:b], sc1[:b]], axis=1),
                          jnp.broadcast_to(jnp.stack([sh1[b], sc1[b]], axis=0)[None], (b, 2, d))], axis=1)
        mod2 = jnp.stack([sh2[:b], sc2[:b]], axis=1)

        w = w_in[l]
        kv = w[:, :4 * KV_W]
        wk_t = jnp.concatenate([kv[:, 0:KV_W], kv[:, 2 * KV_W:3 * KV_W]], axis=1).T.astype(bf)
        wv_t = jnp.concatenate([kv[:, KV_W:2 * KV_W], kv[:, 3 * KV_W:4 * KV_W]], axis=1).T.astype(bf)
        wq_t = w[:, 4 * KV_W:4 * KV_W + Q_A + Q_B].T.astype(bf)
        wg_t = w[:, 4 * KV_W + Q_A + Q_B:].T.astype(bf)
        qg = jnp.broadcast_to((q_norm_g[l] * (ATTN_SCALE * LOG2E))[:, None], (HEAD_DIM, t))
        kg = jnp.broadcast_to(k_norm_g[l][:, None], (HEAD_DIM, t))

        qa_t, qb_t, ka, kb, va_t, vb_t, gate_t = _in_proj(
            x, ctx, mod1, norm1_g[l][None], wq_t, wk_t, wv_t, wg_t, qg, kg, cos, sin)

        n_key_tiles = (s + n_ctx) // t
        ya_t = _attention_global(128, True, n_key_tiles, qa_t, ka, va_t)
        yb_t = _attention_window(n_key_tiles, sink[l], qb_t, kb, vb_t)

        x1, h2, top_idx, top_w = _merge(
            x, ya_t, yb_t, gate_t, w_br_a[l].T.astype(bf), w_br_b[l].T.astype(bf), w_o[l].astype(bf),
            g1[:b, None, :], mod2, norm2_g[l][None], w_router[l].T, b_router[l][:, None])

        offs, tok_sorted, w_sorted = _route_lists(top_idx, top_w, moe_tile)
        w1g, w1l = _w1_split(w_e1[l])
        moe_tiles = _moe(offs, tok_sorted, w_sorted, h2, w1g, w1l,
                         b_e1[l][:, None, 0::2], b_e1[l][:, None, 1::2],
                         w_e2[l].astype(bf), b_e2[l][:, None, :], moe_tile)
        out = _final(x1, moe_tiles, g2[:b, None, :], final_g[None], moe_tile)
    return out
```

```python
import functools

import numpy as np
import jax
import jax.numpy as jnp
from jax import lax
from jax.experimental import pallas as pl
from jax.experimental.pallas import tpu as pltpu

HEAD_DIM = 64
KV_HEADS = 2
A_HEADS = 8
B_HEADS = 8
GROUP = A_HEADS // KV_HEADS
GRID_W = 64
WINDOW = 128
ROPE_HALF = HEAD_DIM // 2
ROPE_QUARTER = ROPE_HALF // 2
ROPE_THETA = 10000.0
N_EXPERTS = 32
TOP_K = 4
SWIGLU_ALPHA = 1.702
SWIGLU_LIMIT = 7.0
RMS_EPS = 1e-6
ATTN_SCALE = HEAD_DIM ** -0.5
LOG2E = 1.4426950408889634
NEG_INF = -1e30
KV_W = KV_HEADS * HEAD_DIM
Q_A = A_HEADS * HEAD_DIM
Q_B = B_HEADS * HEAD_DIM
V_ROWS = HEAD_DIM + 16

SUBLANES = 8
LANES = 128
TOKEN_TILE = 256
IN_PROJ_SUBS = 2
MERGE_TILE = 1024
FINAL_TILE = 512
MOE_TILE = 4096
MOE_CHUNK = 256
MOE_VMEM_LIMIT = 60 * 1024 * 1024
ROW_UNROLL = 8
SPLIT_BLOCK = 256
VMEM_LIMIT = 56 * 1024 * 1024

_NT = (((1,), (1,)), ((), ()))


def _dot_nt(a, b, precision=None):
    return lax.dot_general(a, b, _NT, preferred_element_type=jnp.float32, precision=precision)


def _dot(a, b, precision=None):
    return jnp.dot(a, b, preferred_element_type=jnp.float32, precision=precision)


def _params(sem):
    return pltpu.CompilerParams(dimension_semantics=sem, vmem_limit_bytes=VMEM_LIMIT)


def _adaln_kernel(cond_ref, w_ref, b_ref, o_ref):
    cond = cond_ref[...]
    act = cond * jax.nn.sigmoid(cond)
    o_ref[...] = _dot(act, w_ref[...], precision=lax.Precision.HIGHEST) + b_ref[...]


def _adaln(cond, w, b):
    rows, d = cond.shape
    n = w.shape[1]
    tn = 1024
    return pl.pallas_call(
        _adaln_kernel,
        grid=(n // tn,),
        in_specs=[pl.BlockSpec((rows, d), lambda j: (0, 0)),
                  pl.BlockSpec((d, tn), lambda j: (0, j)),
                  pl.BlockSpec((1, tn), lambda j: (0, j))],
        out_specs=pl.BlockSpec((rows, tn), lambda j: (0, j)),
        out_shape=jax.ShapeDtypeStruct((rows, n), jnp.float32),
        compiler_params=_params(("arbitrary",)),
        name="adaln",
    )(cond, w, b.reshape(1, n))


def _rope_t(xh, cos, sin):
    q = ROPE_QUARTER
    a0, a1, b0, b1 = xh[0:q], xh[q:2 * q], xh[2 * q:3 * q], xh[3 * q:4 * q]
    cr, cc = cos[0:q], cos[q:2 * q]
    sr, sc = sin[0:q], sin[q:2 * q]
    return jnp.concatenate([a0 * cr - a1 * sr, a1 * cr + a0 * sr,
                            b0 * cc - b1 * sc, b1 * cc + b0 * sc], axis=0)


def _head_norm_t(xh, g):
    ms = jnp.mean(xh * xh, axis=0, keepdims=True)
    return xh * lax.rsqrt(ms + RMS_EPS) * g


def _mod_norm(x, g, shift, scale):
    ms = jnp.mean(x * x, axis=-1, keepdims=True)
    return x * lax.rsqrt(ms + RMS_EPS) * g * (1.0 + scale) + shift


def _in_proj_kernel(n_lat, x_ref, ctx_ref, mod_ref, g_ref, wq_ref, wk_ref, wv_ref, wg_ref,
                    qg_ref, kg_ref, cos_ref, sin_ref,
                    qa_ref, qb_ref, ka_ref, kb_ref, va_ref, vb_ref, gate_ref, h_ref):
    i = pl.program_id(1)
    t = TOKEN_TILE
    shift = mod_ref[0, 0, 0:1, :]
    scale = mod_ref[0, 0, 1:2, :]
    is_lat = i < n_lat

    for sub in range(IN_PROJ_SUBS):
        @pl.when(is_lat)
        def _(sub=sub):
            h_ref[sub] = _mod_norm(x_ref[0, sub * t:(sub + 1) * t, :], g_ref[...], shift, scale).astype(jnp.bfloat16)

        @pl.when(jnp.logical_not(is_lat))
        def _(sub=sub):
            if sub == 0:
                h_ref[sub] = _mod_norm(ctx_ref[0], g_ref[...], shift, scale).astype(jnp.bfloat16)
            else:
                h_ref[sub] = jnp.zeros(h_ref.shape[1:], jnp.bfloat16)

    kg = kg_ref[...]
    ones = jnp.ones((V_ROWS - HEAD_DIM, t), jnp.float32)
    for sub in range(IN_PROJ_SUBS):
        h = h_ref[sub]
        cos = cos_ref[:, sub * t:(sub + 1) * t]
        sin = sin_ref[:, sub * t:(sub + 1) * t]
        kt = _dot_nt(wk_ref[...], h)
        ka = [_rope_t(_head_norm_t(kt[j * HEAD_DIM:(j + 1) * HEAD_DIM], kg), cos, sin) for j in range(KV_HEADS)]
        kb = [_rope_t(kt[KV_W + j * HEAD_DIM:KV_W + (j + 1) * HEAD_DIM], cos, sin) for j in range(KV_HEADS)]
        ka_ref[0, sub] = jnp.concatenate(ka, axis=0).T.astype(jnp.bfloat16)
        kb_ref[0, sub] = jnp.concatenate(kb, axis=0).T.astype(jnp.bfloat16)
        vt = _dot_nt(wv_ref[...], h)
        for br, v_ref in enumerate((va_ref, vb_ref)):
            rows = []
            for j in range(KV_HEADS):
                rows += [vt[br * KV_W + j * HEAD_DIM:br * KV_W + (j + 1) * HEAD_DIM], ones]
            v_ref[0, sub] = jnp.concatenate(rows, axis=0).astype(jnp.bfloat16)

    @pl.when(is_lat)
    def _():
        qg = qg_ref[...]
        for sub in range(IN_PROJ_SUBS):
            h = h_ref[sub]
            tok = slice(sub * t, (sub + 1) * t)
            cos = cos_ref[:, tok]
            sin = sin_ref[:, tok]
            for half in range(2):
                qt = _dot_nt(wq_ref[half * Q_A:(half + 1) * Q_A, :], h)
                for hd in range(A_HEADS):
                    xh = qt[hd * HEAD_DIM:(hd + 1) * HEAD_DIM]
                    if half == 0:
                        out = _rope_t(_head_norm_t(xh, qg), cos, sin)
                        qa_ref[0, hd * HEAD_DIM:(hd + 1) * HEAD_DIM, tok] = out.astype(jnp.bfloat16)
                    else:
                        out = _rope_t(xh * (ATTN_SCALE * LOG2E), cos, sin)
                        qb_ref[0, hd * HEAD_DIM:(hd + 1) * HEAD_DIM, tok] = out.astype(jnp.bfloat16)
            rows = 512
            for c in range(wg_ref.shape[0] // rows):
                gt = _dot_nt(wg_ref[c * rows:(c + 1) * rows, :], h)
                gate_ref[0, c * rows:(c + 1) * rows, tok] = jax.nn.sigmoid(gt).astype(jnp.bfloat16)


def _in_proj(x, ctx, mod, norm_g, wq_t, wk_t, wv_t, wg_t, qg, kg, cos, sin):
    b, s, d = x.shape
    t = TOKEN_TILE
    subs = IN_PROJ_SUBS
    st = subs * t
    assert ctx.shape[1] == t and s % st == 0
    n_lat = s // st
    n_tiles = (n_lat + 1) * subs
    last = n_lat - 1
    full = lambda shape: pl.BlockSpec(shape, lambda bi, i: (0,) * len(shape))
    q_spec = pl.BlockSpec((1, Q_A, st), lambda bi, i: (bi, 0, jnp.minimum(i, last)))
    k_spec = pl.BlockSpec((1, subs, t, KV_W), lambda bi, i: (bi, i, 0, 0))
    v_spec = pl.BlockSpec((1, subs, KV_HEADS * V_ROWS, t), lambda bi, i: (bi, i, 0, 0))
    bf = jnp.bfloat16
    return pl.pallas_call(
        functools.partial(_in_proj_kernel, n_lat),
        grid=(b, n_lat + 1),
        in_specs=[pl.BlockSpec((1, st, d), lambda bi, i: (bi, jnp.minimum(i, last), 0)),
                  pl.BlockSpec((1, t, d), lambda bi, i: (bi, 0, 0)),
                  pl.BlockSpec((1, 1, 2, d), lambda bi, i: (bi, i // n_lat, 0, 0)),
                  full((1, d)), full(wq_t.shape), full(wk_t.shape), full(wv_t.shape), full(wg_t.shape),
                  full((HEAD_DIM, t)), full((HEAD_DIM, t)),
                  pl.BlockSpec((ROPE_HALF, st), lambda bi, i: (0, i)),
                  pl.BlockSpec((ROPE_HALF, st), lambda bi, i: (0, i))],
        out_specs=[q_spec, q_spec, k_spec, k_spec, v_spec, v_spec,
                   pl.BlockSpec((1, 2 * d, st), lambda bi, i: (bi, 0, jnp.minimum(i, last)))],
        out_shape=[jax.ShapeDtypeStruct((b, Q_A, s), bf), jax.ShapeDtypeStruct((b, Q_B, s), bf),
                   jax.ShapeDtypeStruct((b, n_tiles, t, KV_W), bf), jax.ShapeDtypeStruct((b, n_tiles, t, KV_W), bf),
                   jax.ShapeDtypeStruct((b, n_tiles, KV_HEADS * V_ROWS, t), bf),
                   jax.ShapeDtypeStruct((b, n_tiles, KV_HEADS * V_ROWS, t), bf),
                   jax.ShapeDtypeStruct((b, 2 * d, s), bf)],
        scratch_shapes=[pltpu.VMEM((subs, t, d), bf)],
        compiler_params=_params(("arbitrary", "arbitrary")),
        name="in_proj",
    )(x, ctx, mod, norm_g, wq_t, wk_t, wv_t, wg_t, qg, kg, cos, sin)


def _attn_window_kernel(subs, n_lat, n_tiles, sink_ref, q_ref, k_ref, v_ref, y_ref, s_ref):
    t = tq = TOKEN_TILE
    n = GROUP * tq
    assert WINDOW * 2 == t
    heads = [[kvh * GROUP + g for g in range(GROUP)] for kvh in range(KV_HEADS)]
    n_slots = s_ref.shape[1] // subs
    sink = [jnp.concatenate([jnp.full((1, tq), sink_ref[hd] * LOG2E, jnp.float32) for hd in heads[kvh]], axis=1)
            for kvh in range(KV_HEADS)]
    c_half = lax.rem(lax.broadcasted_iota(jnp.int32, (WINDOW, n), 1), tq)
    r_half = lax.broadcasted_iota(jnp.int32, (WINDOW, n), 0)
    c_full = lax.rem(lax.broadcasted_iota(jnp.int32, (t, n), 1), tq)
    r_full = lax.broadcasted_iota(jnp.int32, (t, n), 0)
    acc_row = lax.broadcasted_iota(jnp.int32, (V_ROWS, n), 0)

    for sub in range(subs):
        qi = pl.program_id(1) * subs + sub
        cols = slice(sub * tq, (sub + 1) * tq)
        qpad = []
        for kvh in range(KV_HEADS):
            qcat = jnp.concatenate([q_ref[0, hd * HEAD_DIM:(hd + 1) * HEAD_DIM, cols] for hd in heads[kvh]], axis=1)
            zeros = jnp.zeros_like(qcat)
            qpad.append(jnp.concatenate([qcat, zeros] if kvh == 0 else [zeros, qcat], axis=0))

        off_prev = jnp.where(qi >= 1, 0, 2 * t)
        off_next = jnp.where(qi + 1 < n_lat, 0, 2 * t)
        pieces = [
            (jnp.maximum(qi - 1, 0), WINDOW, WINDOW, r_half >= c_half + off_prev),
            (qi, 0, t, jnp.abs(r_full - c_full) <= WINDOW),
            (jnp.minimum(qi + 1, n_lat - 1), 0, WINDOW, c_half - r_half >= WINDOW + off_next),
        ] + [(kb, 0, t, None) for kb in range(n_lat, n_tiles)]

        m8 = [jnp.broadcast_to(sink[kvh], (8, n)) for kvh in range(KV_HEADS)]
        for slot, (kb, lo, rows, ok) in enumerate(pieces):
            kblk = k_ref[0, kb, lo:lo + rows, :]
            for kvh in range(KV_HEADS):
                s = _dot(kblk, qpad[kvh])
                if ok is not None:
                    s = jnp.where(ok, s, NEG_INF)
                s_ref[kvh, sub * n_slots + slot, 0:rows, :] = s
                m8[kvh] = jnp.maximum(m8[kvh], jnp.max(s.reshape(rows // 8, 8, n), axis=0))
        m = [jnp.max(m8[kvh], axis=0, keepdims=True) for kvh in range(KV_HEADS)]

        accs = [jnp.where(acc_row == HEAD_DIM, jnp.exp2(sink[kvh] - m[kvh]), 0.0) for kvh in range(KV_HEADS)]
        for slot, (kb, lo, rows, _) in enumerate(pieces):
            for kvh in range(KV_HEADS):
                p = jnp.exp2(s_ref[kvh, sub * n_slots + slot, 0:rows, :] - m[kvh]).astype(jnp.bfloat16)
                vblk = v_ref[0, kb, kvh * V_ROWS:(kvh + 1) * V_ROWS, lo:lo + rows]
                accs[kvh] = accs[kvh] + _dot(vblk, p)
        for kvh in range(KV_HEADS):
            out = accs[kvh][0:HEAD_DIM] / accs[kvh][HEAD_DIM:HEAD_DIM + 1]
            for g, hd in enumerate(heads[kvh]):
                y_ref[0, hd * HEAD_DIM:(hd + 1) * HEAD_DIM, cols] = out[:, g * tq:(g + 1) * tq].astype(jnp.bfloat16)


def _attn_global_kernel(tq, n_tiles, n_q, unroll, q_ref, k_ref, v_ref, y_ref, s0_ref, s1_ref, m0_ref, m1_ref):
    j = pl.program_id(1)
    t = TOKEN_TILE
    n = GROUP * tq
    heads = [[kvh * GROUP + g for g in range(GROUP)] for kvh in range(KV_HEADS)]
    bufs = ((s0_ref, m0_ref), (s1_ref, m1_ref))

    def padded_q():
        qpad = []
        for kvh in range(KV_HEADS):
            qcat = jnp.concatenate([q_ref[0, hd * HEAD_DIM:(hd + 1) * HEAD_DIM, :] for hd in heads[kvh]], axis=1)
            zeros = jnp.zeros_like(qcat)
            qpad.append(jnp.concatenate([qcat, zeros] if kvh == 0 else [zeros, qcat], axis=0))
        return qpad

    def scores(s_ref, kb, m8, qpad):
        kblk = k_ref[0, kb]
        out = []
        for kvh in range(KV_HEADS):
            s = _dot(kblk, qpad[kvh])
            s_ref[kvh, kb] = s
            out.append(jnp.maximum(m8[kvh], jnp.max(s.reshape(t // 8, 8, n), axis=0)))
        return tuple(out)

    def weigh(s_ref, kb, m, accs):
        out = []
        for kvh in range(KV_HEADS):
            p = jnp.exp2(s_ref[kvh, kb] - m[kvh]).astype(jnp.bfloat16)
            vblk = v_ref[0, kb, kvh * V_ROWS:(kvh + 1) * V_ROWS, :]
            out.append(accs[kvh] + _dot(vblk, p))
        return tuple(out)

    def keep_maxima(m_ref, m8):
        for kvh in range(KV_HEADS):
            m_ref[kvh] = m8[kvh]

    def maxima(m_ref):
        return [jnp.max(m_ref[kvh], axis=0, keepdims=True) for kvh in range(KV_HEADS)]

    def write(accs):
        for kvh in range(KV_HEADS):
            out = accs[kvh][0:HEAD_DIM] / accs[kvh][HEAD_DIM:HEAD_DIM + 1]
            for g, hd in enumerate(heads[kvh]):
                y_ref[0, hd * HEAD_DIM:(hd + 1) * HEAD_DIM, :] = out[:, g * tq:(g + 1) * tq].astype(jnp.bfloat16)

    m8_0 = tuple(jnp.full((8, n), -jnp.inf, jnp.float32) for _ in range(KV_HEADS))
    acc_0 = tuple(jnp.zeros((V_ROWS, n), jnp.float32) for _ in range(KV_HEADS))

    @pl.when(j == 0)
    def _():
        s_w, m_w = bufs[0]
        qpad = padded_q()
        keep_maxima(m_w, lax.fori_loop(0, n_tiles, lambda kb, c: scores(s_w, kb, c, qpad), m8_0, unroll=unroll))

    for parity in range(2):
        (s_w, m_w), (s_r, m_r) = bufs[parity], bufs[1 - parity]

        @pl.when((j > 0) & (j < n_q) & (lax.rem(j, 2) == parity))
        def _():
            qpad = padded_q()
            m = maxima(m_r)
            m8, accs = lax.fori_loop(
                0, n_tiles, lambda kb, c: (scores(s_w, kb, c[0], qpad), weigh(s_r, kb, m, c[1])),
                (m8_0, acc_0), unroll=unroll)
            keep_maxima(m_w, m8)
            write(accs)

    @pl.when(j == n_q)
    def _():
        s_r, m_r = bufs[(n_q - 1) % 2]
        m = maxima(m_r)
        write(lax.fori_loop(0, n_tiles, lambda kb, c: weigh(s_r, kb, m, c), acc_0, unroll=unroll))


def _attention_global(tq, unroll, n_tiles, q_t, k, v_t):
    b, hq, s = q_t.shape
    n_q = s // tq
    n = GROUP * tq
    return pl.pallas_call(
        functools.partial(_attn_global_kernel, tq, n_tiles, n_q, unroll),
        grid=(b, n_q + 1),
        in_specs=[pl.BlockSpec((1, hq, tq), lambda bi, j: (bi, 0, jnp.minimum(j, n_q - 1))),
                  pl.BlockSpec((1,) + k.shape[1:], lambda bi, j: (bi, 0, 0, 0)),
                  pl.BlockSpec((1,) + v_t.shape[1:], lambda bi, j: (bi, 0, 0, 0))],
        out_specs=pl.BlockSpec((1, hq, tq), lambda bi, j: (bi, 0, jnp.maximum(j - 1, 0))),
        out_shape=jax.ShapeDtypeStruct((b, hq, s), jnp.bfloat16),
        scratch_shapes=[pltpu.VMEM((KV_HEADS, n_tiles, TOKEN_TILE, n), jnp.float32)] * 2
        + [pltpu.VMEM((KV_HEADS, 8, n), jnp.float32)] * 2,
        compiler_params=_params(("arbitrary", "arbitrary")),
        name="attn_global",
    )(q_t, k, v_t)


def _attention_window(n_tiles, sink, q_t, k, v_t):
    b, hq, s = q_t.shape
    t = TOKEN_TILE
    n_lat = s // t
    subs = 4 if n_lat % 4 == 0 else (2 if n_lat % 2 == 0 else 1)
    slots = 3 + (n_tiles - n_lat)
    grid_spec = pltpu.PrefetchScalarGridSpec(
        num_scalar_prefetch=1,
        grid=(b, n_lat // subs),
        in_specs=[pl.BlockSpec((1, hq, subs * t), lambda bi, qi, sk: (bi, 0, qi)),
                  pl.BlockSpec((1,) + k.shape[1:], lambda bi, qi, sk: (bi, 0, 0, 0)),
                  pl.BlockSpec((1,) + v_t.shape[1:], lambda bi, qi, sk: (bi, 0, 0, 0))],
        out_specs=pl.BlockSpec((1, hq, subs * t), lambda bi, qi, sk: (bi, 0, qi)),
        scratch_shapes=[pltpu.VMEM((KV_HEADS, subs * slots, t, GROUP * t), jnp.float32)],
    )
    return pl.pallas_call(
        functools.partial(_attn_window_kernel, subs, n_lat, n_tiles),
        grid_spec=grid_spec,
        out_shape=jax.ShapeDtypeStruct((b, hq, s), jnp.bfloat16),
        compiler_params=_params(("arbitrary", "arbitrary")),
        name="attn_window",
    )(sink, q_t, k, v_t)


def _to_token_tiles(ref, row0, rows, val):
    for j in range(SUBLANES):
        ref[pl.ds(row0 * SUBLANES + j, rows, stride=SUBLANES), :] = val[:, j * LANES:(j + 1) * LANES]


def _from_token_tiles(ref, row0, rows):
    return jnp.concatenate([ref[pl.ds(row0 * SUBLANES + j, rows, stride=SUBLANES), :] for j in range(SUBLANES)],
                           axis=1)


def _merge_kernel(x_ref, ya_ref, yb_ref, gate_ref, wa_ref, wb_ref, wo_ref, g1_ref, mod_ref, g2_ref,
                  wr_ref, br_ref, x1_ref, h2_ref, idx_ref, wt_ref):
    d = x_ref.shape[-1]
    t = TOKEN_TILE
    for sub in range(x_ref.shape[1] // t):
        tok = slice(sub * t, (sub + 1) * t)
        za = _dot(wa_ref[...], ya_ref[0, :, tok])
        zb = _dot(wb_ref[...], yb_ref[0, :, tok])
        mt = (gate_ref[0, 0:d, tok].astype(jnp.float32) * za
              + gate_ref[0, d:2 * d, tok].astype(jnp.float32) * zb)
        o = _dot(mt.T.astype(jnp.bfloat16), wo_ref[...])
        x1 = x_ref[0, tok, :] + g1_ref[0] * o
        x1_ref[0, tok, :] = x1
        h2 = _mod_norm(x1, g2_ref[...], mod_ref[0, 0:1, :], mod_ref[0, 1:2, :])
        _to_token_tiles(h2_ref, sub * t, t, h2)
        h_hi = h2.astype(jnp.bfloat16)
        h_lo = (h2 - h_hi.astype(jnp.float32)).astype(jnp.bfloat16)
        logits = (_dot_nt(wr_ref[0], h_hi) + _dot_nt(wr_ref[0], h_lo) + _dot_nt(wr_ref[1], h_hi)
                  + br_ref[...])
        n_e = logits.shape[0]
        row = lax.broadcasted_iota(jnp.int32, logits.shape, 0)
        work = logits
        top_v, top_i = [], []
        for _ in range(TOP_K):
            mk = jnp.max(work, axis=0, keepdims=True)
            ik = jnp.min(jnp.where(work == mk, row, n_e), axis=0, keepdims=True)
            top_v.append(mk)
            top_i.append(ik)
            work = jnp.where(row == ik, -jnp.inf, work)
        ex = [jnp.exp(v - top_v[0]) for v in top_v]
        den = ex[0] + ex[1] + ex[2] + ex[3]
        idx_ref[0, :, tok] = jnp.concatenate(top_i, axis=0)
        wt_ref[0, :, tok] = jnp.concatenate([e / den for e in ex], axis=0)


def _merge(x, ya_t, yb_t, gate_t, wa_t, wb_t, wo, g1, mod2, norm2_g, wr_t, br):
    b, s, d = x.shape
    t = min(MERGE_TILE, s)
    per = s // t
    assert d == SUBLANES * LANES
    full = lambda shape: pl.BlockSpec(shape, lambda bi, i: (0,) * len(shape))
    tok = pl.BlockSpec((1, t, d), lambda bi, i: (bi, i, 0))
    col = lambda rows: pl.BlockSpec((1, rows, t), lambda bi, i: (bi, 0, i))
    return pl.pallas_call(
        _merge_kernel,
        grid=(b, per),
        in_specs=[tok, col(Q_A), col(Q_B), col(2 * d),
                  full(wa_t.shape), full(wb_t.shape), full(wo.shape),
                  pl.BlockSpec((1, 1, d), lambda bi, i: (bi, 0, 0)),
                  pl.BlockSpec((1, 2, d), lambda bi, i: (bi, 0, 0)),
                  full((1, d)), full(wr_t.shape), full(br.shape)],
        out_specs=[tok, pl.BlockSpec((t * SUBLANES, LANES), lambda bi, i: (bi * per + i, 0)),
                   col(TOP_K), col(TOP_K)],
        out_shape=[jax.ShapeDtypeStruct((b, s, d), jnp.float32),
                   jax.ShapeDtypeStruct((b * s * SUBLANES, LANES), jnp.float32),
                   jax.ShapeDtypeStruct((b, TOP_K, s), jnp.int32), jax.ShapeDtypeStruct((b, TOP_K, s), jnp.float32)],
        compiler_params=_params(("arbitrary", "arbitrary")),
        name="merge_router",
    )(x, ya_t, yb_t, gate_t, wa_t, wb_t, wo, g1, mod2, norm2_g, wr_t, br)


def _w1_split_kernel(w_ref, p_ref, g_ref, l_ref):
    half = SPLIT_BLOCK // 2
    for jb in range(w_ref.shape[-1] // SPLIT_BLOCK):
        blk = w_ref[0, :, jb * SPLIT_BLOCK:(jb + 1) * SPLIT_BLOCK].astype(jnp.bfloat16)
        r = _dot(blk, p_ref[...])
        g_ref[0, :, jb * half:(jb + 1) * half] = r[:, :half].astype(jnp.bfloat16)
        l_ref[0, :, jb * half:(jb + 1) * half] = r[:, half:].astype(jnp.bfloat16)


def _w1_split(w1):
    n_e, d, d2 = w1.shape
    rows = 512
    half = SPLIT_BLOCK // 2
    perm = np.zeros((SPLIT_BLOCK, SPLIT_BLOCK), np.float32)
    for j in range(half):
        perm[2 * j, j] = 1.0
        perm[2 * j + 1, half + j] = 1.0
    out = pl.BlockSpec((1, rows, d2 // 2), lambda e, r: (e, r, 0))
    return pl.pallas_call(
        _w1_split_kernel,
        grid=(n_e, d // rows),
        in_specs=[pl.BlockSpec((1, rows, d2), lambda e, r: (e, r, 0)),
                  pl.BlockSpec((SPLIT_BLOCK, SPLIT_BLOCK), lambda e, r: (0, 0))],
        out_specs=[out, out],
        out_shape=[jax.ShapeDtypeStruct((n_e, d, d2 // 2), jnp.bfloat16)] * 2,
        compiler_params=_params(("arbitrary", "arbitrary")),
        name="w1_split",
    )(w1, jnp.asarray(perm, jnp.bfloat16))


def _moe_kernel(tile, offs_ref, tok_ref, wt_ref, h_ref, w1g_ref, w1l_ref, b1g_ref, b1l_ref, w2_ref, b2_ref,
                acc_ref, xg_ref, y_ref, pend_ref):
    ti = pl.program_id(0)
    e = pl.program_id(1)
    n_e = pl.num_programs(1)
    ch = xg_ref.shape[0] // SUBLANES
    off = ti * (n_e + 1)

    def tile_rows(r):
        return pl.ds(pl.multiple_of(r * SUBLANES, SUBLANES), SUBLANES)

    def static_rows(r):
        return slice(r * SUBLANES, (r + 1) * SUBLANES)

    def expert_mlp(xg):
        glu = jnp.minimum(_dot(xg, w1g_ref[0]) + b1g_ref[0], SWIGLU_LIMIT)
        lin = jnp.clip(_dot(xg, w1l_ref[0]) + b1l_ref[0], -SWIGLU_LIMIT, SWIGLU_LIMIT)
        act = glu * jax.nn.sigmoid(SWIGLU_ALPHA * glu) * (lin + 1.0)
        return _dot(act.astype(jnp.bfloat16), w2_ref[0]) + b2_ref[0]

    def gather_rows(dst_ref, base, n):
        def group(i, _):
            r0 = pl.multiple_of(i * ROW_UNROLL, ROW_UNROLL)
            for u in range(ROW_UNROLL):
                dst_ref[tile_rows(r0 + u), :] = h_ref[tile_rows(tok_ref[0, 0, base + r0 + u]), :]
            return 0

        lax.fori_loop(0, (n + ROW_UNROLL - 1) // ROW_UNROLL, group, 0)

    def scatter_rows(src_ref, base, n):
        def group(i, _):
            r0 = pl.multiple_of(i * ROW_UNROLL, ROW_UNROLL)
            toks = [tok_ref[0, 0, base + r0 + u] for u in range(ROW_UNROLL)]
            new = [acc_ref[tile_rows(toks[u]), :] + wt_ref[0, 0, base + r0 + u] * src_ref[tile_rows(r0 + u), :]
                   for u in range(ROW_UNROLL)]
            for u in range(ROW_UNROLL):
                acc_ref[tile_rows(toks[u]), :] = new[u]
            return 0

        full_groups = n // ROW_UNROLL
        lax.fori_loop(0, full_groups, group, 0)

        def row(r, _):
            tok = tok_ref[0, 0, base + r]
            acc_ref[tile_rows(tok), :] = acc_ref[tile_rows(tok), :] + wt_ref[0, 0, base + r] * src_ref[tile_rows(r), :]
            return 0

        lax.fori_loop(full_groups * ROW_UNROLL, n, row, 0)

    start = offs_ref[off + e]
    count = offs_ref[off + e + 1] - start
    n_blocks = jnp.maximum((count + ch - 1) // ch, 1)
    start_next = offs_ref[off + jnp.minimum(e + 1, n_e - 1)]

    @pl.when((ti == 0) & (e == 0))
    def _():
        for ref in (xg_ref, y_ref):
            ref[...] = jnp.zeros_like(ref)

    @pl.when(e == 0)
    def _():
        acc_ref[...] = jnp.zeros_like(acc_ref)
        pend_ref[0] = 0
        pend_ref[1] = 0
        gather_rows(xg_ref, start, jnp.minimum(count, ch))

    def block_body(rows, k, base_prev, n_prev):
        xg = _from_token_tiles(xg_ref, 0, rows).astype(jnp.bfloat16)

        base_next = jnp.where(k + 1 < n_blocks, start + (k + 1) * ch, start_next)
        for r in range(ch):
            xg_ref[static_rows(r), :] = h_ref[tile_rows(tok_ref[0, 0, base_next + r]), :]

        for g in range(ch // ROW_UNROLL):
            toks, new = [], []
            for u in range(ROW_UNROLL):
                r = g * ROW_UNROLL + u
                valid = r < n_prev
                tok = jnp.where(valid, tok_ref[0, 0, base_prev + r], tile + u)
                w = jnp.where(valid, wt_ref[0, 0, base_prev + r], 0.0)
                toks.append(tok)
                new.append(acc_ref[tile_rows(tok), :] + w * y_ref[static_rows(r), :])
            for u in range(ROW_UNROLL):
                acc_ref[tile_rows(toks[u]), :] = new[u]

        _to_token_tiles(y_ref, 0, rows, expert_mlp(xg))

    def block(k, pending):
        base_prev, n_prev = pending
        n = jnp.minimum(ch, count - k * ch)
        small = n <= ch // 2
        pl.when(small)(functools.partial(block_body, ch // 2, k, base_prev, n_prev))
        pl.when(jnp.logical_not(small))(functools.partial(block_body, ch, k, base_prev, n_prev))
        return start + k * ch, n

    base_prev, n_prev = lax.fori_loop(0, n_blocks, block, (pend_ref[0], pend_ref[1]))
    pend_ref[0] = base_prev
    pend_ref[1] = n_prev

    @pl.when(e == n_e - 1)
    def _():
        scatter_rows(y_ref, base_prev, n_prev)


def _moe(offs, tok_sorted, wt_sorted, h2_tiles, w1g, w1l, b1g, b1l, w2, b2, tile):
    n_tok = h2_tiles.shape[0] // SUBLANES
    n_e, d, d_e = w1g.shape
    n_t = n_tok // tile
    a = tok_sorted.shape[-1]
    assert d == SUBLANES * LANES and MOE_CHUNK % ROW_UNROLL == 0
    smem = lambda: pl.BlockSpec((1, 1, a), lambda ti, e, o: (ti, 0, 0), memory_space=pltpu.SMEM)
    ex = lambda shape: pl.BlockSpec((1,) + shape, lambda ti, e, o: (e, 0, 0))
    out_rows = (tile + ROW_UNROLL) * SUBLANES
    rows = pltpu.VMEM((MOE_CHUNK * SUBLANES, LANES), jnp.float32)
    once = pl.Buffered(1)
    grid_spec = pltpu.PrefetchScalarGridSpec(
        num_scalar_prefetch=1,
        grid=(n_t, n_e),
        in_specs=[smem(), smem(),
                  pl.BlockSpec((tile * SUBLANES, LANES), lambda ti, e, o: (ti, 0), pipeline_mode=once),
                  ex((d, d_e)), ex((d, d_e)), ex((1, d_e)), ex((1, d_e)), ex((d_e, d)), ex((1, d))],
        out_specs=pl.BlockSpec((out_rows, LANES), lambda ti, e, o: (ti, 0), pipeline_mode=once),
        scratch_shapes=[rows, rows, pltpu.SMEM((2,), jnp.int32)],
    )
    return pl.pallas_call(
        functools.partial(_moe_kernel, tile),
        grid_spec=grid_spec,
        out_shape=jax.ShapeDtypeStruct((n_t * out_rows, LANES), jnp.float32),
        compiler_params=pltpu.CompilerParams(dimension_semantics=("arbitrary", "arbitrary"),
                                             vmem_limit_bytes=MOE_VMEM_LIMIT),
        name="moe",
    )(offs, tok_sorted, wt_sorted, h2_tiles, w1g, w1l, b1g, b1l, w2, b2)


def _final_kernel(sub, x_ref, m_ref, g2_ref, g_ref, o_ref):
    t = x_ref.shape[1]
    row0 = lax.rem(pl.program_id(1), sub) * t
    x = x_ref[0] + g2_ref[0] * _from_token_tiles(m_ref, row0, t)
    ms = jnp.mean(x * x, axis=-1, keepdims=True)
    o_ref[0] = x * lax.rsqrt(ms + RMS_EPS) * g_ref[...]


def _final(x1, moe_tiles, g2, final_g, moe_tile):
    b, s, d = x1.shape
    t = min(FINAL_TILE, moe_tile)
    per = s // t
    sub = moe_tile // t
    moe_rows = (moe_tile + ROW_UNROLL) * SUBLANES
    tok = pl.BlockSpec((1, t, d), lambda bi, i: (bi, i, 0))
    return pl.pallas_call(
        functools.partial(_final_kernel, sub),
        grid=(b, per),
        in_specs=[tok, pl.BlockSpec((moe_rows, LANES), lambda bi, i: (bi * (per // sub) + i // sub, 0)),
                  pl.BlockSpec((1, 1, d), lambda bi, i: (bi, 0, 0)),
                  pl.BlockSpec((1, d), lambda bi, i: (0, 0))],
        out_specs=tok,
        out_shape=jax.ShapeDtypeStruct((b, s, d), jnp.float32),
        compiler_params=_params(("arbitrary", "arbitrary")),
        name="final_norm",
    )(x1, moe_tiles, g2, final_g)


def _rope_tables(s, n_ctx):
    inv = ROPE_THETA ** (-(jnp.arange(ROPE_QUARTER, dtype=jnp.float32) * 2.0 / ROPE_HALF))
    pos = jnp.arange(s)
    ang_r = (pos // GRID_W).astype(jnp.float32)[None, :] * inv[:, None]
    ang_c = (pos % GRID_W).astype(jnp.float32)[None, :] * inv[:, None]
    ang = jnp.concatenate([ang_r, ang_c], axis=0)
    cos = jnp.concatenate([jnp.cos(ang), jnp.ones((ROPE_HALF, n_ctx), jnp.float32)], axis=1)
    sin = jnp.concatenate([jnp.sin(ang), jnp.zeros((ROPE_HALF, n_ctx), jnp.float32)], axis=1)
    return cos, sin


def _split_bf16(w):
    hi = w.astype(jnp.bfloat16)
    return jnp.stack([hi, (w - hi.astype(jnp.float32)).astype(jnp.bfloat16)])


def _route_lists(top_idx, top_w, tile):
    b, k, s = top_idx.shape
    n_t = b * s // tile
    per = s // tile
    e_flat = top_idx.reshape(b, k, per, tile).transpose(0, 2, 1, 3).reshape(n_t, k * tile)
    w_flat = top_w.reshape(b, k, per, tile).transpose(0, 2, 1, 3).reshape(n_t, k * tile)
    tok = jnp.tile(jnp.arange(tile, dtype=jnp.int32), k)[None, :]
    key_sorted, w_sorted = lax.sort((e_flat * tile + tok, w_flat), dimension=1, num_keys=1)
    tok_sorted = lax.rem(key_sorted, tile)
    counts = jnp.sum(e_flat[:, :, None] == jnp.arange(N_EXPERTS, dtype=jnp.int32)[None, None, :], axis=1,
                     dtype=jnp.int32)
    offs = jnp.concatenate([jnp.zeros((n_t, 1), jnp.int32), jnp.cumsum(counts, axis=1, dtype=jnp.int32)], axis=1)
    pad = ((0, 0), (0, MOE_CHUNK))
    return offs.reshape(-1), jnp.pad(tok_sorted, pad)[:, None, :], jnp.pad(w_sorted, pad)[:, None, :]


def kernel(x, c, ctx, c_ctx, w_mod, b_mod, norm1_g, norm2_g, w_in, q_norm_g, k_norm_g, sink, w_br_a, w_br_b,
           w_o, w_router, b_router, w_e1, b_e1, w_e2, b_e2, final_g):
    b, s, d = x.shape
    n_ctx = ctx.shape[1]
    assert w_mod.shape[0] == 1
    bf = jnp.bfloat16
    t = TOKEN_TILE
    cos, sin = _rope_tables(s, IN_PROJ_SUBS * t)
    moe_tile = min(MOE_TILE, s)
    for l in range(1):
        rows = ((b + 1 + 7) // 8) * 8
        cond = jnp.zeros((rows, d), jnp.float32).at[:b].set(c).at[b].set(c_ctx)
        mods = _adaln(cond, w_mod[l], b_mod[l])
        sh1, sc1, g1, sh2, sc2, g2 = jnp.split(mods, 6, axis=-1)
        mod1 = jnp.stack([jnp.stack([sh1[:b], sc1[:b]], axis=1),
                          jnp.broadcast_to(jnp.stack([sh1[b], sc1[b]], axis=0)[None], (b, 2, d))], axis=1)
        mod2 = jnp.stack([sh2[:b], sc2[:b]], axis=1)

        w = w_in[l]
        kv = w[:, :4 * KV_W]
        wk_t = jnp.concatenate([kv[:, 0:KV_W], kv[:, 2 * KV_W:3 * KV_W]], axis=1).T.astype(bf)
        wv_t = jnp.concatenate([kv[:, KV_W:2 * KV_W], kv[:, 3 * KV_W:4 * KV_W]], axis=1).T.astype(bf)
        wq_t = w[:, 4 * KV_W:4 * KV_W + Q_A + Q_B].T.astype(bf)
        wg_t = w[:, 4 * KV_W + Q_A + Q_B:].T.astype(bf)
        qg = jnp.broadcast_to((q_norm_g[l] * (ATTN_SCALE * LOG2E))[:, None], (HEAD_DIM, t))
        kg = jnp.broadcast_to(k_norm_g[l][:, None], (HEAD_DIM, t))

        qa_t, qb_t, ka, kb, va_t, vb_t, gate_t = _in_proj(
            x, ctx, mod1, norm1_g[l][None], wq_t, wk_t, wv_t, wg_t, qg, kg, cos, sin)

        n_key_tiles = (s + n_ctx) // t
        ya_t = _attention_global(128, True, n_key_tiles, qa_t, ka, va_t)
        yb_t = _attention_window(n_key_tiles, sink[l], qb_t, kb, vb_t)

        x1, h2, top_idx, top_w = _merge(
            x, ya_t, yb_t, gate_t, w_br_a[l].T.astype(bf), w_br_b[l].T.astype(bf), w_o[l].astype(bf),
            g1[:b, None, :], mod2, norm2_g[l][None], _split_bf16(w_router[l].T), b_router[l][:, None])

        offs, tok_sorted, w_sorted = _route_lists(top_idx, top_w, moe_tile)
        w1g, w1l = _w1_split(w_e1[l])
        moe_tiles = _moe(offs, tok_sorted, w_sorted, h2, w1g, w1l,
                         b_e1[l][:, None, 0::2], b_e1[l][:, None, 1::2],
                         w_e2[l].astype(bf), b_e2[l][:, None, :], moe_tile)
        out = _final(x1, moe_tiles, g2[:b, None, :], final_g[None], moe_tile)
    return out
```

```python
import functools

import numpy as np
import jax
import jax.numpy as jnp
from jax import lax
from jax.experimental import pallas as pl
from jax.experimental.pallas import tpu as pltpu

HEAD_DIM = 64
KV_HEADS = 2
A_HEADS = 8
B_HEADS = 8
GROUP = A_HEADS // KV_HEADS
GRID_W = 64
WINDOW = 128
ROPE_HALF = HEAD_DIM // 2
ROPE_QUARTER = ROPE_HALF // 2
ROPE_THETA = 10000.0
N_EXPERTS = 32
TOP_K = 4
SWIGLU_ALPHA = 1.702
SWIGLU_LIMIT = 7.0
RMS_EPS = 1e-6
ATTN_SCALE = HEAD_DIM ** -0.5
LOG2E = 1.4426950408889634
NEG_INF = -1e30
KV_W = KV_HEADS * HEAD_DIM
Q_A = A_HEADS * HEAD_DIM
Q_B = B_HEADS * HEAD_DIM
V_ROWS = HEAD_DIM + 16

SUBLANES = 8
LANES = 128
TOKEN_TILE = 256
IN_PROJ_SUBS = 2
MERGE_TILE = 1024
FINAL_TILE = 512
MOE_TILE = 4096
MOE_CHUNK = 256
MOE_VMEM_LIMIT = 60 * 1024 * 1024
ROW_UNROLL = 8
SPLIT_BLOCK = 256
VMEM_LIMIT = 56 * 1024 * 1024

_NT = (((1,), (1,)), ((), ()))


def _dot_nt(a, b, precision=None):
    return lax.dot_general(a, b, _NT, preferred_element_type=jnp.float32, precision=precision)


def _dot(a, b, precision=None):
    return jnp.dot(a, b, preferred_element_type=jnp.float32, precision=precision)


def _params(sem):
    return pltpu.CompilerParams(dimension_semantics=sem, vmem_limit_bytes=VMEM_LIMIT)


def _adaln_kernel(cond_ref, w_ref, b_ref, o_ref):
    cond = cond_ref[...]
    act = cond * jax.nn.sigmoid(cond)
    o_ref[...] = _dot(act, w_ref[...], precision=lax.Precision.HIGHEST) + b_ref[...]


def _adaln(cond, w, b):
    rows, d = cond.shape
    n = w.shape[1]
    tn = 1024
    return pl.pallas_call(
        _adaln_kernel,
        grid=(n // tn,),
        in_specs=[pl.BlockSpec((rows, d), lambda j: (0, 0)),
                  pl.BlockSpec((d, tn), lambda j: (0, j)),
                  pl.BlockSpec((1, tn), lambda j: (0, j))],
        out_specs=pl.BlockSpec((rows, tn), lambda j: (0, j)),
        out_shape=jax.ShapeDtypeStruct((rows, n), jnp.float32),
        compiler_params=_params(("arbitrary",)),
        name="adaln",
    )(cond, w, b.reshape(1, n))


def _rope_t(xh, cos, sin):
    q = ROPE_QUARTER
    a0, a1, b0, b1 = xh[0:q], xh[q:2 * q], xh[2 * q:3 * q], xh[3 * q:4 * q]
    cr, cc = cos[0:q], cos[q:2 * q]
    sr, sc = sin[0:q], sin[q:2 * q]
    return jnp.concatenate([a0 * cr - a1 * sr, a1 * cr + a0 * sr,
                            b0 * cc - b1 * sc, b1 * cc + b0 * sc], axis=0)


def _head_norm_t(xh, g):
    ms = jnp.mean(xh * xh, axis=0, keepdims=True)
    return xh * lax.rsqrt(ms + RMS_EPS) * g


def _mod_norm(x, g, shift, scale):
    ms = jnp.mean(x * x, axis=-1, keepdims=True)
    return x * lax.rsqrt(ms + RMS_EPS) * g * (1.0 + scale) + shift


def _in_proj_kernel(n_lat, x_ref, ctx_ref, mod_ref, g_ref, wq_ref, wk_ref, wv_ref, wg_ref,
                    qg_ref, kg_ref, cos_ref, sin_ref,
                    qa_ref, qb_ref, ka_ref, kb_ref, va_ref, vb_ref, gate_ref, h_ref):
    i = pl.program_id(1)
    t = TOKEN_TILE
    shift = mod_ref[0, 0, 0:1, :]
    scale = mod_ref[0, 0, 1:2, :]
    is_lat = i < n_lat

    for sub in range(IN_PROJ_SUBS):
        @pl.when(is_lat)
        def _(sub=sub):
            h_ref[sub] = _mod_norm(x_ref[0, sub * t:(sub + 1) * t, :], g_ref[...], shift, scale).astype(jnp.bfloat16)

        @pl.when(jnp.logical_not(is_lat))
        def _(sub=sub):
            if sub == 0:
                h_ref[sub] = _mod_norm(ctx_ref[0], g_ref[...], shift, scale).astype(jnp.bfloat16)
            else:
                h_ref[sub] = jnp.zeros(h_ref.shape[1:], jnp.bfloat16)

    kg = kg_ref[...]
    ones = jnp.ones((V_ROWS - HEAD_DIM, t), jnp.float32)
    for sub in range(IN_PROJ_SUBS):
        h = h_ref[sub]
        cos = cos_ref[:, sub * t:(sub + 1) * t]
        sin = sin_ref[:, sub * t:(sub + 1) * t]
        kt = _dot_nt(wk_ref[...], h)
        ka = [_rope_t(_head_norm_t(kt[j * HEAD_DIM:(j + 1) * HEAD_DIM], kg), cos, sin) for j in range(KV_HEADS)]
        kb = [_rope_t(kt[KV_W + j * HEAD_DIM:KV_W + (j + 1) * HEAD_DIM], cos, sin) for j in range(KV_HEADS)]
        ka_ref[0, sub] = jnp.concatenate(ka, axis=0).T.astype(jnp.bfloat16)
        kb_ref[0, sub] = jnp.concatenate(kb, axis=0).T.astype(jnp.bfloat16)
        vt = _dot_nt(wv_ref[...], h)
        for br, v_ref in enumerate((va_ref, vb_ref)):
            rows = []
            for j in range(KV_HEADS):
                rows += [vt[br * KV_W + j * HEAD_DIM:br * KV_W + (j + 1) * HEAD_DIM], ones]
            v_ref[0, sub] = jnp.concatenate(rows, axis=0).astype(jnp.bfloat16)

    @pl.when(is_lat)
    def _():
        qg = qg_ref[...]
        for sub in range(IN_PROJ_SUBS):
            h = h_ref[sub]
            tok = slice(sub * t, (sub + 1) * t)
            cos = cos_ref[:, tok]
            sin = sin_ref[:, tok]
            for half in range(2):
                qt = _dot_nt(wq_ref[half * Q_A:(half + 1) * Q_A, :], h)
                for hd in range(A_HEADS):
                    xh = qt[hd * HEAD_DIM:(hd + 1) * HEAD_DIM]
                    if half == 0:
                        out = _rope_t(_head_norm_t(xh, qg), cos, sin)
                        qa_ref[0, hd * HEAD_DIM:(hd + 1) * HEAD_DIM, tok] = out.astype(jnp.bfloat16)
                    else:
                        out = _rope_t(xh * (ATTN_SCALE * LOG2E), cos, sin)
                        qb_ref[0, hd * HEAD_DIM:(hd + 1) * HEAD_DIM, tok] = out.astype(jnp.bfloat16)
            rows = 512
            for c in range(wg_ref.shape[0] // rows):
                gt = _dot_nt(wg_ref[c * rows:(c + 1) * rows, :], h)
                gate_ref[0, c * rows:(c + 1) * rows, tok] = jax.nn.sigmoid(gt).astype(jnp.bfloat16)


def _in_proj(x, ctx, mod, norm_g, wq_t, wk_t, wv_t, wg_t, qg, kg, cos, sin):
    b, s, d = x.shape
    t = TOKEN_TILE
    subs = IN_PROJ_SUBS
    st = subs * t
    assert ctx.shape[1] == t and s % st == 0
    n_lat = s // st
    n_tiles = (n_lat + 1) * subs
    last = n_lat - 1
    full = lambda shape: pl.BlockSpec(shape, lambda bi, i: (0,) * len(shape))
    q_spec = pl.BlockSpec((1, Q_A, st), lambda bi, i: (bi, 0, jnp.minimum(i, last)))
    k_spec = pl.BlockSpec((1, subs, t, KV_W), lambda bi, i: (bi, i, 0, 0))
    v_spec = pl.BlockSpec((1, subs, KV_HEADS * V_ROWS, t), lambda bi, i: (bi, i, 0, 0))
    bf = jnp.bfloat16
    return pl.pallas_call(
        functools.partial(_in_proj_kernel, n_lat),
        grid=(b, n_lat + 1),
        in_specs=[pl.BlockSpec((1, st, d), lambda bi, i: (bi, jnp.minimum(i, last), 0)),
                  pl.BlockSpec((1, t, d), lambda bi, i: (bi, 0, 0)),
                  pl.BlockSpec((1, 1, 2, d), lambda bi, i: (bi, i // n_lat, 0, 0)),
                  full((1, d)), full(wq_t.shape), full(wk_t.shape), full(wv_t.shape), full(wg_t.shape),
                  full((HEAD_DIM, t)), full((HEAD_DIM, t)),
                  pl.BlockSpec((ROPE_HALF, st), lambda bi, i: (0, i)),
                  pl.BlockSpec((ROPE_HALF, st), lambda bi, i: (0, i))],
        out_specs=[q_spec, q_spec, k_spec, k_spec, v_spec, v_spec,
                   pl.BlockSpec((1, 2 * d, st), lambda bi, i: (bi, 0, jnp.minimum(i, last)))],
        out_shape=[jax.ShapeDtypeStruct((b, Q_A, s), bf), jax.ShapeDtypeStruct((b, Q_B, s), bf),
                   jax.ShapeDtypeStruct((b, n_tiles, t, KV_W), bf), jax.ShapeDtypeStruct((b, n_tiles, t, KV_W), bf),
                   jax.ShapeDtypeStruct((b, n_tiles, KV_HEADS * V_ROWS, t), bf),
                   jax.ShapeDtypeStruct((b, n_tiles, KV_HEADS * V_ROWS, t), bf),
                   jax.ShapeDtypeStruct((b, 2 * d, s), bf)],
        scratch_shapes=[pltpu.VMEM((subs, t, d), bf)],
        compiler_params=_params(("arbitrary", "arbitrary")),
        name="in_proj",
    )(x, ctx, mod, norm_g, wq_t, wk_t, wv_t, wg_t, qg, kg, cos, sin)


def _attn_window_kernel(subs, n_lat, n_tiles, sink_ref, q_ref, k_ref, v_ref, y_ref, s_ref):
    t = tq = TOKEN_TILE
    n = GROUP * tq
    assert WINDOW * 2 == t
    heads = [[kvh * GROUP + g for g in range(GROUP)] for kvh in range(KV_HEADS)]
    n_slots = s_ref.shape[1] // subs
    sink = [jnp.concatenate([jnp.full((1, tq), sink_ref[hd] * LOG2E, jnp.float32) for hd in heads[kvh]], axis=1)
            for kvh in range(KV_HEADS)]
    c_half = lax.rem(lax.broadcasted_iota(jnp.int32, (WINDOW, n), 1), tq)
    r_half = lax.broadcasted_iota(jnp.int32, (WINDOW, n), 0)
    c_full = lax.rem(lax.broadcasted_iota(jnp.int32, (t, n), 1), tq)
    r_full = lax.broadcasted_iota(jnp.int32, (t, n), 0)
    acc_row = lax.broadcasted_iota(jnp.int32, (V_ROWS, n), 0)

    for sub in range(subs):
        qi = pl.program_id(1) * subs + sub
        cols = slice(sub * tq, (sub + 1) * tq)
        qpad = []
        for kvh in range(KV_HEADS):
            qcat = jnp.concatenate([q_ref[0, hd * HEAD_DIM:(hd + 1) * HEAD_DIM, cols] for hd in heads[kvh]], axis=1)
            zeros = jnp.zeros_like(qcat)
            qpad.append(jnp.concatenate([qcat, zeros] if kvh == 0 else [zeros, qcat], axis=0))

        off_prev = jnp.where(qi >= 1, 0, 2 * t)
        off_next = jnp.where(qi + 1 < n_lat, 0, 2 * t)
        pieces = [
            (jnp.maximum(qi - 1, 0), WINDOW, WINDOW, r_half >= c_half + off_prev),
            (qi, 0, t, jnp.abs(r_full - c_full) <= WINDOW),
            (jnp.minimum(qi + 1, n_lat - 1), 0, WINDOW, c_half - r_half >= WINDOW + off_next),
        ] + [(kb, 0, t, None) for kb in range(n_lat, n_tiles)]

        m8 = [jnp.broadcast_to(sink[kvh], (8, n)) for kvh in range(KV_HEADS)]
        for slot, (kb, lo, rows, ok) in enumerate(pieces):
            kblk = k_ref[0, kb, lo:lo + rows, :]
            for kvh in range(KV_HEADS):
                s = _dot(kblk, qpad[kvh])
                if ok is not None:
                    s = jnp.where(ok, s, NEG_INF)
                s_ref[kvh, sub * n_slots + slot, 0:rows, :] = s
                m8[kvh] = jnp.maximum(m8[kvh], jnp.max(s.reshape(rows // 8, 8, n), axis=0))
        m = [jnp.max(m8[kvh], axis=0, keepdims=True) for kvh in range(KV_HEADS)]

        accs = [jnp.where(acc_row == HEAD_DIM, jnp.exp2(sink[kvh] - m[kvh]), 0.0) for kvh in range(KV_HEADS)]
        for slot, (kb, lo, rows, _) in enumerate(pieces):
            for kvh in range(KV_HEADS):
                p = jnp.exp2(s_ref[kvh, sub * n_slots + slot, 0:rows, :] - m[kvh]).astype(jnp.bfloat16)
                vblk = v_ref[0, kb, kvh * V_ROWS:(kvh + 1) * V_ROWS, lo:lo + rows]
                accs[kvh] = accs[kvh] + _dot(vblk, p)
        for kvh in range(KV_HEADS):
            out = accs[kvh][0:HEAD_DIM] / accs[kvh][HEAD_DIM:HEAD_DIM + 1]
            for g, hd in enumerate(heads[kvh]):
                y_ref[0, hd * HEAD_DIM:(hd + 1) * HEAD_DIM, cols] = out[:, g * tq:(g + 1) * tq].astype(jnp.bfloat16)


def _attn_global_kernel(tq, n_tiles, n_q, unroll, q_ref, k_ref, v_ref, y_ref, s0_ref, s1_ref, m0_ref, m1_ref):
    j = pl.program_id(0)
    t = TOKEN_TILE
    n = GROUP * tq
    heads = [[kvh * GROUP + g for g in range(GROUP)] for kvh in range(KV_HEADS)]
    bufs = ((s0_ref, m0_ref), (s1_ref, m1_ref))

    def padded_q():
        qpad = []
        for kvh in range(KV_HEADS):
            qcat = jnp.concatenate([q_ref[0, hd * HEAD_DIM:(hd + 1) * HEAD_DIM, :] for hd in heads[kvh]], axis=1)
            zeros = jnp.zeros_like(qcat)
            qpad.append(jnp.concatenate([qcat, zeros] if kvh == 0 else [zeros, qcat], axis=0))
        return qpad

    def scores(s_ref, kb, m8, qpad):
        kblk = k_ref[0, kb]
        out = []
        for kvh in range(KV_HEADS):
            s = _dot(kblk, qpad[kvh])
            s_ref[kvh, kb] = s
            out.append(jnp.maximum(m8[kvh], jnp.max(s.reshape(t // 8, 8, n), axis=0)))
        return tuple(out)

    def weigh(s_ref, kb, m, accs):
        out = []
        for kvh in range(KV_HEADS):
            p = jnp.exp2(s_ref[kvh, kb] - m[kvh]).astype(jnp.bfloat16)
            vblk = v_ref[0, kb, kvh * V_ROWS:(kvh + 1) * V_ROWS, :]
            out.append(accs[kvh] + _dot(vblk, p))
        return tuple(out)

    def keep_maxima(m_ref, m8):
        for kvh in range(KV_HEADS):
            m_ref[kvh] = m8[kvh]

    def maxima(m_ref):
        return [jnp.max(m_ref[kvh], axis=0, keepdims=True) for kvh in range(KV_HEADS)]

    def write(accs):
        for kvh in range(KV_HEADS):
            out = accs[kvh][0:HEAD_DIM] / accs[kvh][HEAD_DIM:HEAD_DIM + 1]
            for g, hd in enumerate(heads[kvh]):
                y_ref[0, hd * HEAD_DIM:(hd + 1) * HEAD_DIM, :] = out[:, g * tq:(g + 1) * tq].astype(jnp.bfloat16)

    m8_0 = tuple(jnp.full((8, n), -jnp.inf, jnp.float32) for _ in range(KV_HEADS))
    acc_0 = tuple(jnp.zeros((V_ROWS, n), jnp.float32) for _ in range(KV_HEADS))

    @pl.when(j == 0)
    def _():
        s_w, m_w = bufs[0]
        qpad = padded_q()
        keep_maxima(m_w, lax.fori_loop(0, n_tiles, lambda kb, c: scores(s_w, kb, c, qpad), m8_0, unroll=unroll))

    for parity in range(2):
        (s_w, m_w), (s_r, m_r) = bufs[parity], bufs[1 - parity]

        @pl.when((j > 0) & (j < n_q) & (lax.rem(j, 2) == parity))
        def _():
            qpad = padded_q()
            m = maxima(m_r)
            m8, accs = lax.fori_loop(
                0, n_tiles, lambda kb, c: (scores(s_w, kb, c[0], qpad), weigh(s_r, kb, m, c[1])),
                (m8_0, acc_0), unroll=unroll)
            keep_maxima(m_w, m8)
            write(accs)

    @pl.when(j == n_q)
    def _():
        s_r, m_r = bufs[(n_q - 1) % 2]
        m = maxima(m_r)
        write(lax.fori_loop(0, n_tiles, lambda kb, c: weigh(s_r, kb, m, c), acc_0, unroll=unroll))


def _attention_global(tq, unroll, n_tiles, q_t, k, v_t):
    b, hq, s = q_t.shape
    per = s // tq
    n_q = b * per
    n = GROUP * tq
    scored = lambda j: jnp.minimum(j, n_q - 1)
    weighed = lambda j: jnp.maximum(j - 1, 0)
    return pl.pallas_call(
        functools.partial(_attn_global_kernel, tq, n_tiles, n_q, unroll),
        grid=(n_q + 1,),
        in_specs=[pl.BlockSpec((1, hq, tq), lambda j: (scored(j) // per, 0, scored(j) % per)),
                  pl.BlockSpec((1,) + k.shape[1:], lambda j: (scored(j) // per, 0, 0, 0)),
                  pl.BlockSpec((1,) + v_t.shape[1:], lambda j: (weighed(j) // per, 0, 0, 0))],
        out_specs=pl.BlockSpec((1, hq, tq), lambda j: (weighed(j) // per, 0, weighed(j) % per)),
        out_shape=jax.ShapeDtypeStruct((b, hq, s), jnp.bfloat16),
        scratch_shapes=[pltpu.VMEM((KV_HEADS, n_tiles, TOKEN_TILE, n), jnp.float32)] * 2
        + [pltpu.VMEM((KV_HEADS, 8, n), jnp.float32)] * 2,
        compiler_params=_params(("arbitrary",)),
        name="attn_global",
    )(q_t, k, v_t)


def _attention_window(n_tiles, sink, q_t, k, v_t):
    b, hq, s = q_t.shape
    t = TOKEN_TILE
    n_lat = s // t
    subs = 4 if n_lat % 4 == 0 else (2 if n_lat % 2 == 0 else 1)
    slots = 3 + (n_tiles - n_lat)
    grid_spec = pltpu.PrefetchScalarGridSpec(
        num_scalar_prefetch=1,
        grid=(b, n_lat // subs),
        in_specs=[pl.BlockSpec((1, hq, subs * t), lambda bi, qi, sk: (bi, 0, qi)),
                  pl.BlockSpec((1,) + k.shape[1:], lambda bi, qi, sk: (bi, 0, 0, 0)),
                  pl.BlockSpec((1,) + v_t.shape[1:], lambda bi, qi, sk: (bi, 0, 0, 0))],
        out_specs=pl.BlockSpec((1, hq, subs * t), lambda bi, qi, sk: (bi, 0, qi)),
        scratch_shapes=[pltpu.VMEM((KV_HEADS, subs * slots, t, GROUP * t), jnp.float32)],
    )
    return pl.pallas_call(
        functools.partial(_attn_window_kernel, subs, n_lat, n_tiles),
        grid_spec=grid_spec,
        out_shape=jax.ShapeDtypeStruct((b, hq, s), jnp.bfloat16),
        compiler_params=_params(("arbitrary", "arbitrary")),
        name="attn_window",
    )(sink, q_t, k, v_t)


def _to_token_tiles(ref, row0, rows, val):
    for j in range(SUBLANES):
        ref[pl.ds(row0 * SUBLANES + j, rows, stride=SUBLANES), :] = val[:, j * LANES:(j + 1) * LANES]


def _from_token_tiles(ref, row0, rows):
    return jnp.concatenate([ref[pl.ds(row0 * SUBLANES + j, rows, stride=SUBLANES), :] for j in range(SUBLANES)],
                           axis=1)


def _merge_kernel(x_ref, ya_ref, yb_ref, gate_ref, wa_ref, wb_ref, wo_ref, g1_ref, mod_ref, g2_ref,
                  wr_ref, br_ref, x1_ref, h2_ref, idx_ref, wt_ref):
    d = x_ref.shape[-1]
    t = TOKEN_TILE
    for sub in range(x_ref.shape[1] // t):
        tok = slice(sub * t, (sub + 1) * t)
        za = _dot(wa_ref[...], ya_ref[0, :, tok])
        zb = _dot(wb_ref[...], yb_ref[0, :, tok])
        mt = (gate_ref[0, 0:d, tok].astype(jnp.float32) * za
              + gate_ref[0, d:2 * d, tok].astype(jnp.float32) * zb)
        o = _dot(mt.T.astype(jnp.bfloat16), wo_ref[...])
        x1 = x_ref[0, tok, :] + g1_ref[0] * o
        x1_ref[0, tok, :] = x1
        h2 = _mod_norm(x1, g2_ref[...], mod_ref[0, 0:1, :], mod_ref[0, 1:2, :])
        _to_token_tiles(h2_ref, sub * t, t, h2)
        h_hi = h2.astype(jnp.bfloat16)
        h_lo = (h2 - h_hi.astype(jnp.float32)).astype(jnp.bfloat16)
        logits = (_dot_nt(wr_ref[0], h_hi) + _dot_nt(wr_ref[0], h_lo) + _dot_nt(wr_ref[1], h_hi)
                  + br_ref[...])
        n_e = logits.shape[0]
        row = lax.broadcasted_iota(jnp.int32, logits.shape, 0)
        work = logits
        top_v, top_i = [], []
        for _ in range(TOP_K):
            mk = jnp.max(work, axis=0, keepdims=True)
            ik = jnp.min(jnp.where(work == mk, row, n_e), axis=0, keepdims=True)
            top_v.append(mk)
            top_i.append(ik)
            work = jnp.where(row == ik, -jnp.inf, work)
        ex = [jnp.exp(v - top_v[0]) for v in top_v]
        den = ex[0] + ex[1] + ex[2] + ex[3]
        idx_ref[0, :, tok] = jnp.concatenate(top_i, axis=0)
        wt_ref[0, :, tok] = jnp.concatenate([e / den for e in ex], axis=0)


def _merge(x, ya_t, yb_t, gate_t, wa_t, wb_t, wo, g1, mod2, norm2_g, wr_t, br):
    b, s, d = x.shape
    t = min(MERGE_TILE, s)
    per = s // t
    assert d == SUBLANES * LANES
    full = lambda shape: pl.BlockSpec(shape, lambda bi, i: (0,) * len(shape))
    tok = pl.BlockSpec((1, t, d), lambda bi, i: (bi, i, 0))
    col = lambda rows: pl.BlockSpec((1, rows, t), lambda bi, i: (bi, 0, i))
    return pl.pallas_call(
        _merge_kernel,
        grid=(b, per),
        in_specs=[tok, col(Q_A), col(Q_B), col(2 * d),
                  full(wa_t.shape), full(wb_t.shape), full(wo.shape),
                  pl.BlockSpec((1, 1, d), lambda bi, i: (bi, 0, 0)),
                  pl.BlockSpec((1, 2, d), lambda bi, i: (bi, 0, 0)),
                  full((1, d)), full(wr_t.shape), full(br.shape)],
        out_specs=[tok, pl.BlockSpec((t * SUBLANES, LANES), lambda bi, i: (bi * per + i, 0)),
                   col(TOP_K), col(TOP_K)],
        out_shape=[jax.ShapeDtypeStruct((b, s, d), jnp.float32),
                   jax.ShapeDtypeStruct((b * s * SUBLANES, LANES), jnp.float32),
                   jax.ShapeDtypeStruct((b, TOP_K, s), jnp.int32), jax.ShapeDtypeStruct((b, TOP_K, s), jnp.float32)],
        compiler_params=_params(("arbitrary", "arbitrary")),
        name="merge_router",
    )(x, ya_t, yb_t, gate_t, wa_t, wb_t, wo, g1, mod2, norm2_g, wr_t, br)


def _w1_split_kernel(w_ref, p_ref, g_ref, l_ref):
    half = SPLIT_BLOCK // 2
    for jb in range(w_ref.shape[-1] // SPLIT_BLOCK):
        blk = w_ref[0, :, jb * SPLIT_BLOCK:(jb + 1) * SPLIT_BLOCK].astype(jnp.bfloat16)
        r = _dot(blk, p_ref[...])
        g_ref[0, :, jb * half:(jb + 1) * half] = r[:, :half].astype(jnp.bfloat16)
        l_ref[0, :, jb * half:(jb + 1) * half] = r[:, half:].astype(jnp.bfloat16)


def _w1_split(w1):
    n_e, d, d2 = w1.shape
    rows = 512
    half = SPLIT_BLOCK // 2
    perm = np.zeros((SPLIT_BLOCK, SPLIT_BLOCK), np.float32)
    for j in range(half):
        perm[2 * j, j] = 1.0
        perm[2 * j + 1, half + j] = 1.0
    out = pl.BlockSpec((1, rows, d2 // 2), lambda e, r: (e, r, 0))
    return pl.pallas_call(
        _w1_split_kernel,
        grid=(n_e, d // rows),
        in_specs=[pl.BlockSpec((1, rows, d2), lambda e, r: (e, r, 0)),
                  pl.BlockSpec((SPLIT_BLOCK, SPLIT_BLOCK), lambda e, r: (0, 0))],
        out_specs=[out, out],
        out_shape=[jax.ShapeDtypeStruct((n_e, d, d2 // 2), jnp.bfloat16)] * 2,
        compiler_params=_params(("arbitrary", "arbitrary")),
        name="w1_split",
    )(w1, jnp.asarray(perm, jnp.bfloat16))


def _moe_kernel(tile, offs_ref, tok_ref, wt_ref, h_ref, w1g_ref, w1l_ref, b1g_ref, b1l_ref, w2_ref, b2_ref,
                acc_ref, xg_ref, y_ref, pend_ref):
    ti = pl.program_id(0)
    e = pl.program_id(1)
    n_e = pl.num_programs(1)
    ch = xg_ref.shape[0] // SUBLANES
    off = ti * (n_e + 1)

    def tile_rows(r):
        return pl.ds(pl.multiple_of(r * SUBLANES, SUBLANES), SUBLANES)

    def static_rows(r):
        return slice(r * SUBLANES, (r + 1) * SUBLANES)

    def expert_mlp(xg):
        glu = jnp.minimum(_dot(xg, w1g_ref[0]) + b1g_ref[0], SWIGLU_LIMIT)
        lin = jnp.clip(_dot(xg, w1l_ref[0]) + b1l_ref[0], -SWIGLU_LIMIT, SWIGLU_LIMIT)
        act = glu * jax.nn.sigmoid(SWIGLU_ALPHA * glu) * (lin + 1.0)
        return _dot(act.astype(jnp.bfloat16), w2_ref[0]) + b2_ref[0]

    def gather_rows(dst_ref, base, n):
        def group(i, _):
            r0 = pl.multiple_of(i * ROW_UNROLL, ROW_UNROLL)
            for u in range(ROW_UNROLL):
                dst_ref[tile_rows(r0 + u), :] = h_ref[tile_rows(tok_ref[0, 0, base + r0 + u]), :]
            return 0

        lax.fori_loop(0, (n + ROW_UNROLL - 1) // ROW_UNROLL, group, 0)

    def scatter_rows(src_ref, base, n):
        def group(i, _):
            r0 = pl.multiple_of(i * ROW_UNROLL, ROW_UNROLL)
            toks = [tok_ref[0, 0, base + r0 + u] for u in range(ROW_UNROLL)]
            new = [acc_ref[tile_rows(toks[u]), :] + wt_ref[0, 0, base + r0 + u] * src_ref[tile_rows(r0 + u), :]
                   for u in range(ROW_UNROLL)]
            for u in range(ROW_UNROLL):
                acc_ref[tile_rows(toks[u]), :] = new[u]
            return 0

        full_groups = n // ROW_UNROLL
        lax.fori_loop(0, full_groups, group, 0)

        def row(r, _):
            tok = tok_ref[0, 0, base + r]
            acc_ref[tile_rows(tok), :] = acc_ref[tile_rows(tok), :] + wt_ref[0, 0, base + r] * src_ref[tile_rows(r), :]
            return 0

        lax.fori_loop(full_groups * ROW_UNROLL, n, row, 0)

    start = offs_ref[off + e]
    count = offs_ref[off + e + 1] - start
    n_blocks = jnp.maximum((count + ch - 1) // ch, 1)
    start_next = offs_ref[off + jnp.minimum(e + 1, n_e - 1)]

    @pl.when((ti == 0) & (e == 0))
    def _():
        for ref in (xg_ref, y_ref):
            ref[...] = jnp.zeros_like(ref)

    @pl.when(e == 0)
    def _():
        acc_ref[...] = jnp.zeros_like(acc_ref)
        pend_ref[0] = 0
        pend_ref[1] = 0
        gather_rows(xg_ref, start, jnp.minimum(count, ch))

    def block_body(rows, k, base_prev, n_prev):
        xg = _from_token_tiles(xg_ref, 0, rows).astype(jnp.bfloat16)

        base_next = jnp.where(k + 1 < n_blocks, start + (k + 1) * ch, start_next)
        for r in range(ch):
            xg_ref[static_rows(r), :] = h_ref[tile_rows(tok_ref[0, 0, base_next + r]), :]

        for g in range(ch // ROW_UNROLL):
            toks, new = [], []
            for u in range(ROW_UNROLL):
                r = g * ROW_UNROLL + u
                valid = r < n_prev
                tok = jnp.where(valid, tok_ref[0, 0, base_prev + r], tile + u)
                w = jnp.where(valid, wt_ref[0, 0, base_prev + r], 0.0)
                toks.append(tok)
                new.append(acc_ref[tile_rows(tok), :] + w * y_ref[static_rows(r), :])
            for u in range(ROW_UNROLL):
                acc_ref[tile_rows(toks[u]), :] = new[u]

        _to_token_tiles(y_ref, 0, rows, expert_mlp(xg))

    def block(k, pending):
        base_prev, n_prev = pending
        n = jnp.minimum(ch, count - k * ch)
        small = n <= ch // 2
        pl.when(small)(functools.partial(block_body, ch // 2, k, base_prev, n_prev))
        pl.when(jnp.logical_not(small))(functools.partial(block_body, ch, k, base_prev, n_prev))
        return start + k * ch, n

    base_prev, n_prev = lax.fori_loop(0, n_blocks, block, (pend_ref[0], pend_ref[1]))
    pend_ref[0] = base_prev
    pend_ref[1] = n_prev

    @pl.when(e == n_e - 1)
    def _():
        scatter_rows(y_ref, base_prev, n_prev)


def _moe(offs, tok_sorted, wt_sorted, h2_tiles, w1g, w1l, b1g, b1l, w2, b2, tile):
    n_tok = h2_tiles.shape[0] // SUBLANES
    n_e, d, d_e = w1g.shape
    n_t = n_tok // tile
    a = tok_sorted.shape[-1]
    assert d == SUBLANES * LANES and MOE_CHUNK % ROW_UNROLL == 0
    smem = lambda: pl.BlockSpec((1, 1, a), lambda ti, e, o: (ti, 0, 0), memory_space=pltpu.SMEM)
    ex = lambda shape: pl.BlockSpec((1,) + shape, lambda ti, e, o: (e, 0, 0))
    out_rows = (tile + ROW_UNROLL) * SUBLANES
    rows = pltpu.VMEM((MOE_CHUNK * SUBLANES, LANES), jnp.float32)
    once = pl.Buffered(1)
    grid_spec = pltpu.PrefetchScalarGridSpec(
        num_scalar_prefetch=1,
        grid=(n_t, n_e),
        in_specs=[smem(), smem(),
                  pl.BlockSpec((tile * SUBLANES, LANES), lambda ti, e, o: (ti, 0), pipeline_mode=once),
                  ex((d, d_e)), ex((d, d_e)), ex((1, d_e)), ex((1, d_e)), ex((d_e, d)), ex((1, d))],
        out_specs=pl.BlockSpec((out_rows, LANES), lambda ti, e, o: (ti, 0), pipeline_mode=once),
        scratch_shapes=[rows, rows, pltpu.SMEM((2,), jnp.int32)],
    )
    return pl.pallas_call(
        functools.partial(_moe_kernel, tile),
        grid_spec=grid_spec,
        out_shape=jax.ShapeDtypeStruct((n_t * out_rows, LANES), jnp.float32),
        compiler_params=pltpu.CompilerParams(dimension_semantics=("arbitrary", "arbitrary"),
                                             vmem_limit_bytes=MOE_VMEM_LIMIT),
        name="moe",
    )(offs, tok_sorted, wt_sorted, h2_tiles, w1g, w1l, b1g, b1l, w2, b2)


def _final_kernel(sub, x_ref, m_ref, g2_ref, g_ref, o_ref):
    t = x_ref.shape[1]
    row0 = lax.rem(pl.program_id(1), sub) * t
    x = x_ref[0] + g2_ref[0] * _from_token_tiles(m_ref, row0, t)
    ms = jnp.mean(x * x, axis=-1, keepdims=True)
    o_ref[0] = x * lax.rsqrt(ms + RMS_EPS) * g_ref[...]


def _final(x1, moe_tiles, g2, final_g, moe_tile):
    b, s, d = x1.shape
    t = min(FINAL_TILE, moe_tile)
    per = s // t
    sub = moe_tile // t
    moe_rows = (moe_tile + ROW_UNROLL) * SUBLANES
    tok = pl.BlockSpec((1, t, d), lambda bi, i: (bi, i, 0))
    return pl.pallas_call(
        functools.partial(_final_kernel, sub),
        grid=(b, per),
        in_specs=[tok, pl.BlockSpec((moe_rows, LANES), lambda bi, i: (bi * (per // sub) + i // sub, 0)),
                  pl.BlockSpec((1, 1, d), lambda bi, i: (bi, 0, 0)),
                  pl.BlockSpec((1, d), lambda bi, i: (0, 0))],
        out_specs=tok,
        out_shape=jax.ShapeDtypeStruct((b, s, d), jnp.float32),
        compiler_params=_params(("arbitrary", "arbitrary")),
        name="final_norm",
    )(x1, moe_tiles, g2, final_g)


def _rope_tables(s, n_ctx):
    inv = ROPE_THETA ** (-(jnp.arange(ROPE_QUARTER, dtype=jnp.float32) * 2.0 / ROPE_HALF))
    pos = jnp.arange(s)
    ang_r = (pos // GRID_W).astype(jnp.float32)[None, :] * inv[:, None]
    ang_c = (pos % GRID_W).astype(jnp.float32)[None, :] * inv[:, None]
    ang = jnp.concatenate([ang_r, ang_c], axis=0)
    cos = jnp.concatenate([jnp.cos(ang), jnp.ones((ROPE_HALF, n_ctx), jnp.float32)], axis=1)
    sin = jnp.concatenate([jnp.sin(ang), jnp.zeros((ROPE_HALF, n_ctx), jnp.float32)], axis=1)
    return cos, sin


def _split_bf16(w):
    hi = w.astype(jnp.bfloat16)
    return jnp.stack([hi, (w - hi.astype(jnp.float32)).astype(jnp.bfloat16)])


def _route_lists(top_idx, top_w, tile):
    b, k, s = top_idx.shape
    n_t = b * s // tile
    per = s // tile
    e_flat = top_idx.reshape(b, k, per, tile).transpose(0, 2, 1, 3).reshape(n_t, k * tile)
    w_flat = top_w.reshape(b, k, per, tile).transpose(0, 2, 1, 3).reshape(n_t, k * tile)
    tok = jnp.tile(jnp.arange(tile, dtype=jnp.int32), k)[None, :]
    key_sorted, w_sorted = lax.sort((e_flat * tile + tok, w_flat), dimension=1, num_keys=1)
    tok_sorted = lax.rem(key_sorted, tile)
    counts = jnp.sum(e_flat[:, :, None] == jnp.arange(N_EXPERTS, dtype=jnp.int32)[None, None, :], axis=1,
                     dtype=jnp.int32)
    offs = jnp.concatenate([jnp.zeros((n_t, 1), jnp.int32), jnp.cumsum(counts, axis=1, dtype=jnp.int32)], axis=1)
    pad = ((0, 0), (0, MOE_CHUNK))
    return offs.reshape(-1), jnp.pad(tok_sorted, pad)[:, None, :], jnp.pad(w_sorted, pad)[:, None, :]


def kernel(x, c, ctx, c_ctx, w_mod, b_mod, norm1_g, norm2_g, w_in, q_norm_g, k_norm_g, sink, w_br_a, w_br_b,
           w_o, w_router, b_router, w_e1, b_e1, w_e2, b_e2, final_g):
    b, s, d = x.shape
    n_ctx = ctx.shape[1]
    assert w_mod.shape[0] == 1
    bf = jnp.bfloat16
    t = TOKEN_TILE
    cos, sin = _rope_tables(s, IN_PROJ_SUBS * t)
    moe_tile = min(MOE_TILE, s)
    for l in range(1):
        rows = ((b + 1 + 7) // 8) * 8
        cond = jnp.zeros((rows, d), jnp.float32).at[:b].set(c).at[b].set(c_ctx)
        mods = _adaln(cond, w_mod[l], b_mod[l])
        sh1, sc1, g1, sh2, sc2, g2 = jnp.split(mods, 6, axis=-1)
        mod1 = jnp.stack([jnp.stack([sh1[:b], sc1[:b]], axis=1),
                          jnp.broadcast_to(jnp.stack([sh1[b], sc1[b]], axis=0)[None], (b, 2, d))], axis=1)
        mod2 = jnp.stack([sh2[:b], sc2[:b]], axis=1)

        w = w_in[l]
        kv = w[:, :4 * KV_W]
        wk_t = jnp.concatenate([kv[:, 0:KV_W], kv[:, 2 * KV_W:3 * KV_W]], axis=1).T.astype(bf)
        wv_t = jnp.concatenate([kv[:, KV_W:2 * KV_W], kv[:, 3 * KV_W:4 * KV_W]], axis=1).T.astype(bf)
        wq_t = w[:, 4 * KV_W:4 * KV_W + Q_A + Q_B].T.astype(bf)
        wg_t = w[:, 4 * KV_W + Q_A + Q_B:].T.astype(bf)
        qg = jnp.broadcast_to((q_norm_g[l] * (ATTN_SCALE * LOG2E))[:, None], (HEAD_DIM, t))
        kg = jnp.broadcast_to(k_norm_g[l][:, None], (HEAD_DIM, t))

        qa_t, qb_t, ka, kb, va_t, vb_t, gate_t = _in_proj(
            x, ctx, mod1, norm1_g[l][None], wq_t, wk_t, wv_t, wg_t, qg, kg, cos, sin)

        n_key_tiles = (s + n_ctx) // t
        ya_t = _attention_global(128, True, n_key_tiles, qa_t, ka, va_t)
        yb_t = _attention_window(n_key_tiles, sink[l], qb_t, kb, vb_t)

        x1, h2, top_idx, top_w = _merge(
            x, ya_t, yb_t, gate_t, w_br_a[l].T.astype(bf), w_br_b[l].T.astype(bf), w_o[l].astype(bf),
            g1[:b, None, :], mod2, norm2_g[l][None], _split_bf16(w_router[l].T), b_router[l][:, None])

        offs, tok_sorted, w_sorted = _route_lists(top_idx, top_w, moe_tile)
        w1g, w1l = _w1_split(w_e1[l])
        moe_tiles = _moe(offs, tok_sorted, w_sorted, h2, w1g, w1l,
                         b_e1[l][:, None, 0::2], b_e1[l][:, None, 1::2],
                         w_e2[l].astype(bf), b_e2[l][:, None, :], moe_tile)
        out = _final(x1, moe_tiles, g2[:b, None, :], final_g[None], moe_tile)
    return out
```

```python
import functools

import numpy as np
import jax
import jax.numpy as jnp
from jax import lax
from jax.experimental import pallas as pl
from jax.experimental.pallas import tpu as pltpu

HEAD_DIM = 64
KV_HEADS = 2
A_HEADS = 8
B_HEADS = 8
GROUP = A_HEADS // KV_HEADS
GRID_W = 64
WINDOW = 128
ROPE_HALF = HEAD_DIM // 2
ROPE_QUARTER = ROPE_HALF // 2
ROPE_THETA = 10000.0
N_EXPERTS = 32
TOP_K = 4
SWIGLU_ALPHA = 1.702
SWIGLU_LIMIT = 7.0
RMS_EPS = 1e-6
ATTN_SCALE = HEAD_DIM ** -0.5
LOG2E = 1.4426950408889634
NEG_INF = -1e30
KV_W = KV_HEADS * HEAD_DIM
Q_A = A_HEADS * HEAD_DIM
Q_B = B_HEADS * HEAD_DIM
V_ROWS = HEAD_DIM + 16

SUBLANES = 8
LANES = 128
TOKEN_TILE = 256
IN_PROJ_SUBS = 4
MERGE_TILE = 1024
FINAL_TILE = 512
MOE_TILE = 4096
MOE_CHUNK = 256
MOE_VMEM_LIMIT = 60 * 1024 * 1024
ROW_UNROLL = 8
SPLIT_BLOCK = 256
VMEM_LIMIT = 56 * 1024 * 1024

_NT = (((1,), (1,)), ((), ()))


def _dot_nt(a, b, precision=None):
    return lax.dot_general(a, b, _NT, preferred_element_type=jnp.float32, precision=precision)


def _dot(a, b, precision=None):
    return jnp.dot(a, b, preferred_element_type=jnp.float32, precision=precision)


def _params(sem):
    return pltpu.CompilerParams(dimension_semantics=sem, vmem_limit_bytes=VMEM_LIMIT)


def _adaln_kernel(cond_ref, w_ref, b_ref, o_ref):
    cond = cond_ref[...]
    act = cond * jax.nn.sigmoid(cond)
    o_ref[...] = _dot(act, w_ref[...], precision=lax.Precision.HIGHEST) + b_ref[...]


def _adaln(cond, w, b):
    rows, d = cond.shape
    n = w.shape[1]
    tn = 1024
    return pl.pallas_call(
        _adaln_kernel,
        grid=(n // tn,),
        in_specs=[pl.BlockSpec((rows, d), lambda j: (0, 0)),
                  pl.BlockSpec((d, tn), lambda j: (0, j)),
                  pl.BlockSpec((1, tn), lambda j: (0, j))],
        out_specs=pl.BlockSpec((rows, tn), lambda j: (0, j)),
        out_shape=jax.ShapeDtypeStruct((rows, n), jnp.float32),
        compiler_params=_params(("arbitrary",)),
        name="adaln",
    )(cond, w, b.reshape(1, n))


def _rope_t(xh, cos, sin):
    q = ROPE_QUARTER
    a0, a1, b0, b1 = xh[0:q], xh[q:2 * q], xh[2 * q:3 * q], xh[3 * q:4 * q]
    cr, cc = cos[0:q], cos[q:2 * q]
    sr, sc = sin[0:q], sin[q:2 * q]
    return jnp.concatenate([a0 * cr - a1 * sr, a1 * cr + a0 * sr,
                            b0 * cc - b1 * sc, b1 * cc + b0 * sc], axis=0)


def _head_norm_t(xh, g):
    ms = jnp.mean(xh * xh, axis=0, keepdims=True)
    return xh * lax.rsqrt(ms + RMS_EPS) * g


def _mod_norm(x, g, shift, scale):
    ms = jnp.mean(x * x, axis=-1, keepdims=True)
    return x * lax.rsqrt(ms + RMS_EPS) * g * (1.0 + scale) + shift


def _in_proj_kernel(n_lat, x_ref, ctx_ref, mod_ref, g_ref, wq_ref, wk_ref, wv_ref, wg_ref,
                    qg_ref, kg_ref, cos_ref, sin_ref,
                    qa_ref, qb_ref, ka_ref, kb_ref, va_ref, vb_ref, gate_ref, h_ref):
    i = pl.program_id(1)
    t = TOKEN_TILE
    shift = mod_ref[0, 0, 0:1, :]
    scale = mod_ref[0, 0, 1:2, :]
    is_lat = i < n_lat

    for sub in range(IN_PROJ_SUBS):
        @pl.when(is_lat)
        def _(sub=sub):
            h_ref[sub] = _mod_norm(x_ref[0, sub * t:(sub + 1) * t, :], g_ref[...], shift, scale).astype(jnp.bfloat16)

        @pl.when(jnp.logical_not(is_lat))
        def _(sub=sub):
            if sub == 0:
                h_ref[sub] = _mod_norm(ctx_ref[0], g_ref[...], shift, scale).astype(jnp.bfloat16)
            else:
                h_ref[sub] = jnp.zeros(h_ref.shape[1:], jnp.bfloat16)

    kg = kg_ref[...]
    ones = jnp.ones((V_ROWS - HEAD_DIM, t), jnp.float32)
    for sub in range(IN_PROJ_SUBS):
        h = h_ref[sub]
        cos = cos_ref[:, sub * t:(sub + 1) * t]
        sin = sin_ref[:, sub * t:(sub + 1) * t]
        kt = _dot_nt(wk_ref[...], h)
        ka = [_rope_t(_head_norm_t(kt[j * HEAD_DIM:(j + 1) * HEAD_DIM], kg), cos, sin) for j in range(KV_HEADS)]
        kb = [_rope_t(kt[KV_W + j * HEAD_DIM:KV_W + (j + 1) * HEAD_DIM], cos, sin) for j in range(KV_HEADS)]
        ka_ref[0, sub] = jnp.concatenate(ka, axis=0).T.astype(jnp.bfloat16)
        kb_ref[0, sub] = jnp.concatenate(kb, axis=0).T.astype(jnp.bfloat16)
        vt = _dot_nt(wv_ref[...], h)
        for br, v_ref in enumerate((va_ref, vb_ref)):
            rows = []
            for j in range(KV_HEADS):
                rows += [vt[br * KV_W + j * HEAD_DIM:br * KV_W + (j + 1) * HEAD_DIM], ones]
            v_ref[0, sub] = jnp.concatenate(rows, axis=0).astype(jnp.bfloat16)

    @pl.when(is_lat)
    def _():
        qg = qg_ref[...]
        for sub in range(IN_PROJ_SUBS):
            h = h_ref[sub]
            tok = slice(sub * t, (sub + 1) * t)
            cos = cos_ref[:, tok]
            sin = sin_ref[:, tok]
            for half in range(2):
                qt = _dot_nt(wq_ref[half * Q_A:(half + 1) * Q_A, :], h)
                for hd in range(A_HEADS):
                    xh = qt[hd * HEAD_DIM:(hd + 1) * HEAD_DIM]
                    if half == 0:
                        out = _rope_t(_head_norm_t(xh, qg), cos, sin)
                        qa_ref[0, hd * HEAD_DIM:(hd + 1) * HEAD_DIM, tok] = out.astype(jnp.bfloat16)
                    else:
                        out = _rope_t(xh * (ATTN_SCALE * LOG2E), cos, sin)
                        qb_ref[0, hd * HEAD_DIM:(hd + 1) * HEAD_DIM, tok] = out.astype(jnp.bfloat16)
            rows = 512
            for c in range(wg_ref.shape[0] // rows):
                gt = _dot_nt(wg_ref[c * rows:(c + 1) * rows, :], h)
                gate_ref[0, c * rows:(c + 1) * rows, tok] = jax.nn.sigmoid(gt).astype(jnp.bfloat16)


def _in_proj(x, ctx, mod, norm_g, wq_t, wk_t, wv_t, wg_t, qg, kg, cos, sin):
    b, s, d = x.shape
    t = TOKEN_TILE
    subs = IN_PROJ_SUBS
    st = subs * t
    assert ctx.shape[1] == t and s % st == 0
    n_lat = s // st
    n_tiles = (n_lat + 1) * subs
    last = n_lat - 1
    full = lambda shape: pl.BlockSpec(shape, lambda bi, i: (0,) * len(shape))
    q_spec = pl.BlockSpec((1, Q_A, st), lambda bi, i: (bi, 0, jnp.minimum(i, last)))
    k_spec = pl.BlockSpec((1, subs, t, KV_W), lambda bi, i: (bi, i, 0, 0))
    v_spec = pl.BlockSpec((1, subs, KV_HEADS * V_ROWS, t), lambda bi, i: (bi, i, 0, 0))
    bf = jnp.bfloat16
    return pl.pallas_call(
        functools.partial(_in_proj_kernel, n_lat),
        grid=(b, n_lat + 1),
        in_specs=[pl.BlockSpec((1, st, d), lambda bi, i: (bi, jnp.minimum(i, last), 0)),
                  pl.BlockSpec((1, t, d), lambda bi, i: (bi, 0, 0)),
                  pl.BlockSpec((1, 1, 2, d), lambda bi, i: (bi, i // n_lat, 0, 0)),
                  full((1, d)), full(wq_t.shape), full(wk_t.shape), full(wv_t.shape), full(wg_t.shape),
                  full((HEAD_DIM, t)), full((HEAD_DIM, t)),
                  pl.BlockSpec((ROPE_HALF, st), lambda bi, i: (0, i)),
                  pl.BlockSpec((ROPE_HALF, st), lambda bi, i: (0, i))],
        out_specs=[q_spec, q_spec, k_spec, k_spec, v_spec, v_spec,
                   pl.BlockSpec((1, 2 * d, st), lambda bi, i: (bi, 0, jnp.minimum(i, last)))],
        out_shape=[jax.ShapeDtypeStruct((b, Q_A, s), bf), jax.ShapeDtypeStruct((b, Q_B, s), bf),
                   jax.ShapeDtypeStruct((b, n_tiles, t, KV_W), bf), jax.ShapeDtypeStruct((b, n_tiles, t, KV_W), bf),
                   jax.ShapeDtypeStruct((b, n_tiles, KV_HEADS * V_ROWS, t), bf),
                   jax.ShapeDtypeStruct((b, n_tiles, KV_HEADS * V_ROWS, t), bf),
                   jax.ShapeDtypeStruct((b, 2 * d, s), bf)],
        scratch_shapes=[pltpu.VMEM((subs, t, d), bf)],
        compiler_params=_params(("arbitrary", "arbitrary")),
        name="in_proj",
    )(x, ctx, mod, norm_g, wq_t, wk_t, wv_t, wg_t, qg, kg, cos, sin)


def _attn_window_kernel(subs, n_lat, n_tiles, sink_ref, q_ref, k_ref, v_ref, y_ref, s_ref):
    t = tq = TOKEN_TILE
    n = GROUP * tq
    assert WINDOW * 2 == t
    heads = [[kvh * GROUP + g for g in range(GROUP)] for kvh in range(KV_HEADS)]
    n_slots = s_ref.shape[1] // subs
    sink = [jnp.concatenate([jnp.full((1, tq), sink_ref[hd] * LOG2E, jnp.float32) for hd in heads[kvh]], axis=1)
            for kvh in range(KV_HEADS)]
    c_half = lax.rem(lax.broadcasted_iota(jnp.int32, (WINDOW, n), 1), tq)
    r_half = lax.broadcasted_iota(jnp.int32, (WINDOW, n), 0)
    c_full = lax.rem(lax.broadcasted_iota(jnp.int32, (t, n), 1), tq)
    r_full = lax.broadcasted_iota(jnp.int32, (t, n), 0)
    acc_row = lax.broadcasted_iota(jnp.int32, (V_ROWS, n), 0)

    for sub in range(subs):
        qi = pl.program_id(1) * subs + sub
        cols = slice(sub * tq, (sub + 1) * tq)
        qpad = []
        for kvh in range(KV_HEADS):
            qcat = jnp.concatenate([q_ref[0, hd * HEAD_DIM:(hd + 1) * HEAD_DIM, cols] for hd in heads[kvh]], axis=1)
            zeros = jnp.zeros_like(qcat)
            qpad.append(jnp.concatenate([qcat, zeros] if kvh == 0 else [zeros, qcat], axis=0))

        off_prev = jnp.where(qi >= 1, 0, 2 * t)
        off_next = jnp.where(qi + 1 < n_lat, 0, 2 * t)
        pieces = [
            (jnp.maximum(qi - 1, 0), WINDOW, WINDOW, r_half >= c_half + off_prev),
            (qi, 0, t, jnp.abs(r_full - c_full) <= WINDOW),
            (jnp.minimum(qi + 1, n_lat - 1), 0, WINDOW, c_half - r_half >= WINDOW + off_next),
        ] + [(kb, 0, t, None) for kb in range(n_lat, n_tiles)]

        m8 = [jnp.broadcast_to(sink[kvh], (8, n)) for kvh in range(KV_HEADS)]
        for slot, (kb, lo, rows, ok) in enumerate(pieces):
            kblk = k_ref[0, kb, lo:lo + rows, :]
            for kvh in range(KV_HEADS):
                s = _dot(kblk, qpad[kvh])
                if ok is not None:
                    s = jnp.where(ok, s, NEG_INF)
                s_ref[kvh, sub * n_slots + slot, 0:rows, :] = s
                m8[kvh] = jnp.maximum(m8[kvh], jnp.max(s.reshape(rows // 8, 8, n), axis=0))
        m = [jnp.max(m8[kvh], axis=0, keepdims=True) for kvh in range(KV_HEADS)]

        accs = [jnp.where(acc_row == HEAD_DIM, jnp.exp2(sink[kvh] - m[kvh]), 0.0) for kvh in range(KV_HEADS)]
        for slot, (kb, lo, rows, _) in enumerate(pieces):
            for kvh in range(KV_HEADS):
                p = jnp.exp2(s_ref[kvh, sub * n_slots + slot, 0:rows, :] - m[kvh]).astype(jnp.bfloat16)
                vblk = v_ref[0, kb, kvh * V_ROWS:(kvh + 1) * V_ROWS, lo:lo + rows]
                accs[kvh] = accs[kvh] + _dot(vblk, p)
        for kvh in range(KV_HEADS):
            out = accs[kvh][0:HEAD_DIM] / accs[kvh][HEAD_DIM:HEAD_DIM + 1]
            for g, hd in enumerate(heads[kvh]):
                y_ref[0, hd * HEAD_DIM:(hd + 1) * HEAD_DIM, cols] = out[:, g * tq:(g + 1) * tq].astype(jnp.bfloat16)


def _attn_global_kernel(tq, n_tiles, n_q, unroll, q_ref, k_ref, v_ref, y_ref, s0_ref, s1_ref, m0_ref, m1_ref):
    j = pl.program_id(0)
    t = TOKEN_TILE
    n = GROUP * tq
    heads = [[kvh * GROUP + g for g in range(GROUP)] for kvh in range(KV_HEADS)]
    bufs = ((s0_ref, m0_ref), (s1_ref, m1_ref))

    def padded_q():
        qpad = []
        for kvh in range(KV_HEADS):
            qcat = jnp.concatenate([q_ref[0, hd * HEAD_DIM:(hd + 1) * HEAD_DIM, :] for hd in heads[kvh]], axis=1)
            zeros = jnp.zeros_like(qcat)
            qpad.append(jnp.concatenate([qcat, zeros] if kvh == 0 else [zeros, qcat], axis=0))
        return qpad

    def scores(s_ref, kb, m8, qpad):
        kblk = k_ref[0, kb]
        out = []
        for kvh in range(KV_HEADS):
            s = _dot(kblk, qpad[kvh])
            s_ref[kvh, kb] = s
            out.append(jnp.maximum(m8[kvh], jnp.max(s.reshape(t // 8, 8, n), axis=0)))
        return tuple(out)

    def weigh(s_ref, kb, m, accs):
        out = []
        for kvh in range(KV_HEADS):
            p = jnp.exp2(s_ref[kvh, kb] - m[kvh]).astype(jnp.bfloat16)
            vblk = v_ref[0, kb, kvh * V_ROWS:(kvh + 1) * V_ROWS, :]
            out.append(accs[kvh] + _dot(vblk, p))
        return tuple(out)

    def keep_maxima(m_ref, m8):
        for kvh in range(KV_HEADS):
            m_ref[kvh] = m8[kvh]

    def maxima(m_ref):
        return [jnp.max(m_ref[kvh], axis=0, keepdims=True) for kvh in range(KV_HEADS)]

    def write(accs):
        for kvh in range(KV_HEADS):
            out = accs[kvh][0:HEAD_DIM] / accs[kvh][HEAD_DIM:HEAD_DIM + 1]
            for g, hd in enumerate(heads[kvh]):
                y_ref[0, hd * HEAD_DIM:(hd + 1) * HEAD_DIM, :] = out[:, g * tq:(g + 1) * tq].astype(jnp.bfloat16)

    m8_0 = tuple(jnp.full((8, n), -jnp.inf, jnp.float32) for _ in range(KV_HEADS))
    acc_0 = tuple(jnp.zeros((V_ROWS, n), jnp.float32) for _ in range(KV_HEADS))

    @pl.when(j == 0)
    def _():
        s_w, m_w = bufs[0]
        qpad = padded_q()
        keep_maxima(m_w, lax.fori_loop(0, n_tiles, lambda kb, c: scores(s_w, kb, c, qpad), m8_0, unroll=unroll))

    for parity in range(2):
        (s_w, m_w), (s_r, m_r) = bufs[parity], bufs[1 - parity]

        @pl.when((j > 0) & (j < n_q) & (lax.rem(j, 2) == parity))
        def _():
            qpad = padded_q()
            m = maxima(m_r)
            m8, accs = lax.fori_loop(
                0, n_tiles, lambda kb, c: (scores(s_w, kb, c[0], qpad), weigh(s_r, kb, m, c[1])),
                (m8_0, acc_0), unroll=unroll)
            keep_maxima(m_w, m8)
            write(accs)

    @pl.when(j == n_q)
    def _():
        s_r, m_r = bufs[(n_q - 1) % 2]
        m = maxima(m_r)
        write(lax.fori_loop(0, n_tiles, lambda kb, c: weigh(s_r, kb, m, c), acc_0, unroll=unroll))


def _attention_global(tq, unroll, n_tiles, q_t, k, v_t):
    b, hq, s = q_t.shape
    per = s // tq
    n_q = b * per
    n = GROUP * tq
    scored = lambda j: jnp.minimum(j, n_q - 1)
    weighed = lambda j: jnp.maximum(j - 1, 0)
    return pl.pallas_call(
        functools.partial(_attn_global_kernel, tq, n_tiles, n_q, unroll),
        grid=(n_q + 1,),
        in_specs=[pl.BlockSpec((1, hq, tq), lambda j: (scored(j) // per, 0, scored(j) % per)),
                  pl.BlockSpec((1,) + k.shape[1:], lambda j: (scored(j) // per, 0, 0, 0)),
                  pl.BlockSpec((1,) + v_t.shape[1:], lambda j: (weighed(j) // per, 0, 0, 0))],
        out_specs=pl.BlockSpec((1, hq, tq), lambda j: (weighed(j) // per, 0, weighed(j) % per)),
        out_shape=jax.ShapeDtypeStruct((b, hq, s), jnp.bfloat16),
        scratch_shapes=[pltpu.VMEM((KV_HEADS, n_tiles, TOKEN_TILE, n), jnp.float32)] * 2
        + [pltpu.VMEM((KV_HEADS, 8, n), jnp.float32)] * 2,
        compiler_params=_params(("arbitrary",)),
        name="attn_global",
    )(q_t, k, v_t)


def _attention_window(n_tiles, sink, q_t, k, v_t):
    b, hq, s = q_t.shape
    t = TOKEN_TILE
    n_lat = s // t
    subs = 4 if n_lat % 4 == 0 else (2 if n_lat % 2 == 0 else 1)
    slots = 3 + (n_tiles - n_lat)
    grid_spec = pltpu.PrefetchScalarGridSpec(
        num_scalar_prefetch=1,
        grid=(b, n_lat // subs),
        in_specs=[pl.BlockSpec((1, hq, subs * t), lambda bi, qi, sk: (bi, 0, qi)),
                  pl.BlockSpec((1,) + k.shape[1:], lambda bi, qi, sk: (bi, 0, 0, 0)),
                  pl.BlockSpec((1,) + v_t.shape[1:], lambda bi, qi, sk: (bi, 0, 0, 0))],
        out_specs=pl.BlockSpec((1, hq, subs * t), lambda bi, qi, sk: (bi, 0, qi)),
        scratch_shapes=[pltpu.VMEM((KV_HEADS, subs * slots, t, GROUP * t), jnp.float32)],
    )
    return pl.pallas_call(
        functools.partial(_attn_window_kernel, subs, n_lat, n_tiles),
        grid_spec=grid_spec,
        out_shape=jax.ShapeDtypeStruct((b, hq, s), jnp.bfloat16),
        compiler_params=_params(("arbitrary", "arbitrary")),
        name="attn_window",
    )(sink, q_t, k, v_t)


def _to_token_tiles(ref, row0, rows, val):
    for j in range(SUBLANES):
        ref[pl.ds(row0 * SUBLANES + j, rows, stride=SUBLANES), :] = val[:, j * LANES:(j + 1) * LANES]


def _from_token_tiles(ref, row0, rows):
    return jnp.concatenate([ref[pl.ds(row0 * SUBLANES + j, rows, stride=SUBLANES), :] for j in range(SUBLANES)],
                           axis=1)


def _merge_kernel(x_ref, ya_ref, yb_ref, gate_ref, wa_ref, wb_ref, wo_ref, g1_ref, mod_ref, g2_ref,
                  wr_ref, br_ref, x1_ref, h2_ref, idx_ref, wt_ref):
    d = x_ref.shape[-1]
    t = TOKEN_TILE
    for sub in range(x_ref.shape[1] // t):
        tok = slice(sub * t, (sub + 1) * t)
        za = _dot(wa_ref[...], ya_ref[0, :, tok])
        zb = _dot(wb_ref[...], yb_ref[0, :, tok])
        mt = (gate_ref[0, 0:d, tok].astype(jnp.float32) * za
              + gate_ref[0, d:2 * d, tok].astype(jnp.float32) * zb)
        o = _dot(mt.T.astype(jnp.bfloat16), wo_ref[...])
        x1 = x_ref[0, tok, :] + g1_ref[0] * o
        x1_ref[0, tok, :] = x1
        h2 = _mod_norm(x1, g2_ref[...], mod_ref[0, 0:1, :], mod_ref[0, 1:2, :])
        _to_token_tiles(h2_ref, sub * t, t, h2)
        h_hi = h2.astype(jnp.bfloat16)
        h_lo = (h2 - h_hi.astype(jnp.float32)).astype(jnp.bfloat16)
        logits = (_dot_nt(wr_ref[0], h_hi) + _dot_nt(wr_ref[0], h_lo) + _dot_nt(wr_ref[1], h_hi)
                  + br_ref[...])
        n_e = logits.shape[0]
        row = lax.broadcasted_iota(jnp.int32, logits.shape, 0)
        work = logits
        top_v, top_i = [], []
        for _ in range(TOP_K):
            mk = jnp.max(work, axis=0, keepdims=True)
            ik = jnp.min(jnp.where(work == mk, row, n_e), axis=0, keepdims=True)
            top_v.append(mk)
            top_i.append(ik)
            work = jnp.where(row == ik, -jnp.inf, work)
        ex = [jnp.exp(v - top_v[0]) for v in top_v]
        den = ex[0] + ex[1] + ex[2] + ex[3]
        idx_ref[0, :, tok] = jnp.concatenate(top_i, axis=0)
        wt_ref[0, :, tok] = jnp.concatenate([e / den for e in ex], axis=0)


def _merge(x, ya_t, yb_t, gate_t, wa_t, wb_t, wo, g1, mod2, norm2_g, wr_t, br):
    b, s, d = x.shape
    t = min(MERGE_TILE, s)
    per = s // t
    assert d == SUBLANES * LANES
    full = lambda shape: pl.BlockSpec(shape, lambda bi, i: (0,) * len(shape))
    tok = pl.BlockSpec((1, t, d), lambda bi, i: (bi, i, 0))
    col = lambda rows: pl.BlockSpec((1, rows, t), lambda bi, i: (bi, 0, i))
    return pl.pallas_call(
        _merge_kernel,
        grid=(b, per),
        in_specs=[tok, col(Q_A), col(Q_B), col(2 * d),
                  full(wa_t.shape), full(wb_t.shape), full(wo.shape),
                  pl.BlockSpec((1, 1, d), lambda bi, i: (bi, 0, 0)),
                  pl.BlockSpec((1, 2, d), lambda bi, i: (bi, 0, 0)),
                  full((1, d)), full(wr_t.shape), full(br.shape)],
        out_specs=[tok, pl.BlockSpec((t * SUBLANES, LANES), lambda bi, i: (bi * per + i, 0)),
                   col(TOP_K), col(TOP_K)],
        out_shape=[jax.ShapeDtypeStruct((b, s, d), jnp.float32),
                   jax.ShapeDtypeStruct((b * s * SUBLANES, LANES), jnp.float32),
                   jax.ShapeDtypeStruct((b, TOP_K, s), jnp.int32), jax.ShapeDtypeStruct((b, TOP_K, s), jnp.float32)],
        compiler_params=_params(("arbitrary", "arbitrary")),
        name="merge_router",
    )(x, ya_t, yb_t, gate_t, wa_t, wb_t, wo, g1, mod2, norm2_g, wr_t, br)


def _w1_split_kernel(w_ref, p_ref, g_ref, l_ref):
    half = SPLIT_BLOCK // 2
    for jb in range(w_ref.shape[-1] // SPLIT_BLOCK):
        blk = w_ref[0, :, jb * SPLIT_BLOCK:(jb + 1) * SPLIT_BLOCK].astype(jnp.bfloat16)
        r = _dot(blk, p_ref[...])
        g_ref[0, :, jb * half:(jb + 1) * half] = r[:, :half].astype(jnp.bfloat16)
        l_ref[0, :, jb * half:(jb + 1) * half] = r[:, half:].astype(jnp.bfloat16)


def _w1_split(w1):
    n_e, d, d2 = w1.shape
    rows = 512
    half = SPLIT_BLOCK // 2
    perm = np.zeros((SPLIT_BLOCK, SPLIT_BLOCK), np.float32)
    for j in range(half):
        perm[2 * j, j] = 1.0
        perm[2 * j + 1, half + j] = 1.0
    out = pl.BlockSpec((1, rows, d2 // 2), lambda e, r: (e, r, 0))
    return pl.pallas_call(
        _w1_split_kernel,
        grid=(n_e, d // rows),
        in_specs=[pl.BlockSpec((1, rows, d2), lambda e, r: (e, r, 0)),
                  pl.BlockSpec((SPLIT_BLOCK, SPLIT_BLOCK), lambda e, r: (0, 0))],
        out_specs=[out, out],
        out_shape=[jax.ShapeDtypeStruct((n_e, d, d2 // 2), jnp.bfloat16)] * 2,
        compiler_params=_params(("arbitrary", "arbitrary")),
        name="w1_split",
    )(w1, jnp.asarray(perm, jnp.bfloat16))


def _moe_kernel(tile, offs_ref, tok_ref, wt_ref, h_ref, w1g_ref, w1l_ref, b1g_ref, b1l_ref, w2_ref, b2_ref,
                acc_ref, xg_ref, y_ref, pend_ref):
    ti = pl.program_id(0)
    e = pl.program_id(1)
    n_e = pl.num_programs(1)
    ch = xg_ref.shape[0] // SUBLANES
    off = ti * (n_e + 1)

    def tile_rows(r):
        return pl.ds(pl.multiple_of(r * SUBLANES, SUBLANES), SUBLANES)

    def static_rows(r):
        return slice(r * SUBLANES, (r + 1) * SUBLANES)

    def expert_mlp(xg):
        glu = jnp.minimum(_dot(xg, w1g_ref[0]) + b1g_ref[0], SWIGLU_LIMIT)
        lin = jnp.clip(_dot(xg, w1l_ref[0]) + b1l_ref[0], -SWIGLU_LIMIT, SWIGLU_LIMIT)
        act = glu * jax.nn.sigmoid(SWIGLU_ALPHA * glu) * (lin + 1.0)
        return _dot(act.astype(jnp.bfloat16), w2_ref[0]) + b2_ref[0]

    def gather_rows(dst_ref, base, n):
        def group(i, _):
            r0 = pl.multiple_of(i * ROW_UNROLL, ROW_UNROLL)
            for u in range(ROW_UNROLL):
                dst_ref[tile_rows(r0 + u), :] = h_ref[tile_rows(tok_ref[0, 0, base + r0 + u]), :]
            return 0

        lax.fori_loop(0, (n + ROW_UNROLL - 1) // ROW_UNROLL, group, 0)

    def scatter_rows(src_ref, base, n):
        def group(i, _):
            r0 = pl.multiple_of(i * ROW_UNROLL, ROW_UNROLL)
            toks = [tok_ref[0, 0, base + r0 + u] for u in range(ROW_UNROLL)]
            new = [acc_ref[tile_rows(toks[u]), :] + wt_ref[0, 0, base + r0 + u] * src_ref[tile_rows(r0 + u), :]
                   for u in range(ROW_UNROLL)]
            for u in range(ROW_UNROLL):
                acc_ref[tile_rows(toks[u]), :] = new[u]
            return 0

        full_groups = n // ROW_UNROLL
        lax.fori_loop(0, full_groups, group, 0)

        def row(r, _):
            tok = tok_ref[0, 0, base + r]
            acc_ref[tile_rows(tok), :] = acc_ref[tile_rows(tok), :] + wt_ref[0, 0, base + r] * src_ref[tile_rows(r), :]
            return 0

        lax.fori_loop(full_groups * ROW_UNROLL, n, row, 0)

    start = offs_ref[off + e]
    count = offs_ref[off + e + 1] - start
    n_blocks = jnp.maximum((count + ch - 1) // ch, 1)
    start_next = offs_ref[off + jnp.minimum(e + 1, n_e - 1)]

    @pl.when((ti == 0) & (e == 0))
    def _():
        for ref in (xg_ref, y_ref):
            ref[...] = jnp.zeros_like(ref)

    @pl.when(e == 0)
    def _():
        acc_ref[...] = jnp.zeros_like(acc_ref)
        pend_ref[0] = 0
        pend_ref[1] = 0
        gather_rows(xg_ref, start, jnp.minimum(count, ch))

    def block_body(rows, k, base_prev, n_prev):
        xg = _from_token_tiles(xg_ref, 0, rows).astype(jnp.bfloat16)

        base_next = jnp.where(k + 1 < n_blocks, start + (k + 1) * ch, start_next)
        for r in range(ch):
            xg_ref[static_rows(r), :] = h_ref[tile_rows(tok_ref[0, 0, base_next + r]), :]

        for g in range(ch // ROW_UNROLL):
            toks, new = [], []
            for u in range(ROW_UNROLL):
                r = g * ROW_UNROLL + u
                valid = r < n_prev
                tok = jnp.where(valid, tok_ref[0, 0, base_prev + r], tile + u)
                w = jnp.where(valid, wt_ref[0, 0, base_prev + r], 0.0)
                toks.append(tok)
                new.append(acc_ref[tile_rows(tok), :] + w * y_ref[static_rows(r), :])
            for u in range(ROW_UNROLL):
                acc_ref[tile_rows(toks[u]), :] = new[u]

        _to_token_tiles(y_ref, 0, rows, expert_mlp(xg))

    def block(k, pending):
        base_prev, n_prev = pending
        n = jnp.minimum(ch, count - k * ch)
        small = n <= ch // 2
        pl.when(small)(functools.partial(block_body, ch // 2, k, base_prev, n_prev))
        pl.when(jnp.logical_not(small))(functools.partial(block_body, ch, k, base_prev, n_prev))
        return start + k * ch, n

    base_prev, n_prev = lax.fori_loop(0, n_blocks, block, (pend_ref[0], pend_ref[1]))
    pend_ref[0] = base_prev
    pend_ref[1] = n_prev

    @pl.when(e == n_e - 1)
    def _():
        scatter_rows(y_ref, base_prev, n_prev)


def _moe(offs, tok_sorted, wt_sorted, h2_tiles, w1g, w1l, b1g, b1l, w2, b2, tile):
    n_tok = h2_tiles.shape[0] // SUBLANES
    n_e, d, d_e = w1g.shape
    n_t = n_tok // tile
    a = tok_sorted.shape[-1]
    assert d == SUBLANES * LANES and MOE_CHUNK % ROW_UNROLL == 0
    smem = lambda: pl.BlockSpec((1, 1, a), lambda ti, e, o: (ti, 0, 0), memory_space=pltpu.SMEM)
    ex = lambda shape: pl.BlockSpec((1,) + shape, lambda ti, e, o: (e, 0, 0))
    out_rows = (tile + ROW_UNROLL) * SUBLANES
    rows = pltpu.VMEM((MOE_CHUNK * SUBLANES, LANES), jnp.float32)
    once = pl.Buffered(1)
    grid_spec = pltpu.PrefetchScalarGridSpec(
        num_scalar_prefetch=1,
        grid=(n_t, n_e),
        in_specs=[smem(), smem(),
                  pl.BlockSpec((tile * SUBLANES, LANES), lambda ti, e, o: (ti, 0), pipeline_mode=once),
                  ex((d, d_e)), ex((d, d_e)), ex((1, d_e)), ex((1, d_e)), ex((d_e, d)), ex((1, d))],
        out_specs=pl.BlockSpec((out_rows, LANES), lambda ti, e, o: (ti, 0), pipeline_mode=once),
        scratch_shapes=[rows, rows, pltpu.SMEM((2,), jnp.int32)],
    )
    return pl.pallas_call(
        functools.partial(_moe_kernel, tile),
        grid_spec=grid_spec,
        out_shape=jax.ShapeDtypeStruct((n_t * out_rows, LANES), jnp.float32),
        compiler_params=pltpu.CompilerParams(dimension_semantics=("arbitrary", "arbitrary"),
                                             vmem_limit_bytes=MOE_VMEM_LIMIT),
        name="moe",
    )(offs, tok_sorted, wt_sorted, h2_tiles, w1g, w1l, b1g, b1l, w2, b2)


def _final_kernel(sub, x_ref, m_ref, g2_ref, g_ref, o_ref):
    t = x_ref.shape[1]
    row0 = lax.rem(pl.program_id(1), sub) * t
    x = x_ref[0] + g2_ref[0] * _from_token_tiles(m_ref, row0, t)
    ms = jnp.mean(x * x, axis=-1, keepdims=True)
    o_ref[0] = x * lax.rsqrt(ms + RMS_EPS) * g_ref[...]


def _final(x1, moe_tiles, g2, final_g, moe_tile):
    b, s, d = x1.shape
    t = min(FINAL_TILE, moe_tile)
    per = s // t
    sub = moe_tile // t
    moe_rows = (moe_tile + ROW_UNROLL) * SUBLANES
    tok = pl.BlockSpec((1, t, d), lambda bi, i: (bi, i, 0))
    return pl.pallas_call(
        functools.partial(_final_kernel, sub),
        grid=(b, per),
        in_specs=[tok, pl.BlockSpec((moe_rows, LANES), lambda bi, i: (bi * (per // sub) + i // sub, 0)),
                  pl.BlockSpec((1, 1, d), lambda bi, i: (bi, 0, 0)),
                  pl.BlockSpec((1, d), lambda bi, i: (0, 0))],
        out_specs=tok,
        out_shape=jax.ShapeDtypeStruct((b, s, d), jnp.float32),
        compiler_params=_params(("arbitrary", "arbitrary")),
        name="final_norm",
    )(x1, moe_tiles, g2, final_g)


def _rope_tables(s, n_ctx):
    inv = ROPE_THETA ** (-(jnp.arange(ROPE_QUARTER, dtype=jnp.float32) * 2.0 / ROPE_HALF))
    pos = jnp.arange(s)
    ang_r = (pos // GRID_W).astype(jnp.float32)[None, :] * inv[:, None]
    ang_c = (pos % GRID_W).astype(jnp.float32)[None, :] * inv[:, None]
    ang = jnp.concatenate([ang_r, ang_c], axis=0)
    cos = jnp.concatenate([jnp.cos(ang), jnp.ones((ROPE_HALF, n_ctx), jnp.float32)], axis=1)
    sin = jnp.concatenate([jnp.sin(ang), jnp.zeros((ROPE_HALF, n_ctx), jnp.float32)], axis=1)
    return cos, sin


def _split_bf16(w):
    hi = w.astype(jnp.bfloat16)
    return jnp.stack([hi, (w - hi.astype(jnp.float32)).astype(jnp.bfloat16)])


def _route_lists(top_idx, top_w, tile):
    b, k, s = top_idx.shape
    n_t = b * s // tile
    per = s // tile
    e_flat = top_idx.reshape(b, k, per, tile).transpose(0, 2, 1, 3).reshape(n_t, k * tile)
    w_flat = top_w.reshape(b, k, per, tile).transpose(0, 2, 1, 3).reshape(n_t, k * tile)
    tok = jnp.tile(jnp.arange(tile, dtype=jnp.int32), k)[None, :]
    key_sorted, w_sorted = lax.sort((e_flat * tile + tok, w_flat), dimension=1, num_keys=1)
    tok_sorted = lax.rem(key_sorted, tile)
    counts = jnp.sum(e_flat[:, :, None] == jnp.arange(N_EXPERTS, dtype=jnp.int32)[None, None, :], axis=1,
                     dtype=jnp.int32)
    offs = jnp.concatenate([jnp.zeros((n_t, 1), jnp.int32), jnp.cumsum(counts, axis=1, dtype=jnp.int32)], axis=1)
    pad = ((0, 0), (0, MOE_CHUNK))
    return offs.reshape(-1), jnp.pad(tok_sorted, pad)[:, None, :], jnp.pad(w_sorted, pad)[:, None, :]


def kernel(x, c, ctx, c_ctx, w_mod, b_mod, norm1_g, norm2_g, w_in, q_norm_g, k_norm_g, sink, w_br_a, w_br_b,
           w_o, w_router, b_router, w_e1, b_e1, w_e2, b_e2, final_g):
    b, s, d = x.shape
    n_ctx = ctx.shape[1]
    assert w_mod.shape[0] == 1
    bf = jnp.bfloat16
    t = TOKEN_TILE
    cos, sin = _rope_tables(s, IN_PROJ_SUBS * t)
    moe_tile = min(MOE_TILE, s)
    for l in range(1):
        rows = ((b + 1 + 7) // 8) * 8
        cond = jnp.zeros((rows, d), jnp.float32).at[:b].set(c).at[b].set(c_ctx)
        mods = _adaln(cond, w_mod[l], b_mod[l])
        sh1, sc1, g1, sh2, sc2, g2 = jnp.split(mods, 6, axis=-1)
        mod1 = jnp.stack([jnp.stack([sh1[:b], sc1[:b]], axis=1),
                          jnp.broadcast_to(jnp.stack([sh1[b], sc1[b]], axis=0)[None], (b, 2, d))], axis=1)
        mod2 = jnp.stack([sh2[:b], sc2[:b]], axis=1)

        w = w_in[l]
        kv = w[:, :4 * KV_W]
        wk_t = jnp.concatenate([kv[:, 0:KV_W], kv[:, 2 * KV_W:3 * KV_W]], axis=1).T.astype(bf)
        wv_t = jnp.concatenate([kv[:, KV_W:2 * KV_W], kv[:, 3 * KV_W:4 * KV_W]], axis=1).T.astype(bf)
        wq_t = w[:, 4 * KV_W:4 * KV_W + Q_A + Q_B].T.astype(bf)
        wg_t = w[:, 4 * KV_W + Q_A + Q_B:].T.astype(bf)
        qg = jnp.broadcast_to((q_norm_g[l] * (ATTN_SCALE * LOG2E))[:, None], (HEAD_DIM, t))
        kg = jnp.broadcast_to(k_norm_g[l][:, None], (HEAD_DIM, t))

        qa_t, qb_t, ka, kb, va_t, vb_t, gate_t = _in_proj(
            x, ctx, mod1, norm1_g[l][None], wq_t, wk_t, wv_t, wg_t, qg, kg, cos, sin)

        n_key_tiles = (s + n_ctx) // t
        ya_t = _attention_global(128, True, n_key_tiles, qa_t, ka, va_t)
        yb_t = _attention_window(n_key_tiles, sink[l], qb_t, kb, vb_t)

        x1, h2, top_idx, top_w = _merge(
            x, ya_t, yb_t, gate_t, w_br_a[l].T.astype(bf), w_br_b[l].T.astype(bf), w_o[l].astype(bf),
            g1[:b, None, :], mod2, norm2_g[l][None], _split_bf16(w_router[l].T), b_router[l][:, None])

        offs, tok_sorted, w_sorted = _route_lists(top_idx, top_w, moe_tile)
        w1g, w1l = _w1_split(w_e1[l])
        moe_tiles = _moe(offs, tok_sorted, w_sorted, h2, w1g, w1l,
                         b_e1[l][:, None, 0::2], b_e1[l][:, None, 1::2],
                         w_e2[l].astype(bf), b_e2[l][:, None, :], moe_tile)
        out = _final(x1, moe_tiles, g2[:b, None, :], final_g[None], moe_tile)
    return out
```

```python
import functools

import numpy as np
import jax
import jax.numpy as jnp
from jax import lax
from jax.experimental import pallas as pl
from jax.experimental.pallas import tpu as pltpu

HEAD_DIM = 64
KV_HEADS = 2
A_HEADS = 8
B_HEADS = 8
GROUP = A_HEADS // KV_HEADS
GRID_W = 64
WINDOW = 128
ROPE_HALF = HEAD_DIM // 2
ROPE_QUARTER = ROPE_HALF // 2
ROPE_THETA = 10000.0
N_EXPERTS = 32
TOP_K = 4
SWIGLU_ALPHA = 1.702
SWIGLU_LIMIT = 7.0
RMS_EPS = 1e-6
ATTN_SCALE = HEAD_DIM ** -0.5
LOG2E = 1.4426950408889634
NEG_INF = -1e30
KV_W = KV_HEADS * HEAD_DIM
Q_A = A_HEADS * HEAD_DIM
Q_B = B_HEADS * HEAD_DIM

SUBLANES = 8
LANES = 128
BF16_ROWS = 16
V7X_VMEM_BYTES = 64 * 1024 * 1024
TOKEN_TILE = 256
ADALN_COLS = 1024
IN_PROJ_SUBS = 4
GATE_ROWS = 512
ATTN_GLOBAL_TQ = 128
SPLIT_ROWS = 512
MERGE_TILE = 1024
FINAL_TILE = 512
MOE_TILE = 4096
MOE_CHUNK = 256
ROW_UNROLL = 8
SPLIT_BLOCK = 256
V_ROWS = HEAD_DIM + BF16_ROWS
VMEM_LIMIT = V7X_VMEM_BYTES - 8 * 1024 * 1024
MOE_VMEM_LIMIT = V7X_VMEM_BYTES - 4 * 1024 * 1024

_NT = (((1,), (1,)), ((), ()))


def _dot_nt(a, b, precision=None):
    return lax.dot_general(a, b, _NT, preferred_element_type=jnp.float32, precision=precision)


def _dot(a, b, precision=None):
    return jnp.dot(a, b, preferred_element_type=jnp.float32, precision=precision)


def _params(sem):
    return pltpu.CompilerParams(dimension_semantics=sem, vmem_limit_bytes=VMEM_LIMIT)


def _adaln_kernel(cond_ref, w_ref, b_ref, o_ref):
    cond = cond_ref[...]
    act = cond * jax.nn.sigmoid(cond)
    o_ref[...] = _dot(act, w_ref[...], precision=lax.Precision.HIGHEST) + b_ref[...]


def _adaln(cond, w, b):
    rows, d = cond.shape
    n = w.shape[1]
    tn = ADALN_COLS
    return pl.pallas_call(
        _adaln_kernel,
        grid=(n // tn,),
        in_specs=[pl.BlockSpec((rows, d), lambda j: (0, 0)),
                  pl.BlockSpec((d, tn), lambda j: (0, j)),
                  pl.BlockSpec((1, tn), lambda j: (0, j))],
        out_specs=pl.BlockSpec((rows, tn), lambda j: (0, j)),
        out_shape=jax.ShapeDtypeStruct((rows, n), jnp.float32),
        compiler_params=_params(("arbitrary",)),
        name="adaln",
    )(cond, w, b.reshape(1, n))


def _rope_t(xh, cos, sin):
    q = ROPE_QUARTER
    a0, a1, b0, b1 = xh[0:q], xh[q:2 * q], xh[2 * q:3 * q], xh[3 * q:4 * q]
    cr, cc = cos[0:q], cos[q:2 * q]
    sr, sc = sin[0:q], sin[q:2 * q]
    return jnp.concatenate([a0 * cr - a1 * sr, a1 * cr + a0 * sr,
                            b0 * cc - b1 * sc, b1 * cc + b0 * sc], axis=0)


def _head_norm_t(xh, g):
    ms = jnp.mean(xh * xh, axis=0, keepdims=True)
    return xh * lax.rsqrt(ms + RMS_EPS) * g


def _mod_norm(x, g, shift, scale):
    ms = jnp.mean(x * x, axis=-1, keepdims=True)
    return x * lax.rsqrt(ms + RMS_EPS) * g * (1.0 + scale) + shift


def _in_proj_kernel(n_lat, x_ref, ctx_ref, mod_ref, g_ref, wq_ref, wk_ref, wv_ref, wg_ref,
                    qg_ref, kg_ref, cos_ref, sin_ref,
                    qa_ref, qb_ref, ka_ref, kb_ref, va_ref, vb_ref, gate_ref, h_ref):
    i = pl.program_id(1)
    t = TOKEN_TILE
    shift = mod_ref[0, 0, 0:1, :]
    scale = mod_ref[0, 0, 1:2, :]
    is_lat = i < n_lat

    for sub in range(IN_PROJ_SUBS):
        @pl.when(is_lat)
        def _(sub=sub):
            h_ref[sub] = _mod_norm(x_ref[0, sub * t:(sub + 1) * t, :], g_ref[...], shift, scale).astype(jnp.bfloat16)

        @pl.when(jnp.logical_not(is_lat))
        def _(sub=sub):
            if sub == 0:
                h_ref[sub] = _mod_norm(ctx_ref[0], g_ref[...], shift, scale).astype(jnp.bfloat16)
            else:
                h_ref[sub] = jnp.zeros(h_ref.shape[1:], jnp.bfloat16)

    kg = kg_ref[...]
    ones = jnp.ones((V_ROWS - HEAD_DIM, t), jnp.float32)
    for sub in range(IN_PROJ_SUBS):
        h = h_ref[sub]
        cos = cos_ref[:, sub * t:(sub + 1) * t]
        sin = sin_ref[:, sub * t:(sub + 1) * t]
        kt = _dot_nt(wk_ref[...], h)
        ka = [_rope_t(_head_norm_t(kt[j * HEAD_DIM:(j + 1) * HEAD_DIM], kg), cos, sin) for j in range(KV_HEADS)]
        kb = [_rope_t(kt[KV_W + j * HEAD_DIM:KV_W + (j + 1) * HEAD_DIM], cos, sin) for j in range(KV_HEADS)]
        ka_ref[0, sub] = jnp.concatenate(ka, axis=0).T.astype(jnp.bfloat16)
        kb_ref[0, sub] = jnp.concatenate(kb, axis=0).T.astype(jnp.bfloat16)
        vt = _dot_nt(wv_ref[...], h)
        for br, v_ref in enumerate((va_ref, vb_ref)):
            rows = []
            for j in range(KV_HEADS):
                rows += [vt[br * KV_W + j * HEAD_DIM:br * KV_W + (j + 1) * HEAD_DIM], ones]
            v_ref[0, sub] = jnp.concatenate(rows, axis=0).astype(jnp.bfloat16)

    @pl.when(is_lat)
    def _():
        qg = qg_ref[...]
        for sub in range(IN_PROJ_SUBS):
            h = h_ref[sub]
            tok = slice(sub * t, (sub + 1) * t)
            cos = cos_ref[:, tok]
            sin = sin_ref[:, tok]
            for half in range(2):
                qt = _dot_nt(wq_ref[half * Q_A:(half + 1) * Q_A, :], h)
                for hd in range(A_HEADS):
                    xh = qt[hd * HEAD_DIM:(hd + 1) * HEAD_DIM]
                    if half == 0:
                        out = _rope_t(_head_norm_t(xh, qg), cos, sin)
                        qa_ref[0, hd * HEAD_DIM:(hd + 1) * HEAD_DIM, tok] = out.astype(jnp.bfloat16)
                    else:
                        out = _rope_t(xh * (ATTN_SCALE * LOG2E), cos, sin)
                        qb_ref[0, hd * HEAD_DIM:(hd + 1) * HEAD_DIM, tok] = out.astype(jnp.bfloat16)
            rows = GATE_ROWS
            for c in range(wg_ref.shape[0] // rows):
                gt = _dot_nt(wg_ref[c * rows:(c + 1) * rows, :], h)
                gate_ref[0, c * rows:(c + 1) * rows, tok] = jax.nn.sigmoid(gt).astype(jnp.bfloat16)


def _in_proj(x, ctx, mod, norm_g, wq_t, wk_t, wv_t, wg_t, qg, kg, cos, sin):
    b, s, d = x.shape
    t = TOKEN_TILE
    subs = IN_PROJ_SUBS
    st = subs * t
    assert ctx.shape[1] == t and s % st == 0
    n_lat = s // st
    n_tiles = (n_lat + 1) * subs
    last = n_lat - 1
    full = lambda shape: pl.BlockSpec(shape, lambda bi, i: (0,) * len(shape))
    q_spec = pl.BlockSpec((1, Q_A, st), lambda bi, i: (bi, 0, jnp.minimum(i, last)))
    k_spec = pl.BlockSpec((1, subs, t, KV_W), lambda bi, i: (bi, i, 0, 0))
    v_spec = pl.BlockSpec((1, subs, KV_HEADS * V_ROWS, t), lambda bi, i: (bi, i, 0, 0))
    bf = jnp.bfloat16
    return pl.pallas_call(
        functools.partial(_in_proj_kernel, n_lat),
        grid=(b, n_lat + 1),
        in_specs=[pl.BlockSpec((1, st, d), lambda bi, i: (bi, jnp.minimum(i, last), 0)),
                  pl.BlockSpec((1, t, d), lambda bi, i: (bi, 0, 0)),
                  pl.BlockSpec((1, 1, 2, d), lambda bi, i: (bi, i // n_lat, 0, 0)),
                  full((1, d)), full(wq_t.shape), full(wk_t.shape), full(wv_t.shape), full(wg_t.shape),
                  full((HEAD_DIM, t)), full((HEAD_DIM, t)),
                  pl.BlockSpec((ROPE_HALF, st), lambda bi, i: (0, i)),
                  pl.BlockSpec((ROPE_HALF, st), lambda bi, i: (0, i))],
        out_specs=[q_spec, q_spec, k_spec, k_spec, v_spec, v_spec,
                   pl.BlockSpec((1, 2 * d, st), lambda bi, i: (bi, 0, jnp.minimum(i, last)))],
        out_shape=[jax.ShapeDtypeStruct((b, Q_A, s), bf), jax.ShapeDtypeStruct((b, Q_B, s), bf),
                   jax.ShapeDtypeStruct((b, n_tiles, t, KV_W), bf), jax.ShapeDtypeStruct((b, n_tiles, t, KV_W), bf),
                   jax.ShapeDtypeStruct((b, n_tiles, KV_HEADS * V_ROWS, t), bf),
                   jax.ShapeDtypeStruct((b, n_tiles, KV_HEADS * V_ROWS, t), bf),
                   jax.ShapeDtypeStruct((b, 2 * d, s), bf)],
        scratch_shapes=[pltpu.VMEM((subs, t, d), bf)],
        compiler_params=_params(("arbitrary", "arbitrary")),
        name="in_proj",
    )(x, ctx, mod, norm_g, wq_t, wk_t, wv_t, wg_t, qg, kg, cos, sin)


def _attn_window_kernel(subs, n_lat, n_tiles, sink_ref, q_ref, k_ref, v_ref, y_ref, s_ref):
    t = tq = TOKEN_TILE
    n = GROUP * tq
    assert WINDOW * 2 == t
    heads = [[kvh * GROUP + g for g in range(GROUP)] for kvh in range(KV_HEADS)]
    n_slots = s_ref.shape[1] // subs
    sink = [jnp.concatenate([jnp.full((1, tq), sink_ref[hd] * LOG2E, jnp.float32) for hd in heads[kvh]], axis=1)
            for kvh in range(KV_HEADS)]
    c_half = lax.rem(lax.broadcasted_iota(jnp.int32, (WINDOW, n), 1), tq)
    r_half = lax.broadcasted_iota(jnp.int32, (WINDOW, n), 0)
    c_full = lax.rem(lax.broadcasted_iota(jnp.int32, (t, n), 1), tq)
    r_full = lax.broadcasted_iota(jnp.int32, (t, n), 0)
    acc_row = lax.broadcasted_iota(jnp.int32, (V_ROWS, n), 0)

    for sub in range(subs):
        qi = pl.program_id(1) * subs + sub
        cols = slice(sub * tq, (sub + 1) * tq)
        qpad = []
        for kvh in range(KV_HEADS):
            qcat = jnp.concatenate([q_ref[0, hd * HEAD_DIM:(hd + 1) * HEAD_DIM, cols] for hd in heads[kvh]], axis=1)
            zeros = jnp.zeros_like(qcat)
            qpad.append(jnp.concatenate([qcat, zeros] if kvh == 0 else [zeros, qcat], axis=0))

        off_prev = jnp.where(qi >= 1, 0, 2 * t)
        off_next = jnp.where(qi + 1 < n_lat, 0, 2 * t)
        pieces = [
            (jnp.maximum(qi - 1, 0), WINDOW, WINDOW, r_half >= c_half + off_prev),
            (qi, 0, t, jnp.abs(r_full - c_full) <= WINDOW),
            (jnp.minimum(qi + 1, n_lat - 1), 0, WINDOW, c_half - r_half >= WINDOW + off_next),
        ] + [(kb, 0, t, None) for kb in range(n_lat, n_tiles)]

        m8 = [jnp.broadcast_to(sink[kvh], (8, n)) for kvh in range(KV_HEADS)]
        for slot, (kb, lo, rows, ok) in enumerate(pieces):
            kblk = k_ref[0, kb, lo:lo + rows, :]
            for kvh in range(KV_HEADS):
                s = _dot(kblk, qpad[kvh])
                if ok is not None:
                    s = jnp.where(ok, s, NEG_INF)
                s_ref[kvh, sub * n_slots + slot, 0:rows, :] = s
                m8[kvh] = jnp.maximum(m8[kvh], jnp.max(s.reshape(rows // 8, 8, n), axis=0))
        m = [jnp.max(m8[kvh], axis=0, keepdims=True) for kvh in range(KV_HEADS)]

        accs = [jnp.where(acc_row == HEAD_DIM, jnp.exp2(sink[kvh] - m[kvh]), 0.0) for kvh in range(KV_HEADS)]
        for slot, (kb, lo, rows, _) in enumerate(pieces):
            for kvh in range(KV_HEADS):
                p = jnp.exp2(s_ref[kvh, sub * n_slots + slot, 0:rows, :] - m[kvh]).astype(jnp.bfloat16)
                vblk = v_ref[0, kb, kvh * V_ROWS:(kvh + 1) * V_ROWS, lo:lo + rows]
                accs[kvh] = accs[kvh] + _dot(vblk, p)
        for kvh in range(KV_HEADS):
            out = accs[kvh][0:HEAD_DIM] / accs[kvh][HEAD_DIM:HEAD_DIM + 1]
            for g, hd in enumerate(heads[kvh]):
                y_ref[0, hd * HEAD_DIM:(hd + 1) * HEAD_DIM, cols] = out[:, g * tq:(g + 1) * tq].astype(jnp.bfloat16)


def _attn_global_kernel(tq, n_tiles, n_q, q_ref, k_ref, v_ref, y_ref, s0_ref, s1_ref, m0_ref, m1_ref):
    unroll = True
    j = pl.program_id(0)
    t = TOKEN_TILE
    n = GROUP * tq
    heads = [[kvh * GROUP + g for g in range(GROUP)] for kvh in range(KV_HEADS)]
    bufs = ((s0_ref, m0_ref), (s1_ref, m1_ref))

    def padded_q():
        qpad = []
        for kvh in range(KV_HEADS):
            qcat = jnp.concatenate([q_ref[0, hd * HEAD_DIM:(hd + 1) * HEAD_DIM, :] for hd in heads[kvh]], axis=1)
            zeros = jnp.zeros_like(qcat)
            qpad.append(jnp.concatenate([qcat, zeros] if kvh == 0 else [zeros, qcat], axis=0))
        return qpad

    def scores(s_ref, kb, m8, qpad):
        kblk = k_ref[0, kb]
        out = []
        for kvh in range(KV_HEADS):
            s = _dot(kblk, qpad[kvh])
            s_ref[kvh, kb] = s
            out.append(jnp.maximum(m8[kvh], jnp.max(s.reshape(t // 8, 8, n), axis=0)))
        return tuple(out)

    def weigh(s_ref, kb, m, accs):
        out = []
        for kvh in range(KV_HEADS):
            p = jnp.exp2(s_ref[kvh, kb] - m[kvh]).astype(jnp.bfloat16)
            vblk = v_ref[0, kb, kvh * V_ROWS:(kvh + 1) * V_ROWS, :]
            out.append(accs[kvh] + _dot(vblk, p))
        return tuple(out)

    def keep_maxima(m_ref, m8):
        for kvh in range(KV_HEADS):
            m_ref[kvh] = m8[kvh]

    def maxima(m_ref):
        return [jnp.max(m_ref[kvh], axis=0, keepdims=True) for kvh in range(KV_HEADS)]

    def write(accs):
        for kvh in range(KV_HEADS):
            out = accs[kvh][0:HEAD_DIM] / accs[kvh][HEAD_DIM:HEAD_DIM + 1]
            for g, hd in enumerate(heads[kvh]):
                y_ref[0, hd * HEAD_DIM:(hd + 1) * HEAD_DIM, :] = out[:, g * tq:(g + 1) * tq].astype(jnp.bfloat16)

    m8_0 = tuple(jnp.full((8, n), -jnp.inf, jnp.float32) for _ in range(KV_HEADS))
    acc_0 = tuple(jnp.zeros((V_ROWS, n), jnp.float32) for _ in range(KV_HEADS))

    @pl.when(j == 0)
    def _():
        s_w, m_w = bufs[0]
        qpad = padded_q()
        keep_maxima(m_w, lax.fori_loop(0, n_tiles, lambda kb, c: scores(s_w, kb, c, qpad), m8_0, unroll=unroll))

    for parity in range(2):
        (s_w, m_w), (s_r, m_r) = bufs[parity], bufs[1 - parity]

        @pl.when((j > 0) & (j < n_q) & (lax.rem(j, 2) == parity))
        def _():
            qpad = padded_q()
            m = maxima(m_r)
            m8, accs = lax.fori_loop(
                0, n_tiles, lambda kb, c: (scores(s_w, kb, c[0], qpad), weigh(s_r, kb, m, c[1])),
                (m8_0, acc_0), unroll=unroll)
            keep_maxima(m_w, m8)
            write(accs)

    @pl.when(j == n_q)
    def _():
        s_r, m_r = bufs[(n_q - 1) % 2]
        m = maxima(m_r)
        write(lax.fori_loop(0, n_tiles, lambda kb, c: weigh(s_r, kb, m, c), acc_0, unroll=unroll))


def _attention_global(tq, n_tiles, q_t, k, v_t):
    b, hq, s = q_t.shape
    per = s // tq
    n_q = b * per
    n = GROUP * tq
    scored = lambda j: jnp.minimum(j, n_q - 1)
    weighed = lambda j: jnp.maximum(j - 1, 0)
    return pl.pallas_call(
        functools.partial(_attn_global_kernel, tq, n_tiles, n_q),
        grid=(n_q + 1,),
        in_specs=[pl.BlockSpec((1, hq, tq), lambda j: (scored(j) // per, 0, scored(j) % per)),
                  pl.BlockSpec((1,) + k.shape[1:], lambda j: (scored(j) // per, 0, 0, 0)),
                  pl.BlockSpec((1,) + v_t.shape[1:], lambda j: (weighed(j) // per, 0, 0, 0))],
        out_specs=pl.BlockSpec((1, hq, tq), lambda j: (weighed(j) // per, 0, weighed(j) % per)),
        out_shape=jax.ShapeDtypeStruct((b, hq, s), jnp.bfloat16),
        scratch_shapes=[pltpu.VMEM((KV_HEADS, n_tiles, TOKEN_TILE, n), jnp.float32)] * 2
        + [pltpu.VMEM((KV_HEADS, 8, n), jnp.float32)] * 2,
        compiler_params=_params(("arbitrary",)),
        name="attn_global",
    )(q_t, k, v_t)


def _attention_window(n_tiles, sink, q_t, k, v_t):
    b, hq, s = q_t.shape
    t = TOKEN_TILE
    n_lat = s // t
    subs = 4 if n_lat % 4 == 0 else (2 if n_lat % 2 == 0 else 1)
    slots = 3 + (n_tiles - n_lat)
    grid_spec = pltpu.PrefetchScalarGridSpec(
        num_scalar_prefetch=1,
        grid=(b, n_lat // subs),
        in_specs=[pl.BlockSpec((1, hq, subs * t), lambda bi, qi, sk: (bi, 0, qi)),
                  pl.BlockSpec((1,) + k.shape[1:], lambda bi, qi, sk: (bi, 0, 0, 0)),
                  pl.BlockSpec((1,) + v_t.shape[1:], lambda bi, qi, sk: (bi, 0, 0, 0))],
        out_specs=pl.BlockSpec((1, hq, subs * t), lambda bi, qi, sk: (bi, 0, qi)),
        scratch_shapes=[pltpu.VMEM((KV_HEADS, subs * slots, t, GROUP * t), jnp.float32)],
    )
    return pl.pallas_call(
        functools.partial(_attn_window_kernel, subs, n_lat, n_tiles),
        grid_spec=grid_spec,
        out_shape=jax.ShapeDtypeStruct((b, hq, s), jnp.bfloat16),
        compiler_params=_params(("arbitrary", "arbitrary")),
        name="attn_window",
    )(sink, q_t, k, v_t)


def _to_token_tiles(ref, row0, rows, val):
    for j in range(SUBLANES):
        ref[pl.ds(row0 * SUBLANES + j, rows, stride=SUBLANES), :] = val[:, j * LANES:(j + 1) * LANES]


def _from_token_tiles(ref, row0, rows):
    return jnp.concatenate([ref[pl.ds(row0 * SUBLANES + j, rows, stride=SUBLANES), :] for j in range(SUBLANES)],
                           axis=1)


def _merge_kernel(x_ref, ya_ref, yb_ref, gate_ref, wa_ref, wb_ref, wo_ref, g1_ref, mod_ref, g2_ref,
                  wr_ref, br_ref, x1_ref, h2_ref, idx_ref, wt_ref):
    d = x_ref.shape[-1]
    t = TOKEN_TILE
    for sub in range(x_ref.shape[1] // t):
        tok = slice(sub * t, (sub + 1) * t)
        za = _dot(wa_ref[...], ya_ref[0, :, tok])
        zb = _dot(wb_ref[...], yb_ref[0, :, tok])
        mt = (gate_ref[0, 0:d, tok].astype(jnp.float32) * za
              + gate_ref[0, d:2 * d, tok].astype(jnp.float32) * zb)
        o = _dot(mt.T.astype(jnp.bfloat16), wo_ref[...])
        x1 = x_ref[0, tok, :] + g1_ref[0] * o
        x1_ref[0, tok, :] = x1
        h2 = _mod_norm(x1, g2_ref[...], mod_ref[0, 0:1, :], mod_ref[0, 1:2, :])
        _to_token_tiles(h2_ref, sub * t, t, h2)
        h_hi = h2.astype(jnp.bfloat16)
        h_lo = (h2 - h_hi.astype(jnp.float32)).astype(jnp.bfloat16)
        logits = (_dot_nt(wr_ref[0], h_hi) + _dot_nt(wr_ref[0], h_lo) + _dot_nt(wr_ref[1], h_hi)
                  + br_ref[...])
        n_e = logits.shape[0]
        row = lax.broadcasted_iota(jnp.int32, logits.shape, 0)
        work = logits
        top_v, top_i = [], []
        for _ in range(TOP_K):
            mk = jnp.max(work, axis=0, keepdims=True)
            ik = jnp.min(jnp.where(work == mk, row, n_e), axis=0, keepdims=True)
            top_v.append(mk)
            top_i.append(ik)
            work = jnp.where(row == ik, -jnp.inf, work)
        ex = [jnp.exp(v - top_v[0]) for v in top_v]
        den = ex[0] + ex[1] + ex[2] + ex[3]
        idx_ref[0, :, tok] = jnp.concatenate(top_i, axis=0)
        wt_ref[0, :, tok] = jnp.concatenate([e / den for e in ex], axis=0)


def _merge(x, ya_t, yb_t, gate_t, wa_t, wb_t, wo, g1, mod2, norm2_g, wr_t, br):
    b, s, d = x.shape
    t = min(MERGE_TILE, s)
    per = s // t
    assert d == SUBLANES * LANES
    full = lambda shape: pl.BlockSpec(shape, lambda bi, i: (0,) * len(shape))
    tok = pl.BlockSpec((1, t, d), lambda bi, i: (bi, i, 0))
    col = lambda rows: pl.BlockSpec((1, rows, t), lambda bi, i: (bi, 0, i))
    return pl.pallas_call(
        _merge_kernel,
        grid=(b, per),
        in_specs=[tok, col(Q_A), col(Q_B), col(2 * d),
                  full(wa_t.shape), full(wb_t.shape), full(wo.shape),
                  pl.BlockSpec((1, 1, d), lambda bi, i: (bi, 0, 0)),
                  pl.BlockSpec((1, 2, d), lambda bi, i: (bi, 0, 0)),
                  full((1, d)), full(wr_t.shape), full(br.shape)],
        out_specs=[tok, pl.BlockSpec((t * SUBLANES, LANES), lambda bi, i: (bi * per + i, 0)),
                   col(TOP_K), col(TOP_K)],
        out_shape=[jax.ShapeDtypeStruct((b, s, d), jnp.float32),
                   jax.ShapeDtypeStruct((b * s * SUBLANES, LANES), jnp.float32),
                   jax.ShapeDtypeStruct((b, TOP_K, s), jnp.int32), jax.ShapeDtypeStruct((b, TOP_K, s), jnp.float32)],
        compiler_params=_params(("arbitrary", "arbitrary")),
        name="merge_router",
    )(x, ya_t, yb_t, gate_t, wa_t, wb_t, wo, g1, mod2, norm2_g, wr_t, br)


def _w1_split_kernel(w_ref, p_ref, g_ref, l_ref):
    half = SPLIT_BLOCK // 2
    for jb in range(w_ref.shape[-1] // SPLIT_BLOCK):
        blk = w_ref[0, :, jb * SPLIT_BLOCK:(jb + 1) * SPLIT_BLOCK].astype(jnp.bfloat16)
        r = _dot(blk, p_ref[...])
        g_ref[0, :, jb * half:(jb + 1) * half] = r[:, :half].astype(jnp.bfloat16)
        l_ref[0, :, jb * half:(jb + 1) * half] = r[:, half:].astype(jnp.bfloat16)


def _w1_split(w1):
    n_e, d, d2 = w1.shape
    rows = SPLIT_ROWS
    half = SPLIT_BLOCK // 2
    perm = np.zeros((SPLIT_BLOCK, SPLIT_BLOCK), np.float32)
    for j in range(half):
        perm[2 * j, j] = 1.0
        perm[2 * j + 1, half + j] = 1.0
    out = pl.BlockSpec((1, rows, d2 // 2), lambda e, r: (e, r, 0))
    return pl.pallas_call(
        _w1_split_kernel,
        grid=(n_e, d // rows),
        in_specs=[pl.BlockSpec((1, rows, d2), lambda e, r: (e, r, 0)),
                  pl.BlockSpec((SPLIT_BLOCK, SPLIT_BLOCK), lambda e, r: (0, 0))],
        out_specs=[out, out],
        out_shape=[jax.ShapeDtypeStruct((n_e, d, d2 // 2), jnp.bfloat16)] * 2,
        compiler_params=_params(("arbitrary", "arbitrary")),
        name="w1_split",
    )(w1, jnp.asarray(perm, jnp.bfloat16))


def _moe_kernel(tile, offs_ref, tok_ref, wt_ref, h_ref, w1g_ref, w1l_ref, b1g_ref, b1l_ref, w2_ref, b2_ref,
                acc_ref, xg_ref, y_ref, pend_ref):
    ti = pl.program_id(0)
    e = pl.program_id(1)
    n_e = pl.num_programs(1)
    ch = xg_ref.shape[0] // SUBLANES
    off = ti * (n_e + 1)

    def tile_rows(r):
        return pl.ds(pl.multiple_of(r * SUBLANES, SUBLANES), SUBLANES)

    def static_rows(r):
        return slice(r * SUBLANES, (r + 1) * SUBLANES)

    def expert_mlp(xg):
        glu = jnp.minimum(_dot(xg, w1g_ref[0]) + b1g_ref[0], SWIGLU_LIMIT)
        lin = jnp.clip(_dot(xg, w1l_ref[0]) + b1l_ref[0], -SWIGLU_LIMIT, SWIGLU_LIMIT)
        act = glu * jax.nn.sigmoid(SWIGLU_ALPHA * glu) * (lin + 1.0)
        return _dot(act.astype(jnp.bfloat16), w2_ref[0]) + b2_ref[0]

    def gather_rows(dst_ref, base, n):
        def group(i, _):
            r0 = pl.multiple_of(i * ROW_UNROLL, ROW_UNROLL)
            for u in range(ROW_UNROLL):
                dst_ref[tile_rows(r0 + u), :] = h_ref[tile_rows(tok_ref[0, 0, base + r0 + u]), :]
            return 0

        lax.fori_loop(0, (n + ROW_UNROLL - 1) // ROW_UNROLL, group, 0)

    def scatter_rows(src_ref, base, n):
        def group(i, _):
            r0 = pl.multiple_of(i * ROW_UNROLL, ROW_UNROLL)
            toks = [tok_ref[0, 0, base + r0 + u] for u in range(ROW_UNROLL)]
            new = [acc_ref[tile_rows(toks[u]), :] + wt_ref[0, 0, base + r0 + u] * src_ref[tile_rows(r0 + u), :]
                   for u in range(ROW_UNROLL)]
            for u in range(ROW_UNROLL):
                acc_ref[tile_rows(toks[u]), :] = new[u]
            return 0

        full_groups = n // ROW_UNROLL
        lax.fori_loop(0, full_groups, group, 0)

        def row(r, _):
            tok = tok_ref[0, 0, base + r]
            acc_ref[tile_rows(tok), :] = acc_ref[tile_rows(tok), :] + wt_ref[0, 0, base + r] * src_ref[tile_rows(r), :]
            return 0

        lax.fori_loop(full_groups * ROW_UNROLL, n, row, 0)

    start = offs_ref[off + e]
    count = offs_ref[off + e + 1] - start
    n_blocks = jnp.maximum((count + ch - 1) // ch, 1)
    start_next = offs_ref[off + jnp.minimum(e + 1, n_e - 1)]

    @pl.when((ti == 0) & (e == 0))
    def _():
        for ref in (xg_ref, y_ref):
            ref[...] = jnp.zeros_like(ref)

    @pl.when(e == 0)
    def _():
        acc_ref[...] = jnp.zeros_like(acc_ref)
        pend_ref[0] = 0
        pend_ref[1] = 0
        gather_rows(xg_ref, start, jnp.minimum(count, ch))

    def block_body(rows, k, base_prev, n_prev):
        xg = _from_token_tiles(xg_ref, 0, rows).astype(jnp.bfloat16)

        base_next = jnp.where(k + 1 < n_blocks, start + (k + 1) * ch, start_next)
        for r in range(ch):
            xg_ref[static_rows(r), :] = h_ref[tile_rows(tok_ref[0, 0, base_next + r]), :]

        for g in range(ch // ROW_UNROLL):
            toks, new = [], []
            for u in range(ROW_UNROLL):
                r = g * ROW_UNROLL + u
                valid = r < n_prev
                tok = jnp.where(valid, tok_ref[0, 0, base_prev + r], tile + u)
                w = jnp.where(valid, wt_ref[0, 0, base_prev + r], 0.0)
                toks.append(tok)
                new.append(acc_ref[tile_rows(tok), :] + w * y_ref[static_rows(r), :])
            for u in range(ROW_UNROLL):
                acc_ref[tile_rows(toks[u]), :] = new[u]

        _to_token_tiles(y_ref, 0, rows, expert_mlp(xg))

    def block(k, pending):
        base_prev, n_prev = pending
        n = jnp.minimum(ch, count - k * ch)
        small = n <= ch // 2
        pl.when(small)(functools.partial(block_body, ch // 2, k, base_prev, n_prev))
        pl.when(jnp.logical_not(small))(functools.partial(block_body, ch, k, base_prev, n_prev))
        return start + k * ch, n

    base_prev, n_prev = lax.fori_loop(0, n_blocks, block, (pend_ref[0], pend_ref[1]))
    pend_ref[0] = base_prev
    pend_ref[1] = n_prev

    @pl.when(e == n_e - 1)
    def _():
        scatter_rows(y_ref, base_prev, n_prev)


def _moe(offs, tok_sorted, wt_sorted, h2_tiles, w1g, w1l, b1g, b1l, w2, b2, tile):
    n_tok = h2_tiles.shape[0] // SUBLANES
    n_e, d, d_e = w1g.shape
    n_t = n_tok // tile
    a = tok_sorted.shape[-1]
    assert d == SUBLANES * LANES and MOE_CHUNK % ROW_UNROLL == 0
    smem = lambda: pl.BlockSpec((1, 1, a), lambda ti, e, o: (ti, 0, 0), memory_space=pltpu.SMEM)
    ex = lambda shape: pl.BlockSpec((1,) + shape, lambda ti, e, o: (e, 0, 0))
    out_rows = (tile + ROW_UNROLL) * SUBLANES
    rows = pltpu.VMEM((MOE_CHUNK * SUBLANES, LANES), jnp.float32)
    once = pl.Buffered(1)
    grid_spec = pltpu.PrefetchScalarGridSpec(
        num_scalar_prefetch=1,
        grid=(n_t, n_e),
        in_specs=[smem(), smem(),
                  pl.BlockSpec((tile * SUBLANES, LANES), lambda ti, e, o: (ti, 0), pipeline_mode=once),
                  ex((d, d_e)), ex((d, d_e)), ex((1, d_e)), ex((1, d_e)), ex((d_e, d)), ex((1, d))],
        out_specs=pl.BlockSpec((out_rows, LANES), lambda ti, e, o: (ti, 0), pipeline_mode=once),
        scratch_shapes=[rows, rows, pltpu.SMEM((2,), jnp.int32)],
    )
    return pl.pallas_call(
        functools.partial(_moe_kernel, tile),
        grid_spec=grid_spec,
        out_shape=jax.ShapeDtypeStruct((n_t * out_rows, LANES), jnp.float32),
        compiler_params=pltpu.CompilerParams(dimension_semantics=("arbitrary", "arbitrary"),
                                             vmem_limit_bytes=MOE_VMEM_LIMIT),
        name="moe",
    )(offs, tok_sorted, wt_sorted, h2_tiles, w1g, w1l, b1g, b1l, w2, b2)


def _final_kernel(sub, x_ref, m_ref, g2_ref, g_ref, o_ref):
    t = x_ref.shape[1]
    row0 = lax.rem(pl.program_id(1), sub) * t
    x = x_ref[0] + g2_ref[0] * _from_token_tiles(m_ref, row0, t)
    ms = jnp.mean(x * x, axis=-1, keepdims=True)
    o_ref[0] = x * lax.rsqrt(ms + RMS_EPS) * g_ref[...]


def _final(x1, moe_tiles, g2, final_g, moe_tile):
    b, s, d = x1.shape
    t = min(FINAL_TILE, moe_tile)
    per = s // t
    sub = moe_tile // t
    moe_rows = (moe_tile + ROW_UNROLL) * SUBLANES
    tok = pl.BlockSpec((1, t, d), lambda bi, i: (bi, i, 0))
    return pl.pallas_call(
        functools.partial(_final_kernel, sub),
        grid=(b, per),
        in_specs=[tok, pl.BlockSpec((moe_rows, LANES), lambda bi, i: (bi * (per // sub) + i // sub, 0)),
                  pl.BlockSpec((1, 1, d), lambda bi, i: (bi, 0, 0)),
                  pl.BlockSpec((1, d), lambda bi, i: (0, 0))],
        out_specs=tok,
        out_shape=jax.ShapeDtypeStruct((b, s, d), jnp.float32),
        compiler_params=_params(("arbitrary", "arbitrary")),
        name="final_norm",
    )(x1, moe_tiles, g2, final_g)


def _rope_tables(s, n_ctx):
    inv = ROPE_THETA ** (-(jnp.arange(ROPE_QUARTER, dtype=jnp.float32) * 2.0 / ROPE_HALF))
    pos = jnp.arange(s)
    ang_r = (pos // GRID_W).astype(jnp.float32)[None, :] * inv[:, None]
    ang_c = (pos % GRID_W).astype(jnp.float32)[None, :] * inv[:, None]
    ang = jnp.concatenate([ang_r, ang_c], axis=0)
    cos = jnp.concatenate([jnp.cos(ang), jnp.ones((ROPE_HALF, n_ctx), jnp.float32)], axis=1)
    sin = jnp.concatenate([jnp.sin(ang), jnp.zeros((ROPE_HALF, n_ctx), jnp.float32)], axis=1)
    return cos, sin


def _split_bf16(w):
    hi = w.astype(jnp.bfloat16)
    return jnp.stack([hi, (w - hi.astype(jnp.float32)).astype(jnp.bfloat16)])


def _route_lists(top_idx, top_w, tile):
    b, k, s = top_idx.shape
    n_t = b * s // tile
    per = s // tile
    e_flat = top_idx.reshape(b, k, per, tile).transpose(0, 2, 1, 3).reshape(n_t, k * tile)
    w_flat = top_w.reshape(b, k, per, tile).transpose(0, 2, 1, 3).reshape(n_t, k * tile)
    tok = jnp.tile(jnp.arange(tile, dtype=jnp.int32), k)[None, :]
    key_sorted, w_sorted = lax.sort((e_flat * tile + tok, w_flat), dimension=1, num_keys=1)
    tok_sorted = lax.rem(key_sorted, tile)
    counts = jnp.sum(e_flat[:, :, None] == jnp.arange(N_EXPERTS, dtype=jnp.int32)[None, None, :], axis=1,
                     dtype=jnp.int32)
    offs = jnp.concatenate([jnp.zeros((n_t, 1), jnp.int32), jnp.cumsum(counts, axis=1, dtype=jnp.int32)], axis=1)
    pad = ((0, 0), (0, MOE_CHUNK))
    return offs.reshape(-1), jnp.pad(tok_sorted, pad)[:, None, :], jnp.pad(w_sorted, pad)[:, None, :]


def kernel(x, c, ctx, c_ctx, w_mod, b_mod, norm1_g, norm2_g, w_in, q_norm_g, k_norm_g, sink, w_br_a, w_br_b,
           w_o, w_router, b_router, w_e1, b_e1, w_e2, b_e2, final_g):
    b, s, d = x.shape
    n_ctx = ctx.shape[1]
    assert w_mod.shape[0] == 1
    bf = jnp.bfloat16
    t = TOKEN_TILE
    cos, sin = _rope_tables(s, IN_PROJ_SUBS * t)
    moe_tile = min(MOE_TILE, s)
    for l in range(1):
        rows = ((b + 1 + 7) // 8) * 8
        cond = jnp.zeros((rows, d), jnp.float32).at[:b].set(c).at[b].set(c_ctx)
        mods = _adaln(cond, w_mod[l], b_mod[l])
        sh1, sc1, g1, sh2, sc2, g2 = jnp.split(mods, 6, axis=-1)
        mod1 = jnp.stack([jnp.stack([sh1[:b], sc1[:b]], axis=1),
                          jnp.broadcast_to(jnp.stack([sh1[b], sc1[b]], axis=0)[None], (b, 2, d))], axis=1)
        mod2 = jnp.stack([sh2[:b], sc2[:b]], axis=1)

        w = w_in[l]
        kv = w[:, :4 * KV_W]
        wk_t = jnp.concatenate([kv[:, 0:KV_W], kv[:, 2 * KV_W:3 * KV_W]], axis=1).T.astype(bf)
        wv_t = jnp.concatenate([kv[:, KV_W:2 * KV_W], kv[:, 3 * KV_W:4 * KV_W]], axis=1).T.astype(bf)
        wq_t = w[:, 4 * KV_W:4 * KV_W + Q_A + Q_B].T.astype(bf)
        wg_t = w[:, 4 * KV_W + Q_A + Q_B:].T.astype(bf)
        qg = jnp.broadcast_to((q_norm_g[l] * (ATTN_SCALE * LOG2E))[:, None], (HEAD_DIM, t))
        kg = jnp.broadcast_to(k_norm_g[l][:, None], (HEAD_DIM, t))

        qa_t, qb_t, ka, kb, va_t, vb_t, gate_t = _in_proj(
            x, ctx, mod1, norm1_g[l][None], wq_t, wk_t, wv_t, wg_t, qg, kg, cos, sin)

        n_key_tiles = (s + n_ctx) // t
        ya_t = _attention_global(ATTN_GLOBAL_TQ, n_key_tiles, qa_t, ka, va_t)
        yb_t = _attention_window(n_key_tiles, sink[l], qb_t, kb, vb_t)

        x1, h2, top_idx, top_w = _merge(
            x, ya_t, yb_t, gate_t, w_br_a[l].T.astype(bf), w_br_b[l].T.astype(bf), w_o[l].astype(bf),
            g1[:b, None, :], mod2, norm2_g[l][None], _split_bf16(w_router[l].T), b_router[l][:, None])

        offs, tok_sorted, w_sorted = _route_lists(top_idx, top_w, moe_tile)
        w1g, w1l = _w1_split(w_e1[l])
        moe_tiles = _moe(offs, tok_sorted, w_sorted, h2, w1g, w1l,
                         b_e1[l][:, None, 0::2], b_e1[l][:, None, 1::2],
                         w_e2[l].astype(bf), b_e2[l][:, None, :], moe_tile)
        out = _final(x1, moe_tiles, g2[:b, None, :], final_g[None], moe_tile)
    return out
```

```python
import functools

import numpy as np
import jax
import jax.numpy as jnp
from jax import lax
from jax.experimental import pallas as pl
from jax.experimental.pallas import tpu as pltpu

HEAD_DIM = 64
KV_HEADS = 2
A_HEADS = 8
B_HEADS = 8
GROUP = A_HEADS // KV_HEADS
GRID_W = 64
WINDOW = 128
ROPE_HALF = HEAD_DIM // 2
ROPE_QUARTER = ROPE_HALF // 2
ROPE_THETA = 10000.0
N_EXPERTS = 32
TOP_K = 4
SWIGLU_ALPHA = 1.702
SWIGLU_LIMIT = 7.0
RMS_EPS = 1e-6
ATTN_SCALE = HEAD_DIM ** -0.5
LOG2E = 1.4426950408889634
NEG_INF = -1e30
KV_W = KV_HEADS * HEAD_DIM
Q_A = A_HEADS * HEAD_DIM
Q_B = B_HEADS * HEAD_DIM

SUBLANES = 8
LANES = 128
BF16_ROWS = 16
V7X_VMEM_BYTES = 64 * 1024 * 1024
TOKEN_TILE = 256
ADALN_COLS = 1024
IN_PROJ_SUBS = 4
GATE_ROWS = 512
ATTN_GLOBAL_TQ = 128
SPLIT_ROWS = 512
MERGE_TILE = 1024
FINAL_TILE = 512
MOE_TILE = 4096
MOE_CHUNK = 256
ROW_UNROLL = 8
SPLIT_BLOCK = 256
V_ROWS = HEAD_DIM + BF16_ROWS
VMEM_LIMIT = V7X_VMEM_BYTES - 8 * 1024 * 1024
MOE_VMEM_LIMIT = V7X_VMEM_BYTES - 4 * 1024 * 1024

_NT = (((1,), (1,)), ((), ()))


def _dot_nt(a, b, precision=None):
    return lax.dot_general(a, b, _NT, preferred_element_type=jnp.float32, precision=precision)


def _dot(a, b, precision=None):
    return jnp.dot(a, b, preferred_element_type=jnp.float32, precision=precision)


def _params(sem):
    return pltpu.CompilerParams(dimension_semantics=sem, vmem_limit_bytes=VMEM_LIMIT)


def _adaln_kernel(cond_ref, w_ref, b_ref, o_ref):
    cond = cond_ref[...]
    act = cond * jax.nn.sigmoid(cond)
    o_ref[...] = _dot(act, w_ref[...], precision=lax.Precision.HIGHEST) + b_ref[...]


def _adaln(cond, w, b):
    rows, d = cond.shape
    n = w.shape[1]
    tn = ADALN_COLS
    return pl.pallas_call(
        _adaln_kernel,
        grid=(n // tn,),
        in_specs=[pl.BlockSpec((rows, d), lambda j: (0, 0)),
                  pl.BlockSpec((d, tn), lambda j: (0, j)),
                  pl.BlockSpec((1, tn), lambda j: (0, j))],
        out_specs=pl.BlockSpec((rows, tn), lambda j: (0, j)),
        out_shape=jax.ShapeDtypeStruct((rows, n), jnp.float32),
        compiler_params=_params(("arbitrary",)),
        name="adaln",
    )(cond, w, b.reshape(1, n))


def _rope_t(xh, cos, sin):
    q = ROPE_QUARTER
    a0, a1, b0, b1 = xh[0:q], xh[q:2 * q], xh[2 * q:3 * q], xh[3 * q:4 * q]
    cr, cc = cos[0:q], cos[q:2 * q]
    sr, sc = sin[0:q], sin[q:2 * q]
    return jnp.concatenate([a0 * cr - a1 * sr, a1 * cr + a0 * sr,
                            b0 * cc - b1 * sc, b1 * cc + b0 * sc], axis=0)


def _head_norm_t(xh, g):
    ms = jnp.mean(xh * xh, axis=0, keepdims=True)
    return xh * lax.rsqrt(ms + RMS_EPS) * g


def _mod_norm(x, g, shift, scale):
    ms = jnp.mean(x * x, axis=-1, keepdims=True)
    return x * lax.rsqrt(ms + RMS_EPS) * g * (1.0 + scale) + shift


def _in_proj_kernel(n_lat, x_ref, ctx_ref, mod_ref, g_ref, wq_ref, wk_ref, wv_ref, wg_ref,
                    qg_ref, kg_ref, cos_ref, sin_ref,
                    qa_ref, qb_ref, ka_ref, kb_ref, va_ref, vb_ref, gate_ref, h_ref):
    i = pl.program_id(1)
    t = TOKEN_TILE
    shift = mod_ref[0, 0, 0:1, :]
    scale = mod_ref[0, 0, 1:2, :]
    is_lat = i < n_lat

    for sub in range(IN_PROJ_SUBS):
        @pl.when(is_lat)
        def _(sub=sub):
            h_ref[sub] = _mod_norm(x_ref[0, sub * t:(sub + 1) * t, :], g_ref[...], shift, scale).astype(jnp.bfloat16)

        @pl.when(jnp.logical_not(is_lat))
        def _(sub=sub):
            if sub == 0:
                h_ref[sub] = _mod_norm(ctx_ref[0], g_ref[...], shift, scale).astype(jnp.bfloat16)
            else:
                h_ref[sub] = jnp.zeros(h_ref.shape[1:], jnp.bfloat16)

    kg = kg_ref[...]
    ones = jnp.ones((V_ROWS - HEAD_DIM, t), jnp.float32)
    for sub in range(IN_PROJ_SUBS):
        h = h_ref[sub]
        cos = cos_ref[:, sub * t:(sub + 1) * t]
        sin = sin_ref[:, sub * t:(sub + 1) * t]
        kt = _dot_nt(wk_ref[...], h)
        ka = [_rope_t(_head_norm_t(kt[j * HEAD_DIM:(j + 1) * HEAD_DIM], kg), cos, sin) for j in range(KV_HEADS)]
        kb = [_rope_t(kt[KV_W + j * HEAD_DIM:KV_W + (j + 1) * HEAD_DIM], cos, sin) for j in range(KV_HEADS)]
        ka_ref[0, sub] = jnp.concatenate(ka, axis=0).T.astype(jnp.bfloat16)
        kb_ref[0, sub] = jnp.concatenate(kb, axis=0).T.astype(jnp.bfloat16)
        vt = _dot_nt(wv_ref[...], h)
        for br, v_ref in enumerate((va_ref, vb_ref)):
            rows = []
            for j in range(KV_HEADS):
                rows += [vt[br * KV_W + j * HEAD_DIM:br * KV_W + (j + 1) * HEAD_DIM], ones]
            v_ref[0, sub] = jnp.concatenate(rows, axis=0).astype(jnp.bfloat16)

    @pl.when(is_lat)
    def _():
        qg = qg_ref[...]
        for sub in range(IN_PROJ_SUBS):
            h = h_ref[sub]
            tok = slice(sub * t, (sub + 1) * t)
            cos = cos_ref[:, tok]
            sin = sin_ref[:, tok]
            for half in range(2):
                qt = _dot_nt(wq_ref[half * Q_A:(half + 1) * Q_A, :], h)
                for hd in range(A_HEADS):
                    xh = qt[hd * HEAD_DIM:(hd + 1) * HEAD_DIM]
                    if half == 0:
                        out = _rope_t(_head_norm_t(xh, qg), cos, sin)
                        qa_ref[0, hd * HEAD_DIM:(hd + 1) * HEAD_DIM, tok] = out.astype(jnp.bfloat16)
                    else:
                        out = _rope_t(xh * (ATTN_SCALE * LOG2E), cos, sin)
                        qb_ref[0, hd * HEAD_DIM:(hd + 1) * HEAD_DIM, tok] = out.astype(jnp.bfloat16)
            rows = GATE_ROWS
            for c in range(wg_ref.shape[0] // rows):
                gt = _dot_nt(wg_ref[c * rows:(c + 1) * rows, :], h)
                gate_ref[0, c * rows:(c + 1) * rows, tok] = jax.nn.sigmoid(gt).astype(jnp.bfloat16)


def _in_proj(x, ctx, mod, norm_g, wq_t, wk_t, wv_t, wg_t, qg, kg, cos, sin):
    b, s, d = x.shape
    t = TOKEN_TILE
    subs = IN_PROJ_SUBS
    st = subs * t
    assert ctx.shape[1] == t and s % st == 0
    n_lat = s // st
    n_tiles = (n_lat + 1) * subs
    last = n_lat - 1
    full = lambda shape: pl.BlockSpec(shape, lambda bi, i: (0,) * len(shape))
    q_spec = pl.BlockSpec((1, Q_A, st), lambda bi, i: (bi, 0, jnp.minimum(i, last)))
    k_spec = pl.BlockSpec((1, subs, t, KV_W), lambda bi, i: (bi, i, 0, 0))
    v_spec = pl.BlockSpec((1, subs, KV_HEADS * V_ROWS, t), lambda bi, i: (bi, i, 0, 0))
    bf = jnp.bfloat16
    return pl.pallas_call(
        functools.partial(_in_proj_kernel, n_lat),
        grid=(b, n_lat + 1),
        in_specs=[pl.BlockSpec((1, st, d), lambda bi, i: (bi, jnp.minimum(i, last), 0)),
                  pl.BlockSpec((1, t, d), lambda bi, i: (bi, 0, 0)),
                  pl.BlockSpec((1, 1, 2, d), lambda bi, i: (bi, i // n_lat, 0, 0)),
                  full((1, d)), full(wq_t.shape), full(wk_t.shape), full(wv_t.shape), full(wg_t.shape),
                  full((HEAD_DIM, t)), full((HEAD_DIM, t)),
                  pl.BlockSpec((ROPE_HALF, st), lambda bi, i: (0, i)),
                  pl.BlockSpec((ROPE_HALF, st), lambda bi, i: (0, i))],
        out_specs=[q_spec, q_spec, k_spec, k_spec, v_spec, v_spec,
                   pl.BlockSpec((1, 2 * d, st), lambda bi, i: (bi, 0, jnp.minimum(i, last)))],
        out_shape=[jax.ShapeDtypeStruct((b, Q_A, s), bf), jax.ShapeDtypeStruct((b, Q_B, s), bf),
                   jax.ShapeDtypeStruct((b, n_tiles, t, KV_W), bf), jax.ShapeDtypeStruct((b, n_tiles, t, KV_W), bf),
                   jax.ShapeDtypeStruct((b, n_tiles, KV_HEADS * V_ROWS, t), bf),
                   jax.ShapeDtypeStruct((b, n_tiles, KV_HEADS * V_ROWS, t), bf),
                   jax.ShapeDtypeStruct((b, 2 * d, s), bf)],
        scratch_shapes=[pltpu.VMEM((subs, t, d), bf)],
        compiler_params=_params(("arbitrary", "arbitrary")),
        name="in_proj",
    )(x, ctx, mod, norm_g, wq_t, wk_t, wv_t, wg_t, qg, kg, cos, sin)


def _attn_window_kernel(subs, n_lat, n_tiles, sink_ref, q_ref, k_ref, v_ref, y_ref, s_ref):
    t = tq = TOKEN_TILE
    n = GROUP * tq
    assert WINDOW * 2 == t
    heads = [[kvh * GROUP + g for g in range(GROUP)] for kvh in range(KV_HEADS)]
    n_slots = s_ref.shape[1] // subs
    sink = [jnp.concatenate([jnp.full((1, tq), sink_ref[hd] * LOG2E, jnp.float32) for hd in heads[kvh]], axis=1)
            for kvh in range(KV_HEADS)]
    c_half = lax.rem(lax.broadcasted_iota(jnp.int32, (WINDOW, n), 1), tq)
    r_half = lax.broadcasted_iota(jnp.int32, (WINDOW, n), 0)
    c_full = lax.rem(lax.broadcasted_iota(jnp.int32, (t, n), 1), tq)
    r_full = lax.broadcasted_iota(jnp.int32, (t, n), 0)
    acc_row = lax.broadcasted_iota(jnp.int32, (V_ROWS, n), 0)

    for sub in range(subs):
        qi = pl.program_id(1) * subs + sub
        cols = slice(sub * tq, (sub + 1) * tq)
        qpad = []
        for kvh in range(KV_HEADS):
            qcat = jnp.concatenate([q_ref[0, hd * HEAD_DIM:(hd + 1) * HEAD_DIM, cols] for hd in heads[kvh]], axis=1)
            zeros = jnp.zeros_like(qcat)
            qpad.append(jnp.concatenate([qcat, zeros] if kvh == 0 else [zeros, qcat], axis=0))

        off_prev = jnp.where(qi >= 1, 0, 2 * t)
        off_next = jnp.where(qi + 1 < n_lat, 0, 2 * t)
        pieces = [
            (jnp.maximum(qi - 1, 0), WINDOW, WINDOW, r_half >= c_half + off_prev),
            (qi, 0, t, jnp.abs(r_full - c_full) <= WINDOW),
            (jnp.minimum(qi + 1, n_lat - 1), 0, WINDOW, c_half - r_half >= WINDOW + off_next),
        ] + [(kb, 0, t, None) for kb in range(n_lat, n_tiles)]

        m8 = [jnp.broadcast_to(sink[kvh], (8, n)) for kvh in range(KV_HEADS)]
        for slot, (kb, lo, rows, ok) in enumerate(pieces):
            kblk = k_ref[0, kb, lo:lo + rows, :]
            for kvh in range(KV_HEADS):
                s = _dot(kblk, qpad[kvh])
                if ok is not None:
                    s = jnp.where(ok, s, NEG_INF)
                s_ref[kvh, sub * n_slots + slot, 0:rows, :] = s
                m8[kvh] = jnp.maximum(m8[kvh], jnp.max(s.reshape(rows // 8, 8, n), axis=0))
        m = [jnp.max(m8[kvh], axis=0, keepdims=True) for kvh in range(KV_HEADS)]

        accs = [jnp.where(acc_row == HEAD_DIM, jnp.exp2(sink[kvh] - m[kvh]), 0.0) for kvh in range(KV_HEADS)]
        for slot, (kb, lo, rows, _) in enumerate(pieces):
            for kvh in range(KV_HEADS):
                p = jnp.exp2(s_ref[kvh, sub * n_slots + slot, 0:rows, :] - m[kvh]).astype(jnp.bfloat16)
                vblk = v_ref[0, kb, kvh * V_ROWS:(kvh + 1) * V_ROWS, lo:lo + rows]
                accs[kvh] = accs[kvh] + _dot(vblk, p)
        for kvh in range(KV_HEADS):
            out = accs[kvh][0:HEAD_DIM] / accs[kvh][HEAD_DIM:HEAD_DIM + 1]
            for g, hd in enumerate(heads[kvh]):
                y_ref[0, hd * HEAD_DIM:(hd + 1) * HEAD_DIM, cols] = out[:, g * tq:(g + 1) * tq].astype(jnp.bfloat16)


def _attn_global_kernel(tq, n_tiles, n_q, q_ref, k_ref, v_ref, y_ref, s0_ref, s1_ref, m0_ref, m1_ref):
    unroll = True
    j = pl.program_id(0)
    t = TOKEN_TILE
    n = GROUP * tq
    heads = [[kvh * GROUP + g for g in range(GROUP)] for kvh in range(KV_HEADS)]
    bufs = ((s0_ref, m0_ref), (s1_ref, m1_ref))

    def padded_q():
        qpad = []
        for kvh in range(KV_HEADS):
            qcat = jnp.concatenate([q_ref[0, hd * HEAD_DIM:(hd + 1) * HEAD_DIM, :] for hd in heads[kvh]], axis=1)
            zeros = jnp.zeros_like(qcat)
            qpad.append(jnp.concatenate([qcat, zeros] if kvh == 0 else [zeros, qcat], axis=0))
        return qpad

    def scores(s_ref, kb, m8, qpad):
        kblk = k_ref[0, kb]
        out = []
        for kvh in range(KV_HEADS):
            s = _dot(kblk, qpad[kvh])
            s_ref[kvh, kb] = s
            out.append(jnp.maximum(m8[kvh], jnp.max(s.reshape(t // 8, 8, n), axis=0)))
        return tuple(out)

    def weigh(s_ref, kb, m, accs):
        out = []
        for kvh in range(KV_HEADS):
            p = jnp.exp2(s_ref[kvh, kb] - m[kvh]).astype(jnp.bfloat16)
            vblk = v_ref[0, kb, kvh * V_ROWS:(kvh + 1) * V_ROWS, :]
            out.append(accs[kvh] + _dot(vblk, p))
        return tuple(out)

    def keep_maxima(m_ref, m8):
        for kvh in range(KV_HEADS):
            m_ref[kvh] = m8[kvh]

    def maxima(m_ref):
        return [jnp.max(m_ref[kvh], axis=0, keepdims=True) for kvh in range(KV_HEADS)]

    def write(accs):
        for kvh in range(KV_HEADS):
            out = accs[kvh][0:HEAD_DIM] / accs[kvh][HEAD_DIM:HEAD_DIM + 1]
            for g, hd in enumerate(heads[kvh]):
                y_ref[0, hd * HEAD_DIM:(hd + 1) * HEAD_DIM, :] = out[:, g * tq:(g + 1) * tq].astype(jnp.bfloat16)

    m8_0 = tuple(jnp.full((8, n), -jnp.inf, jnp.float32) for _ in range(KV_HEADS))
    acc_0 = tuple(jnp.zeros((V_ROWS, n), jnp.float32) for _ in range(KV_HEADS))

    @pl.when(j == 0)
    def _():
        s_w, m_w = bufs[0]
        qpad = padded_q()
        keep_maxima(m_w, lax.fori_loop(0, n_tiles, lambda kb, c: scores(s_w, kb, c, qpad), m8_0, unroll=unroll))

    for parity in range(2):
        (s_w, m_w), (s_r, m_r) = bufs[parity], bufs[1 - parity]

        @pl.when((j > 0) & (j < n_q) & (lax.rem(j, 2) == parity))
        def _():
            qpad = padded_q()
            m = maxima(m_r)
            m8, accs = lax.fori_loop(
                0, n_tiles, lambda kb, c: (scores(s_w, kb, c[0], qpad), weigh(s_r, kb, m, c[1])),
                (m8_0, acc_0), unroll=unroll)
            keep_maxima(m_w, m8)
            write(accs)

    @pl.when(j == n_q)
    def _():
        s_r, m_r = bufs[(n_q - 1) % 2]
        m = maxima(m_r)
        write(lax.fori_loop(0, n_tiles, lambda kb, c: weigh(s_r, kb, m, c), acc_0, unroll=unroll))


def _attention_global(tq, n_tiles, q_t, k, v_t):
    b, hq, s = q_t.shape
    per = s // tq
    n_q = b * per
    n = GROUP * tq
    scored = lambda j: jnp.minimum(j, n_q - 1)
    weighed = lambda j: jnp.maximum(j - 1, 0)
    return pl.pallas_call(
        functools.partial(_attn_global_kernel, tq, n_tiles, n_q),
        grid=(n_q + 1,),
        in_specs=[pl.BlockSpec((1, hq, tq), lambda j: (scored(j) // per, 0, scored(j) % per)),
                  pl.BlockSpec((1,) + k.shape[1:], lambda j: (scored(j) // per, 0, 0, 0)),
                  pl.BlockSpec((1,) + v_t.shape[1:], lambda j: (weighed(j) // per, 0, 0, 0))],
        out_specs=pl.BlockSpec((1, hq, tq), lambda j: (weighed(j) // per, 0, weighed(j) % per)),
        out_shape=jax.ShapeDtypeStruct((b, hq, s), jnp.bfloat16),
        scratch_shapes=[pltpu.VMEM((KV_HEADS, n_tiles, TOKEN_TILE, n), jnp.float32)] * 2
        + [pltpu.VMEM((KV_HEADS, 8, n), jnp.float32)] * 2,
        compiler_params=_params(("arbitrary",)),
        name="attn_global",
    )(q_t, k, v_t)


def _attention_window(n_tiles, sink, q_t, k, v_t):
    b, hq, s = q_t.shape
    t = TOKEN_TILE
    n_lat = s // t
    subs = 4 if n_lat % 4 == 0 else (2 if n_lat % 2 == 0 else 1)
    slots = 3 + (n_tiles - n_lat)
    grid_spec = pltpu.PrefetchScalarGridSpec(
        num_scalar_prefetch=1,
        grid=(b, n_lat // subs),
        in_specs=[pl.BlockSpec((1, hq, subs * t), lambda bi, qi, sk: (bi, 0, qi)),
                  pl.BlockSpec((1,) + k.shape[1:], lambda bi, qi, sk: (bi, 0, 0, 0)),
                  pl.BlockSpec((1,) + v_t.shape[1:], lambda bi, qi, sk: (bi, 0, 0, 0))],
        out_specs=pl.BlockSpec((1, hq, subs * t), lambda bi, qi, sk: (bi, 0, qi)),
        scratch_shapes=[pltpu.VMEM((KV_HEADS, subs * slots, t, GROUP * t), jnp.float32)],
    )
    return pl.pallas_call(
        functools.partial(_attn_window_kernel, subs, n_lat, n_tiles),
        grid_spec=grid_spec,
        out_shape=jax.ShapeDtypeStruct((b, hq, s), jnp.bfloat16),
        compiler_params=_params(("arbitrary", "arbitrary")),
        name="attn_window",
    )(sink, q_t, k, v_t)


def _to_token_tiles(ref, row0, rows, val):
    for j in range(SUBLANES):
        ref[pl.ds(row0 * SUBLANES + j, rows, stride=SUBLANES), :] = val[:, j * LANES:(j + 1) * LANES]


def _from_token_tiles(ref, row0, rows):
    return jnp.concatenate([ref[pl.ds(row0 * SUBLANES + j, rows, stride=SUBLANES), :] for j in range(SUBLANES)],
                           axis=1)


def _merge_kernel(x_ref, ya_ref, yb_ref, gate_ref, wa_ref, wb_ref, wo_ref, g1_ref, mod_ref, g2_ref,
                  wr_ref, br_ref, x1_ref, h2_ref, idx_ref, wt_ref):
    d = x_ref.shape[-1]
    t = TOKEN_TILE
    for sub in range(x_ref.shape[1] // t):
        tok = slice(sub * t, (sub + 1) * t)
        za = _dot(wa_ref[...], ya_ref[0, :, tok])
        zb = _dot(wb_ref[...], yb_ref[0, :, tok])
        mt = (gate_ref[0, 0:d, tok].astype(jnp.float32) * za
              + gate_ref[0, d:2 * d, tok].astype(jnp.float32) * zb)
        o = _dot(mt.T.astype(jnp.bfloat16), wo_ref[...])
        x1 = x_ref[0, tok, :] + g1_ref[0] * o
        x1_ref[0, tok, :] = x1
        h2 = _mod_norm(x1, g2_ref[...], mod_ref[0, 0:1, :], mod_ref[0, 1:2, :])
        _to_token_tiles(h2_ref, sub * t, t, h2)
        h_hi = h2.astype(jnp.bfloat16)
        h_lo = (h2 - h_hi.astype(jnp.float32)).astype(jnp.bfloat16)
        logits = (_dot_nt(wr_ref[0], h_hi) + _dot_nt(wr_ref[0], h_lo) + _dot_nt(wr_ref[1], h_hi)
                  + br_ref[...])
        n_e = logits.shape[0]
        row = lax.broadcasted_iota(jnp.int32, logits.shape, 0)
        work = logits
        top_v, top_i = [], []
        for _ in range(TOP_K):
            mk = jnp.max(work, axis=0, keepdims=True)
            ik = jnp.min(jnp.where(work == mk, row, n_e), axis=0, keepdims=True)
            top_v.append(mk)
            top_i.append(ik)
            work = jnp.where(row == ik, -jnp.inf, work)
        ex = [jnp.exp(v - top_v[0]) for v in top_v]
        den = ex[0] + ex[1] + ex[2] + ex[3]
        idx_ref[0, :, tok] = jnp.concatenate(top_i, axis=0)
        wt_ref[0, :, tok] = jnp.concatenate([e / den for e in ex], axis=0)


def _merge(x, ya_t, yb_t, gate_t, wa_t, wb_t, wo, g1, mod2, norm2_g, wr_t, br):
    b, s, d = x.shape
    t = min(MERGE_TILE, s)
    per = s // t
    assert d == SUBLANES * LANES
    full = lambda shape: pl.BlockSpec(shape, lambda bi, i: (0,) * len(shape))
    tok = pl.BlockSpec((1, t, d), lambda bi, i: (bi, i, 0))
    col = lambda rows: pl.BlockSpec((1, rows, t), lambda bi, i: (bi, 0, i))
    return pl.pallas_call(
        _merge_kernel,
        grid=(b, per),
        in_specs=[tok, col(Q_A), col(Q_B), col(2 * d),
                  full(wa_t.shape), full(wb_t.shape), full(wo.shape),
                  pl.BlockSpec((1, 1, d), lambda bi, i: (bi, 0, 0)),
                  pl.BlockSpec((1, 2, d), lambda bi, i: (bi, 0, 0)),
                  full((1, d)), full(wr_t.shape), full(br.shape)],
        out_specs=[tok, pl.BlockSpec((t * SUBLANES, LANES), lambda bi, i: (bi * per + i, 0)),
                   col(TOP_K), col(TOP_K)],
        out_shape=[jax.ShapeDtypeStruct((b, s, d), jnp.float32),
                   jax.ShapeDtypeStruct((b * s * SUBLANES, LANES), jnp.float32),
                   jax.ShapeDtypeStruct((b, TOP_K, s), jnp.int32), jax.ShapeDtypeStruct((b, TOP_K, s), jnp.float32)],
        compiler_params=_params(("arbitrary", "arbitrary")),
        name="merge_router",
    )(x, ya_t, yb_t, gate_t, wa_t, wb_t, wo, g1, mod2, norm2_g, wr_t, br)


def _w1_split_kernel(w_ref, p_ref, g_ref, l_ref):
    half = SPLIT_BLOCK // 2
    for jb in range(w_ref.shape[-1] // SPLIT_BLOCK):
        blk = w_ref[0, :, jb * SPLIT_BLOCK:(jb + 1) * SPLIT_BLOCK].astype(jnp.bfloat16)
        r = _dot(blk, p_ref[...])
        g_ref[0, :, jb * half:(jb + 1) * half] = r[:, :half].astype(jnp.bfloat16)
        l_ref[0, :, jb * half:(jb + 1) * half] = r[:, half:].astype(jnp.bfloat16)


def _w1_split(w1):
    n_e, d, d2 = w1.shape
    rows = SPLIT_ROWS
    half = SPLIT_BLOCK // 2
    perm = np.zeros((SPLIT_BLOCK, SPLIT_BLOCK), np.float32)
    for j in range(half):
        perm[2 * j, j] = 1.0
        perm[2 * j + 1, half + j] = 1.0
    out = pl.BlockSpec((1, rows, d2 // 2), lambda e, r: (e, r, 0))
    return pl.pallas_call(
        _w1_split_kernel,
        grid=(n_e, d // rows),
        in_specs=[pl.BlockSpec((1, rows, d2), lambda e, r: (e, r, 0)),
                  pl.BlockSpec((SPLIT_BLOCK, SPLIT_BLOCK), lambda e, r: (0, 0))],
        out_specs=[out, out],
        out_shape=[jax.ShapeDtypeStruct((n_e, d, d2 // 2), jnp.bfloat16)] * 2,
        compiler_params=_params(("arbitrary", "arbitrary")),
        name="w1_split",
    )(w1, jnp.asarray(perm, jnp.bfloat16))


def _moe_kernel(tile, offs_ref, tok_ref, wt_ref, h_ref, w1g_ref, w1l_ref, b1g_ref, b1l_ref, w2_ref, b2_ref,
                acc_ref, xg_ref, y_ref, pend_ref):
    ti = pl.program_id(0)
    e = pl.program_id(1)
    n_e = pl.num_programs(1)
    ch = xg_ref.shape[0] // SUBLANES
    off = ti * (n_e + 1)

    def tile_rows(r):
        return pl.ds(pl.multiple_of(r * SUBLANES, SUBLANES), SUBLANES)

    def static_rows(r):
        return slice(r * SUBLANES, (r + 1) * SUBLANES)

    def expert_mlp(xg):
        d_e = w2_ref.shape[1]
        half = d_e // 2
        y = b2_ref[0]
        for c0 in range(0, d_e, half):
            cols = slice(c0, c0 + half)
            glu = jnp.minimum(_dot(xg, w1g_ref[0, :, cols]) + b1g_ref[0, :, cols], SWIGLU_LIMIT)
            lin = jnp.clip(_dot(xg, w1l_ref[0, :, cols]) + b1l_ref[0, :, cols], -SWIGLU_LIMIT, SWIGLU_LIMIT)
            act = glu * jax.nn.sigmoid(SWIGLU_ALPHA * glu) * (lin + 1.0)
            y = y + _dot(act.astype(jnp.bfloat16), w2_ref[0, cols, :])
        return y

    def gather_rows(dst_ref, base, n):
        def group(i, _):
            r0 = pl.multiple_of(i * ROW_UNROLL, ROW_UNROLL)
            for u in range(ROW_UNROLL):
                dst_ref[tile_rows(r0 + u), :] = h_ref[tile_rows(tok_ref[0, 0, base + r0 + u]), :]
            return 0

        lax.fori_loop(0, (n + ROW_UNROLL - 1) // ROW_UNROLL, group, 0)

    def scatter_rows(src_ref, base, n):
        def group(i, _):
            r0 = pl.multiple_of(i * ROW_UNROLL, ROW_UNROLL)
            toks = [tok_ref[0, 0, base + r0 + u] for u in range(ROW_UNROLL)]
            new = [acc_ref[tile_rows(toks[u]), :] + wt_ref[0, 0, base + r0 + u] * src_ref[tile_rows(r0 + u), :]
                   for u in range(ROW_UNROLL)]
            for u in range(ROW_UNROLL):
                acc_ref[tile_rows(toks[u]), :] = new[u]
            return 0

        full_groups = n // ROW_UNROLL
        lax.fori_loop(0, full_groups, group, 0)

        def row(r, _):
            tok = tok_ref[0, 0, base + r]
            acc_ref[tile_rows(tok), :] = acc_ref[tile_rows(tok), :] + wt_ref[0, 0, base + r] * src_ref[tile_rows(r), :]
            return 0

        lax.fori_loop(full_groups * ROW_UNROLL, n, row, 0)

    start = offs_ref[off + e]
    count = offs_ref[off + e + 1] - start
    n_blocks = jnp.maximum((count + ch - 1) // ch, 1)
    start_next = offs_ref[off + jnp.minimum(e + 1, n_e - 1)]

    @pl.when((ti == 0) & (e == 0))
    def _():
        for ref in (xg_ref, y_ref):
            ref[...] = jnp.zeros_like(ref)

    @pl.when(e == 0)
    def _():
        acc_ref[...] = jnp.zeros_like(acc_ref)
        pend_ref[0] = 0
        pend_ref[1] = 0
        gather_rows(xg_ref, start, jnp.minimum(count, ch))

    def block_body(rows, k, base_prev, n_prev):
        xg = _from_token_tiles(xg_ref, 0, rows).astype(jnp.bfloat16)

        base_next = jnp.where(k + 1 < n_blocks, start + (k + 1) * ch, start_next)
        for r in range(ch):
            xg_ref[static_rows(r), :] = h_ref[tile_rows(tok_ref[0, 0, base_next + r]), :]

        for g in range(ch // ROW_UNROLL):
            toks, new = [], []
            for u in range(ROW_UNROLL):
                r = g * ROW_UNROLL + u
                valid = r < n_prev
                tok = jnp.where(valid, tok_ref[0, 0, base_prev + r], tile + u)
                w = jnp.where(valid, wt_ref[0, 0, base_prev + r], 0.0)
                toks.append(tok)
                new.append(acc_ref[tile_rows(tok), :] + w * y_ref[static_rows(r), :])
            for u in range(ROW_UNROLL):
                acc_ref[tile_rows(toks[u]), :] = new[u]

        _to_token_tiles(y_ref, 0, rows, expert_mlp(xg))

    def block(k, pending):
        base_prev, n_prev = pending
        n = jnp.minimum(ch, count - k * ch)
        small = n <= ch // 2
        pl.when(small)(functools.partial(block_body, ch // 2, k, base_prev, n_prev))
        pl.when(jnp.logical_not(small))(functools.partial(block_body, ch, k, base_prev, n_prev))
        return start + k * ch, n

    base_prev, n_prev = lax.fori_loop(0, n_blocks, block, (pend_ref[0], pend_ref[1]))
    pend_ref[0] = base_prev
    pend_ref[1] = n_prev

    @pl.when(e == n_e - 1)
    def _():
        scatter_rows(y_ref, base_prev, n_prev)


def _moe(offs, tok_sorted, wt_sorted, h2_tiles, w1g, w1l, b1g, b1l, w2, b2, tile):
    n_tok = h2_tiles.shape[0] // SUBLANES
    n_e, d, d_e = w1g.shape
    n_t = n_tok // tile
    a = tok_sorted.shape[-1]
    assert d == SUBLANES * LANES and MOE_CHUNK % ROW_UNROLL == 0
    smem = lambda: pl.BlockSpec((1, 1, a), lambda ti, e, o: (ti, 0, 0), memory_space=pltpu.SMEM)
    ex = lambda shape: pl.BlockSpec((1,) + shape, lambda ti, e, o: (e, 0, 0))
    out_rows = (tile + ROW_UNROLL) * SUBLANES
    rows = pltpu.VMEM((MOE_CHUNK * SUBLANES, LANES), jnp.float32)
    once = pl.Buffered(1)
    grid_spec = pltpu.PrefetchScalarGridSpec(
        num_scalar_prefetch=1,
        grid=(n_t, n_e),
        in_specs=[smem(), smem(),
                  pl.BlockSpec((tile * SUBLANES, LANES), lambda ti, e, o: (ti, 0), pipeline_mode=once),
                  ex((d, d_e)), ex((d, d_e)), ex((1, d_e)), ex((1, d_e)), ex((d_e, d)), ex((1, d))],
        out_specs=pl.BlockSpec((out_rows, LANES), lambda ti, e, o: (ti, 0), pipeline_mode=once),
        scratch_shapes=[rows, rows, pltpu.SMEM((2,), jnp.int32)],
    )
    return pl.pallas_call(
        functools.partial(_moe_kernel, tile),
        grid_spec=grid_spec,
        out_shape=jax.ShapeDtypeStruct((n_t * out_rows, LANES), jnp.float32),
        compiler_params=pltpu.CompilerParams(dimension_semantics=("arbitrary", "arbitrary"),
                                             vmem_limit_bytes=MOE_VMEM_LIMIT),
        name="moe",
    )(offs, tok_sorted, wt_sorted, h2_tiles, w1g, w1l, b1g, b1l, w2, b2)


def _final_kernel(sub, x_ref, m_ref, g2_ref, g_ref, o_ref):
    t = x_ref.shape[1]
    row0 = lax.rem(pl.program_id(1), sub) * t
    x = x_ref[0] + g2_ref[0] * _from_token_tiles(m_ref, row0, t)
    ms = jnp.mean(x * x, axis=-1, keepdims=True)
    o_ref[0] = x * lax.rsqrt(ms + RMS_EPS) * g_ref[...]


def _final(x1, moe_tiles, g2, final_g, moe_tile):
    b, s, d = x1.shape
    t = min(FINAL_TILE, moe_tile)
    per = s // t
    sub = moe_tile // t
    moe_rows = (moe_tile + ROW_UNROLL) * SUBLANES
    tok = pl.BlockSpec((1, t, d), lambda bi, i: (bi, i, 0))
    return pl.pallas_call(
        functools.partial(_final_kernel, sub),
        grid=(b, per),
        in_specs=[tok, pl.BlockSpec((moe_rows, LANES), lambda bi, i: (bi * (per // sub) + i // sub, 0)),
                  pl.BlockSpec((1, 1, d), lambda bi, i: (bi, 0, 0)),
                  pl.BlockSpec((1, d), lambda bi, i: (0, 0))],
        out_specs=tok,
        out_shape=jax.ShapeDtypeStruct((b, s, d), jnp.float32),
        compiler_params=_params(("arbitrary", "arbitrary")),
        name="final_norm",
    )(x1, moe_tiles, g2, final_g)


def _rope_tables(s, n_ctx):
    inv = ROPE_THETA ** (-(jnp.arange(ROPE_QUARTER, dtype=jnp.float32) * 2.0 / ROPE_HALF))
    pos = jnp.arange(s)
    ang_r = (pos // GRID_W).astype(jnp.float32)[None, :] * inv[:, None]
    ang_c = (pos % GRID_W).astype(jnp.float32)[None, :] * inv[:, None]
    ang = jnp.concatenate([ang_r, ang_c], axis=0)
    cos = jnp.concatenate([jnp.cos(ang), jnp.ones((ROPE_HALF, n_ctx), jnp.float32)], axis=1)
    sin = jnp.concatenate([jnp.sin(ang), jnp.zeros((ROPE_HALF, n_ctx), jnp.float32)], axis=1)
    return cos, sin


def _split_bf16(w):
    hi = w.astype(jnp.bfloat16)
    return jnp.stack([hi, (w - hi.astype(jnp.float32)).astype(jnp.bfloat16)])


def _route_lists(top_idx, top_w, tile):
    b, k, s = top_idx.shape
    n_t = b * s // tile
    per = s // tile
    e_flat = top_idx.reshape(b, k, per, tile).transpose(0, 2, 1, 3).reshape(n_t, k * tile)
    w_flat = top_w.reshape(b, k, per, tile).transpose(0, 2, 1, 3).reshape(n_t, k * tile)
    tok = jnp.tile(jnp.arange(tile, dtype=jnp.int32), k)[None, :]
    key_sorted, w_sorted = lax.sort((e_flat * tile + tok, w_flat), dimension=1, num_keys=1)
    tok_sorted = lax.rem(key_sorted, tile)
    counts = jnp.sum(e_flat[:, :, None] == jnp.arange(N_EXPERTS, dtype=jnp.int32)[None, None, :], axis=1,
                     dtype=jnp.int32)
    offs = jnp.concatenate([jnp.zeros((n_t, 1), jnp.int32), jnp.cumsum(counts, axis=1, dtype=jnp.int32)], axis=1)
    pad = ((0, 0), (0, MOE_CHUNK))
    return offs.reshape(-1), jnp.pad(tok_sorted, pad)[:, None, :], jnp.pad(w_sorted, pad)[:, None, :]


def kernel(x, c, ctx, c_ctx, w_mod, b_mod, norm1_g, norm2_g, w_in, q_norm_g, k_norm_g, sink, w_br_a, w_br_b,
           w_o, w_router, b_router, w_e1, b_e1, w_e2, b_e2, final_g):
    b, s, d = x.shape
    n_ctx = ctx.shape[1]
    assert w_mod.shape[0] == 1
    bf = jnp.bfloat16
    t = TOKEN_TILE
    cos, sin = _rope_tables(s, IN_PROJ_SUBS * t)
    moe_tile = min(MOE_TILE, s)
    for l in range(1):
        rows = ((b + 1 + 7) // 8) * 8
        cond = jnp.zeros((rows, d), jnp.float32).at[:b].set(c).at[b].set(c_ctx)
        mods = _adaln(cond, w_mod[l], b_mod[l])
        sh1, sc1, g1, sh2, sc2, g2 = jnp.split(mods, 6, axis=-1)
        mod1 = jnp.stack([jnp.stack([sh1[:b], sc1[:b]], axis=1),
                          jnp.broadcast_to(jnp.stack([sh1[b], sc1[b]], axis=0)[None], (b, 2, d))], axis=1)
        mod2 = jnp.stack([sh2[:b], sc2[:b]], axis=1)

        w = w_in[l]
        kv = w[:, :4 * KV_W]
        wk_t = jnp.concatenate([kv[:, 0:KV_W], kv[:, 2 * KV_W:3 * KV_W]], axis=1).T.astype(bf)
        wv_t = jnp.concatenate([kv[:, KV_W:2 * KV_W], kv[:, 3 * KV_W:4 * KV_W]], axis=1).T.astype(bf)
        wq_t = w[:, 4 * KV_W:4 * KV_W + Q_A + Q_B].T.astype(bf)
        wg_t = w[:, 4 * KV_W + Q_A + Q_B:].T.astype(bf)
        qg = jnp.broadcast_to((q_norm_g[l] * (ATTN_SCALE * LOG2E))[:, None], (HEAD_DIM, t))
        kg = jnp.broadcast_to(k_norm_g[l][:, None], (HEAD_DIM, t))

        qa_t, qb_t, ka, kb, va_t, vb_t, gate_t = _in_proj(
            x, ctx, mod1, norm1_g[l][None], wq_t, wk_t, wv_t, wg_t, qg, kg, cos, sin)

        n_key_tiles = (s + n_ctx) // t
        ya_t = _attention_global(ATTN_GLOBAL_TQ, n_key_tiles, qa_t, ka, va_t)
        yb_t = _attention_window(n_key_tiles, sink[l], qb_t, kb, vb_t)

        x1, h2, top_idx, top_w = _merge(
            x, ya_t, yb_t, gate_t, w_br_a[l].T.astype(bf), w_br_b[l].T.astype(bf), w_o[l].astype(bf),
            g1[:b, None, :], mod2, norm2_g[l][None], _split_bf16(w_router[l].T), b_router[l][:, None])

        offs, tok_sorted, w_sorted = _route_lists(top_idx, top_w, moe_tile)
        w1g, w1l = _w1_split(w_e1[l])
        moe_tiles = _moe(offs, tok_sorted, w_sorted, h2, w1g, w1l,
                         b_e1[l][:, None, 0::2], b_e1[l][:, None, 1::2],
                         w_e2[l].astype(bf), b_e2[l][:, None, :], moe_tile)
        out = _final(x1, moe_tiles, g2[:b, None, :], final_g[None], moe_tile)
    return out
```
